```python
import jax, jax.numpy as jnp
from jax import lax
import numpy as np

D_MODEL = 1024
BATCH = 16
SEQ = 2048
DEPTH = 1

HEAD_DIM = 64
SB_HEADS = 8
SB_W = SB_HEADS * HEAD_DIM
CV_W = D_MODEL // 4
CV_GROUPS = CV_W // HEAD_DIM
CONV_K = 31
MX_HEADS = 4
MX_W = MX_HEADS * HEAD_DIM
MIX_W = SB_W + CV_W + MX_W
IN_W = 3 * SB_W + 2 * CV_W + MX_W
N_MEM = 256
Q_BLOCK = 128
N_EXPERTS = 32
TOP_K = 4
D_FF = D_MODEL
SWIGLU_ALPHA = 1.702
SWIGLU_LIMIT = 7.0
MOE_BLOCK = 128
EPS = 1e-6

kernel_name = "hybrid_sb_conformer_memx_moe"


def rms_norm(x, g):
    xf = x.astype(jnp.float32)
    y = xf * lax.rsqrt(jnp.mean(xf * xf, axis=-1, keepdims=True) + EPS)
    return (y * g.astype(jnp.float32)).astype(x.dtype)


def layer_norm(x, g, b):
    xf = x.astype(jnp.float32)
    mu = jnp.mean(xf, axis=-1, keepdims=True)
    var = jnp.mean(jnp.square(xf - mu), axis=-1, keepdims=True)
    y = (xf - mu) * lax.rsqrt(var + EPS)
    return (y * g.astype(jnp.float32) + b.astype(jnp.float32)).astype(x.dtype)


def split_heads(t, n_heads):
    B, S, _ = t.shape
    return t.reshape(B, S, n_heads, -1).transpose(0, 2, 1, 3)


def merge_heads(t):
    B, H, S, Dh = t.shape
    return t.transpose(0, 2, 1, 3).reshape(B, S, H * Dh)


def stick_breaking_attention(q, k, v):
    B, H, S, Dh = q.shape
    scale = Dh ** -0.5
    outs = []
    for qb in range(S // Q_BLOCK):
        t0 = qb * Q_BLOCK
        kv_len = t0 + Q_BLOCK
        qs = q[:, :, t0:kv_len]
        ks = k[:, :, :kv_len]
        vs = v[:, :, :kv_len]
        z = jnp.einsum('bhqd,bhkd->bhqk', qs, ks).astype(jnp.float32) * scale
        t_idx = t0 + jnp.arange(Q_BLOCK)[:, None]
        s_idx = jnp.arange(kv_len)[None, :]
        mask = s_idx < t_idx
        log_beta = jax.nn.log_sigmoid(z)
        log_1m_beta = jnp.where(mask, jax.nn.log_sigmoid(-z), 0.0)
        after = lax.cumsum(log_1m_beta, axis=3, reverse=True) - log_1m_beta
        a = jnp.where(mask, jnp.exp(log_beta + after), 0.0)
        outs.append(jnp.einsum('bhqk,bhkd->bhqd', a.astype(vs.dtype), vs))
    return jnp.concatenate(outs, axis=2)


def conformer_conv(glu_in, b_glu, w_dw, b_dw, g_ln, b_ln, w_pw2, b_pw2):
    a, gt = jnp.split(glu_in + b_glu, 2, axis=-1)
    u = a * jax.nn.sigmoid(gt)
    u = lax.conv_general_dilated(
        u, w_dw.astype(u.dtype), window_strides=(1,), padding=[(CONV_K - 1, 0)],
        dimension_numbers=('NWC', 'WIO', 'NWC'), feature_group_count=CV_W) + b_dw
    u = layer_norm(u, g_ln, b_ln)
    u = jax.nn.silu(u)
    return u @ w_pw2 + b_pw2


def memory_cross_attention(q_mx, mem, g_mem, w_mem_kv):
    mem_n = rms_norm(mem, g_mem)
    k_m, v_m = jnp.split(mem_n @ w_mem_kv, 2, axis=-1)
    q = split_heads(q_mx, MX_HEADS)
    k = split_heads(k_m, MX_HEADS)
    v = split_heads(v_m, MX_HEADS)
    s = jnp.einsum('bhsd,bhmd->bhsm', q, k).astype(jnp.float32) * (HEAD_DIM ** -0.5)
    p = jax.nn.softmax(s, axis=-1).astype(v.dtype)
    return merge_heads(jnp.einsum('bhsm,bhmd->bhsd', p, v))


def hybrid_mixer(xn, mem, w_in, b_glu, w_dw, b_dw, g_cv_ln, b_cv_ln, w_pw2, b_pw2,
                 g_mem, w_mem_kv, g_sb_out, g_cv_out, g_mx_out, w_out):
    proj = xn @ w_in
    q_sb, k_sb, v_sb, glu_in, q_mx = jnp.split(
        proj, [SB_W, 2 * SB_W, 3 * SB_W, 3 * SB_W + 2 * CV_W], axis=-1)
    o_sb = merge_heads(stick_breaking_attention(
        split_heads(q_sb, SB_HEADS), split_heads(k_sb, SB_HEADS), split_heads(v_sb, SB_HEADS)))
    o_cv = conformer_conv(glu_in, b_glu, w_dw, b_dw, g_cv_ln, b_cv_ln, w_pw2, b_pw2)
    o_mx = memory_cross_attention(q_mx, mem, g_mem, w_mem_kv)
    cat = jnp.concatenate([rms_norm(o_sb, g_sb_out), rms_norm(o_cv, g_cv_out),
                           rms_norm(o_mx, g_mx_out)], axis=-1)
    return cat @ w_out


def moe_ffn(xn, w_router, b_router, w_gu, b_gu, w_down, b_down):
    T, D = xn.shape
    logits = (xn @ w_router + b_router).astype(jnp.float32)
    top_val, top_idx = lax.top_k(logits, TOP_K)
    gates = jax.nn.softmax(top_val, axis=-1)
    M = T * TOP_K
    e_flat = top_idx.reshape(M).astype(jnp.int32)
    tok_flat = jnp.broadcast_to(jnp.arange(T, dtype=jnp.int32)[:, None], (T, TOP_K)).reshape(M)
    g_flat = gates.reshape(M)
    order = jnp.argsort(e_flat)
    e_sorted = e_flat[order]
    counts = jnp.bincount(e_flat, length=N_EXPERTS)
    starts = jnp.cumsum(counts) - counts
    padded = ((counts + MOE_BLOCK - 1) // MOE_BLOCK) * MOE_BLOCK
    pends = jnp.cumsum(padded)
    pstarts = pends - padded
    dest = pstarts[e_sorted] + (jnp.arange(M, dtype=jnp.int32) - starts[e_sorted])
    P = (-(-M // MOE_BLOCK)) * MOE_BLOCK + N_EXPERTS * MOE_BLOCK
    NB = P // MOE_BLOCK
    row_tok = jnp.zeros((P,), jnp.int32).at[dest].set(tok_flat[order])
    row_gate = jnp.zeros((P,), jnp.float32).at[dest].set(g_flat[order])
    block_exp = jnp.minimum(
        jnp.searchsorted(pends, jnp.arange(NB) * MOE_BLOCK, side='right'), N_EXPERTS - 1)
    xs = xn[row_tok].reshape(NB, MOE_BLOCK, D)

    def expert_block(args):
        xb, e = args
        gate, up = jnp.split(xb @ w_gu[e] + b_gu[e], 2, axis=-1)
        gate = jnp.minimum(gate, SWIGLU_LIMIT)
        up = jnp.clip(up, -SWIGLU_LIMIT, SWIGLU_LIMIT)
        h = (up + 1.0) * (gate * jax.nn.sigmoid(SWIGLU_ALPHA * gate))
        return h @ w_down[e] + b_down[e]

    ys = lax.map(expert_block, (xs, block_exp)).reshape(P, D)
    ys = ys * row_gate[:, None].astype(ys.dtype)
    return jnp.zeros((T, D), ys.dtype).at[row_tok].add(ys)


def setup_inputs(seed: int = 0) -> dict:
    key = jax.random.key(seed)
    ks = jax.random.split(key, 32)
    f32 = jnp.float32
    L = DEPTH

    def nrm(k, shape, scale):
        return jax.random.normal(k, shape, f32) * scale

    def gain(k, shape):
        return 1.0 + 0.02 * jax.random.normal(k, shape, f32)

    return {
        "x": jax.random.normal(ks[0], (BATCH, SEQ, D_MODEL), f32),
        "mem": jax.random.normal(ks[1], (BATCH, N_MEM, D_MODEL), f32),
        "g_attn_norm": gain(ks[2], (L, D_MODEL)),
        "w_in": nrm(ks[3], (L, D_MODEL, IN_W), D_MODEL ** -0.5),
        "b_glu": nrm(ks[4], (L, 2 * CV_W), 0.02),
        "w_dw": nrm(ks[5], (L, CONV_K, 1, CV_W), CONV_K ** -0.5),
        "b_dw": nrm(ks[6], (L, CV_W), 0.02),
        "g_cv_ln": gain(ks[7], (L, CV_W)),
        "b_cv_ln": nrm(ks[8], (L, CV_W), 0.02),
        "w_pw2": nrm(ks[9], (L, CV_W, CV_W), CV_W ** -0.5),
        "b_pw2": nrm(ks[10], (L, CV_W), 0.02),
        "g_mem": gain(ks[11], (L, D_MODEL)),
        "w_mem_kv": nrm(ks[12], (L, D_MODEL, 2 * MX_W), D_MODEL ** -0.5),
        "g_sb_out": gain(ks[13], (L, SB_W)),
        "g_cv_out": gain(ks[14], (L, CV_W)),
        "g_mx_out": gain(ks[15], (L, MX_W)),
        "w_out": nrm(ks[16], (L, MIX_W, D_MODEL), MIX_W ** -0.5),
        "g_ffn_norm": gain(ks[17], (L, D_MODEL)),
        "w_router": nrm(ks[18], (L, D_MODEL, N_EXPERTS), D_MODEL ** -0.5),
        "b_router": nrm(ks[19], (L, N_EXPERTS), 0.01),
        "w_gu": nrm(ks[20], (L, N_EXPERTS, D_MODEL, 2 * D_FF), D_MODEL ** -0.5),
        "b_gu": nrm(ks[21], (L, N_EXPERTS, 2 * D_FF), 0.02),
        "w_down": nrm(ks[22], (L, N_EXPERTS, D_FF, D_MODEL), D_FF ** -0.5),
        "b_down": nrm(ks[23], (L, N_EXPERTS, D_MODEL), 0.02),
        "g_final": gain(ks[24], (D_MODEL,)),
    }


def reference(x, mem, g_attn_norm, w_in, b_glu, w_dw, b_dw, g_cv_ln, b_cv_ln, w_pw2, b_pw2,
              g_mem, w_mem_kv, g_sb_out, g_cv_out, g_mx_out, w_out, g_ffn_norm,
              w_router, b_router, w_gu, b_gu, w_down, b_down, g_final):
    B, S, D = x.shape
    h = x
    for l in range(DEPTH):
        xn = rms_norm(h, g_attn_norm[l])
        h = h + hybrid_mixer(xn, mem, w_in[l], b_glu[l], w_dw[l], b_dw[l], g_cv_ln[l],
                             b_cv_ln[l], w_pw2[l], b_pw2[l], g_mem[l], w_mem_kv[l],
                             g_sb_out[l], g_cv_out[l], g_mx_out[l], w_out[l])
        hn = rms_norm(h, g_ffn_norm[l]).reshape(B * S, D)
        h = h + moe_ffn(hn, w_router[l], b_router[l], w_gu[l], b_gu[l],
                        w_down[l], b_down[l]).reshape(B, S, D)
    return rms_norm(h, g_final)
```

```python
import functools

import jax
import jax.numpy as jnp
from jax import lax
from jax.experimental import pallas as pl
from jax.experimental.pallas import tpu as pltpu

F32 = jnp.float32
BF16 = jnp.bfloat16

D_MODEL = 1024
HEAD_DIM = 64
SB_W = 512
CV_W = 256
MX_W = 256
IN_W = 3 * SB_W + 2 * CV_W + MX_W
CONV_K = 31
N_MEM = 256
N_EXPERTS = 32
TOP_K = 4
D_FF = 1024
SWIGLU_ALPHA = 1.702
SWIGLU_LIMIT = 7.0
EPS = 1e-6

LANES = 128
ROW_TILE = 512
QB = 128
CONV_PAD = 32
CONV_ROWS = 128
MX_ROWS = 256
MOE_BM = 256
VMEM_LIMIT = 48 * 1024 * 1024


def _rms(x, g):
    return x * lax.rsqrt(jnp.mean(x * x, axis=-1, keepdims=True) + EPS) * g


def _inproj_kernel(x_ref, g_ref, w_ref, o_ref):
    xn = _rms(x_ref[...], g_ref[...]).astype(BF16)
    o_ref[...] = jnp.dot(xn, w_ref[...], preferred_element_type=F32).astype(BF16)


def _inproj(x2, g, w_bf):
    T = x2.shape[0]
    return pl.pallas_call(
        _inproj_kernel,
        grid=(T // ROW_TILE,),
        in_specs=[
            pl.BlockSpec((ROW_TILE, D_MODEL), lambda i: (i, 0)),
            pl.BlockSpec((1, D_MODEL), lambda i: (0, 0)),
            pl.BlockSpec((D_MODEL, IN_W), lambda i: (0, 0)),
        ],
        out_specs=pl.BlockSpec((ROW_TILE, IN_W), lambda i: (i, 0)),
        out_shape=jax.ShapeDtypeStruct((T, IN_W), BF16),
        compiler_params=pltpu.CompilerParams(
            dimension_semantics=("arbitrary",), vmem_limit_bytes=VMEM_LIMIT),
        name="inproj",
    )(x2, g, w_bf)


def _log_sigmoid(z):
    return jnp.minimum(z, 0.0) - jnp.log1p(jnp.exp(-jnp.abs(z)))


def _sb_kernel(q_ref, k_ref, v_ref, o_ref, *, seq):
    scale = HEAD_DIM ** -0.5
    lane = lax.broadcasted_iota(jnp.int32, (QB, LANES), 1)
    row = lax.broadcasted_iota(jnp.int32, (QB, QB), 0)
    col = lax.broadcasted_iota(jnp.int32, (QB, QB), 1)
    tri = jnp.where(row > col, 1.0, 0.0).astype(BF16)
    dmask = col < row

    def suffix_sum(l1m):
        hi = l1m.astype(BF16)
        lo = (l1m - hi.astype(F32)).astype(BF16)
        return (jnp.dot(hi, tri, preferred_element_type=F32)
                + jnp.dot(lo, tri, preferred_element_type=F32))

    def scores(qh, s0):
        kb = k_ref[pl.ds(s0, QB), :]
        return lax.dot_general(qh, kb, (((1,), (1,)), ((), ())),
                               preferred_element_type=F32) * scale

    def qblock(i, _):
        t0 = pl.multiple_of(i * QB, QB)
        q = q_ref[pl.ds(t0, QB), :]
        outs = []
        for h in range(2):
            head = (lane >= HEAD_DIM * h) & (lane < HEAD_DIM * (h + 1))
            qh = jnp.where(head, q, jnp.zeros_like(q))
            z = scores(qh, t0)
            lb = _log_sigmoid(z)
            l1m = jnp.where(dmask, lb - z, 0.0)
            a = jnp.where(dmask, jnp.exp(lb + suffix_sum(l1m)), 0.0)
            acc = jnp.dot(a.astype(BF16), v_ref[pl.ds(t0, QB), :], preferred_element_type=F32)
            c = jnp.sum(l1m, axis=1, keepdims=True)

            def kvblock(jj, carry):
                acc, c = carry
                s0 = pl.multiple_of((i - jj) * QB, QB)
                z = scores(qh, s0)
                lb = _log_sigmoid(z)
                l1m = lb - z
                a = jnp.exp(lb + suffix_sum(l1m) + c)
                acc = acc + jnp.dot(a.astype(BF16), v_ref[pl.ds(s0, QB), :],
                                    preferred_element_type=F32)
                return acc, c + jnp.sum(l1m, axis=1, keepdims=True)

            acc, _ = lax.fori_loop(1, i + 1, kvblock, (acc, c))
            outs.append(acc)
        o_ref[pl.ds(t0, QB), :] = jnp.where(lane < HEAD_DIM, outs[0], outs[1]).astype(BF16)
        return 0

    lax.fori_loop(0, seq // QB, qblock, 0)


def _sb_attention(proj3):
    B, S, _ = proj3.shape
    pairs = SB_W // LANES
    return pl.pallas_call(
        functools.partial(_sb_kernel, seq=S),
        grid=(B, pairs),
        in_specs=[
            pl.BlockSpec((None, S, LANES), lambda b, p: (b, 0, p)),
            pl.BlockSpec((None, S, LANES), lambda b, p: (b, 0, pairs + p)),
            pl.BlockSpec((None, S, LANES), lambda b, p: (b, 0, 2 * pairs + p)),
        ],
        out_specs=pl.BlockSpec((None, S, LANES), lambda b, p: (b, 0, p)),
        out_shape=jax.ShapeDtypeStruct((B, S, SB_W), BF16),
        compiler_params=pltpu.CompilerParams(
            dimension_semantics=("arbitrary", "arbitrary"), vmem_limit_bytes=VMEM_LIMIT),
        name="sb_attention",
    )(proj3, proj3, proj3)


def _conv_kernel(glu_ref, bglu_ref, wdw_ref, bdw_ref, gln_ref, bln_ref, wpw_ref, bpw_ref,
                 o_ref, upad_ref, *, seq):
    upad_ref[0:CONV_PAD, :] = jnp.zeros((CONV_PAD, CV_W), F32)
    for c in range(seq // CONV_ROWS):
        r0 = c * CONV_ROWS
        g = glu_ref[r0:r0 + CONV_ROWS, :].astype(F32) + bglu_ref[...]
        upad_ref[CONV_PAD + r0:CONV_PAD + r0 + CONV_ROWS, :] = (
            g[:, :CV_W] * jax.nn.sigmoid(g[:, CV_W:]))
    for c in range(seq // CONV_ROWS):
        r0 = c * CONV_ROWS
        acc = jnp.zeros((CONV_ROWS, CV_W), F32) + bdw_ref[...]
        for k in range(CONV_K):
            off = CONV_PAD + r0 - (CONV_K - 1) + k
            acc = acc + upad_ref[off:off + CONV_ROWS, :] * wdw_ref[k:k + 1, :]
        mu = jnp.mean(acc, axis=-1, keepdims=True)
        d = acc - mu
        var = jnp.mean(d * d, axis=-1, keepdims=True)
        y = d * lax.rsqrt(var + EPS) * gln_ref[...] + bln_ref[...]
        y = y * jax.nn.sigmoid(y)
        out = jnp.dot(y.astype(BF16), wpw_ref[...], preferred_element_type=F32) + bpw_ref[...]
        o_ref[r0:r0 + CONV_ROWS, :] = out.astype(BF16)


def _conformer(proj3, b_glu, w_dw, b_dw, g_ln, b_ln, w_pw_bf, b_pw):
    B, S, _ = proj3.shape
    glu_block = (3 * SB_W) // (2 * CV_W)
    vec = lambda n: pl.BlockSpec((1, n), lambda b: (0, 0))
    return pl.pallas_call(
        functools.partial(_conv_kernel, seq=S),
        grid=(B,),
        in_specs=[
            pl.BlockSpec((None, S, 2 * CV_W), lambda b: (b, 0, glu_block)),
            vec(2 * CV_W),
            pl.BlockSpec((CONV_K, CV_W), lambda b: (0, 0)),
            vec(CV_W), vec(CV_W), vec(CV_W),
            pl.BlockSpec((CV_W, CV_W), lambda b: (0, 0)),
            vec(CV_W),
        ],
        out_specs=pl.BlockSpec((None, S, CV_W), lambda b: (b, 0, 0)),
        out_shape=jax.ShapeDtypeStruct((B, S, CV_W), BF16),
        scratch_shapes=[pltpu.VMEM((CONV_PAD + S, CV_W), F32)],
        compiler_params=pltpu.CompilerParams(
            dimension_semantics=("arbitrary",), vmem_limit_bytes=VMEM_LIMIT),
        name="conformer",
    )(proj3, b_glu, w_dw, b_dw, g_ln, b_ln, w_pw_bf, b_pw)


def _memx_kernel(mem_ref, gm_ref, wkv_ref, q_ref, o_ref, *, seq):
    scale = HEAD_DIM ** -0.5
    mn = _rms(mem_ref[...], gm_ref[...]).astype(BF16)
    kv = jnp.dot(mn, wkv_ref[...], preferred_element_type=F32)
    km = kv[:, :MX_W].astype(BF16)
    vm = kv[:, MX_W:].astype(BF16)
    lane = lax.broadcasted_iota(jnp.int32, (MX_ROWS, MX_W), 1)

    def chunk(c, _):
        r0 = pl.multiple_of(c * MX_ROWS, MX_ROWS)
        q = q_ref[pl.ds(r0, MX_ROWS), :]
        out = jnp.zeros((MX_ROWS, MX_W), F32)
        for h in range(MX_W // HEAD_DIM):
            head = (lane >= HEAD_DIM * h) & (lane < HEAD_DIM * (h + 1))
            qh = jnp.where(head, q, jnp.zeros_like(q))
            s = lax.dot_general(qh, km, (((1,), (1,)), ((), ())),
                                preferred_element_type=F32) * scale
            p = jnp.exp(s - jnp.max(s, axis=-1, keepdims=True))
            p = p / jnp.sum(p, axis=-1, keepdims=True)
            oh = jnp.dot(p.astype(BF16), vm, preferred_element_type=F32)
            out = jnp.where(head, oh, out)
        o_ref[pl.ds(r0, MX_ROWS), :] = out.astype(BF16)
        return 0

    lax.fori_loop(0, seq // MX_ROWS, chunk, 0)


def _memx(mem, g_mem, w_kv_bf, proj3):
    B, S, _ = proj3.shape
    q_block = (3 * SB_W + 2 * CV_W) // MX_W
    return pl.pallas_call(
        functools.partial(_memx_kernel, seq=S),
        grid=(B,),
        in_specs=[
            pl.BlockSpec((None, N_MEM, D_MODEL), lambda b: (b, 0, 0)),
            pl.BlockSpec((1, D_MODEL), lambda b: (0, 0)),
            pl.BlockSpec((D_MODEL, 2 * MX_W), lambda b: (0, 0)),
            pl.BlockSpec((None, S, MX_W), lambda b: (b, 0, q_block)),
        ],
        out_specs=pl.BlockSpec((None, S, MX_W), lambda b: (b, 0, 0)),
        out_shape=jax.ShapeDtypeStruct((B, S, MX_W), BF16),
        compiler_params=pltpu.CompilerParams(
            dimension_semantics=("arbitrary",), vmem_limit_bytes=VMEM_LIMIT),
        name="memx",
    )(mem, g_mem, w_kv_bf, proj3)


def _outproj_kernel(sb_ref, cv_ref, mx_ref, x_ref, gsb_ref, gcv_ref, gmx_ref, wo_ref,
                    gffn_ref, wr_ref, br_ref, h_ref, hn_ref, idx_ref, gate_ref):
    def normed(o_ref, g_ref):
        return _rms(o_ref[...].astype(F32), g_ref[...]).astype(BF16)

    mix = jnp.dot(normed(sb_ref, gsb_ref), wo_ref[0:SB_W, :], preferred_element_type=F32)
    mix += jnp.dot(normed(cv_ref, gcv_ref), wo_ref[SB_W:SB_W + CV_W, :],
                   preferred_element_type=F32)
    mix += jnp.dot(normed(mx_ref, gmx_ref), wo_ref[SB_W + CV_W:, :],
                   preferred_element_type=F32)
    h = x_ref[...] + mix
    h_ref[...] = h
    hn = _rms(h, gffn_ref[...])
    hn_ref[...] = hn
    logits = jnp.dot(hn, wr_ref[...], preferred_element_type=F32,
                     precision=lax.Precision.HIGHEST) + br_ref[...]
    eid = lax.broadcasted_iota(jnp.int32, logits.shape, 1)
    vals, idxs = [], []
    for _ in range(TOP_K):
        m = jnp.max(logits, axis=-1, keepdims=True)
        i = jnp.min(jnp.where(logits == m, eid, N_EXPERTS), axis=-1, keepdims=True)
        vals.append(m)
        idxs.append(i)
        logits = jnp.where(eid == i, -jnp.inf, logits)
    es = [jnp.exp(v - vals[0]) for v in vals]
    denom = es[0] + es[1] + es[2] + es[3]
    lane = lax.broadcasted_iota(jnp.int32, idx_ref.shape, 1)
    idx_out = jnp.zeros(idx_ref.shape, jnp.int32)
    gate_out = jnp.zeros(gate_ref.shape, F32)
    for k in range(TOP_K):
        idx_out = jnp.where(lane == k, idxs[k], idx_out)
        gate_out = jnp.where(lane == k, es[k] / denom, gate_out)
    idx_ref[...] = idx_out
    gate_ref[...] = gate_out


def _outproj(o_sb, o_cv, o_mx, x2, g_sb, g_cv, g_mx, w_out_bf, g_ffn, w_router, b_router):
    T = x2.shape[0]
    rows = lambda n: pl.BlockSpec((ROW_TILE, n), lambda i: (i, 0))
    full = lambda a, b: pl.BlockSpec((a, b), lambda i: (0, 0))
    return pl.pallas_call(
        _outproj_kernel,
        grid=(T // ROW_TILE,),
        in_specs=[
            rows(SB_W), rows(CV_W), rows(MX_W), rows(D_MODEL),
            full(1, SB_W), full(1, CV_W), full(1, MX_W),
            full(D_MODEL, D_MODEL), full(1, D_MODEL),
            full(D_MODEL, N_EXPERTS), full(1, N_EXPERTS),
        ],
        out_specs=[rows(D_MODEL), rows(D_MODEL), rows(LANES), rows(LANES)],
        out_shape=[
            jax.ShapeDtypeStruct((T, D_MODEL), F32),
            jax.ShapeDtypeStruct((T, D_MODEL), F32),
            jax.ShapeDtypeStruct((T, LANES), jnp.int32),
            jax.ShapeDtypeStruct((T, LANES), F32),
        ],
        compiler_params=pltpu.CompilerParams(
            dimension_semantics=("arbitrary",), vmem_limit_bytes=VMEM_LIMIT),
        name="outproj_router",
    )(o_sb, o_cv, o_mx, x2, g_sb, g_cv, g_mx, w_out_bf, g_ffn, w_router, b_router)


def _row_copy(src_ref, src_row, dst_ref, dst_row, sem):
    return pltpu.make_async_copy(src_ref.at[pl.ds(src_row, 1), :],
                                 dst_ref.at[pl.ds(dst_row, 1), :], sem)


def _experts_kernel(bexp_ref, nvalid_ref, src_ref, dst_ref, hn_ref, wgu_ref, bgu_ref,
                    wd_ref, bd_ref, out_ref, xs_ref, ys_ref, wgu_bf_ref, wd_bf_ref, sem_ref):
    b = pl.program_id(0)

    @pl.when(nvalid_ref[b] > 0)
    def _():
        def gather(r, _):
            _row_copy(hn_ref, src_ref[0, 0, r], xs_ref, r, sem_ref.at[0]).start()
            return 0
        lax.fori_loop(0, MOE_BM, gather, 0)

        prev = jnp.maximum(b - 1, 0)
        @pl.when((b == 0) | (bexp_ref[b] != bexp_ref[prev]))
        def _():
            wgu_bf_ref[...] = wgu_ref[...].astype(BF16)
            wd_bf_ref[...] = wd_ref[...].astype(BF16)

        def gather_wait(r, _):
            _row_copy(hn_ref, 0, xs_ref, r, sem_ref.at[0]).wait()
            return 0
        lax.fori_loop(0, MOE_BM, gather_wait, 0)

        x = xs_ref[...].astype(BF16)
        gate = jnp.dot(x, wgu_bf_ref[:, :D_FF], preferred_element_type=F32) + bgu_ref[:, :D_FF]
        up = jnp.dot(x, wgu_bf_ref[:, D_FF:], preferred_element_type=F32) + bgu_ref[:, D_FF:]
        gate = jnp.minimum(gate, SWIGLU_LIMIT)
        up = jnp.clip(up, -SWIGLU_LIMIT, SWIGLU_LIMIT)
        hmid = (up + 1.0) * (gate * jax.nn.sigmoid(SWIGLU_ALPHA * gate))
        ys_ref[...] = (jnp.dot(hmid.astype(BF16), wd_bf_ref[...], preferred_element_type=F32)
                       + bd_ref[...])

        def scatter(r, _):
            d = dst_ref[0, 0, r]
            @pl.when(d >= 0)
            def _():
                _row_copy(ys_ref, r, out_ref, d, sem_ref.at[1]).start()
            return 0
        lax.fori_loop(0, MOE_BM, scatter, 0)

        def scatter_wait(r, _):
            @pl.when(dst_ref[0, 0, r] >= 0)
            def _():
                _row_copy(ys_ref, r, out_ref, 0, sem_ref.at[1]).wait()
            return 0
        lax.fori_loop(0, MOE_BM, scatter_wait, 0)


def _experts(block_exp, nvalid, src_tok, dst_row, hn, w_gu, b_gu, w_down, b_down):
    T = hn.shape[0]
    NB = block_exp.shape[0]
    idx_spec = pl.BlockSpec((1, 1, MOE_BM), lambda b, be, nv: (b, 0, 0),
                            memory_space=pltpu.SMEM)
    grid_spec = pltpu.PrefetchScalarGridSpec(
        num_scalar_prefetch=2,
        grid=(NB,),
        in_specs=[
            idx_spec, idx_spec,
            pl.BlockSpec(memory_space=pl.ANY),
            pl.BlockSpec((None, D_MODEL, 2 * D_FF), lambda b, be, nv: (be[b], 0, 0)),
            pl.BlockSpec((None, 1, 2 * D_FF), lambda b, be, nv: (be[b], 0, 0)),
            pl.BlockSpec((None, D_FF, D_MODEL), lambda b, be, nv: (be[b], 0, 0)),
            pl.BlockSpec((None, 1, D_MODEL), lambda b, be, nv: (be[b], 0, 0)),
        ],
        out_specs=pl.BlockSpec(memory_space=pl.ANY),
        scratch_shapes=[
            pltpu.VMEM((MOE_BM, D_MODEL), F32),
            pltpu.VMEM((MOE_BM, D_MODEL), F32),
            pltpu.VMEM((D_MODEL, 2 * D_FF), BF16),
            pltpu.VMEM((D_FF, D_MODEL), BF16),
            pltpu.SemaphoreType.DMA((2,)),
        ],
    )
    return pl.pallas_call(
        _experts_kernel,
        grid_spec=grid_spec,
        out_shape=jax.ShapeDtypeStruct((TOP_K * T, D_MODEL), F32),
        compiler_params=pltpu.CompilerParams(
            dimension_semantics=("arbitrary",), vmem_limit_bytes=VMEM_LIMIT),
        name="experts",
    )(block_exp, nvalid, src_tok, dst_row, hn, w_gu, b_gu, w_down, b_down)


def _routing_tables(top_idx):
    T = top_idx.shape[0]
    M = T * TOP_K
    NB = -(-M // MOE_BM) + N_EXPERTS
    e_flat = top_idx.reshape(M)
    order = jnp.argsort(e_flat, stable=True).astype(jnp.int32)
    counts = jnp.sum((e_flat[:, None] == jnp.arange(N_EXPERTS, dtype=jnp.int32)[None, :])
                     .astype(jnp.int32), axis=0)
    starts = jnp.cumsum(counts) - counts
    nblk_e = (counts + MOE_BM - 1) // MOE_BM
    blk_end = jnp.cumsum(nblk_e)
    blk_start = blk_end - nblk_e
    blk = jnp.arange(NB, dtype=jnp.int32)
    used = blk < blk_end[-1]
    bexp = jnp.minimum(jnp.searchsorted(blk_end, blk, side='right'), N_EXPERTS - 1)
    bexp = jnp.where(used, bexp, bexp[jnp.maximum(blk_end[-1] - 1, 0)]).astype(jnp.int32)
    local = (blk - blk_start[bexp]) * MOE_BM
    nvalid = jnp.where(used, jnp.clip(counts[bexp] - local, 0, MOE_BM), 0).astype(jnp.int32)
    r = jnp.arange(MOE_BM, dtype=jnp.int32)[None, :]
    pos = (starts[bexp] + local)[:, None] + r
    valid = r < nvalid[:, None]
    a = order[jnp.clip(pos, 0, M - 1)]
    tok = a // TOP_K
    slot = a % TOP_K
    src_tok = jnp.where(valid, tok, 0).astype(jnp.int32)
    dst_row = jnp.where(valid, slot * T + tok, -1).astype(jnp.int32)
    return bexp, nvalid, src_tok.reshape(NB, 1, MOE_BM), dst_row.reshape(NB, 1, MOE_BM)


def _combine_kernel(h_ref, y0_ref, y1_ref, y2_ref, y3_ref, gate_ref, g_ref, o_ref):
    acc = h_ref[...]
    gates = gate_ref[...]
    for k, y_ref in enumerate((y0_ref, y1_ref, y2_ref, y3_ref)):
        acc = acc + gates[:, k:k + 1] * y_ref[...]
    o_ref[...] = _rms(acc, g_ref[...])


def _combine(h, ys4, gates, g_final):
    T = h.shape[0]
    ys4 = ys4.reshape(TOP_K, T, D_MODEL)
    y_spec = lambda k: pl.BlockSpec((None, ROW_TILE, D_MODEL), lambda i: (k, i, 0))
    return pl.pallas_call(
        _combine_kernel,
        grid=(T // ROW_TILE,),
        in_specs=[
            pl.BlockSpec((ROW_TILE, D_MODEL), lambda i: (i, 0)),
            y_spec(0), y_spec(1), y_spec(2), y_spec(3),
            pl.BlockSpec((ROW_TILE, LANES), lambda i: (i, 0)),
            pl.BlockSpec((1, D_MODEL), lambda i: (0, 0)),
        ],
        out_specs=pl.BlockSpec((ROW_TILE, D_MODEL), lambda i: (i, 0)),
        out_shape=jax.ShapeDtypeStruct((T, D_MODEL), F32),
        compiler_params=pltpu.CompilerParams(
            dimension_semantics=("arbitrary",), vmem_limit_bytes=VMEM_LIMIT),
        name="combine",
    )(h, ys4, ys4, ys4, ys4, gates, g_final)


def kernel(x, mem, g_attn_norm, w_in, b_glu, w_dw, b_dw, g_cv_ln, b_cv_ln, w_pw2, b_pw2, g_mem, w_mem_kv, g_sb_out, g_cv_out, g_mx_out, w_out, g_ffn_norm, w_router, b_router, w_gu, b_gu, w_down, b_down, g_final):
    B, S, D = x.shape
    T = B * S
    assert D == D_MODEL and S % ROW_TILE == 0 and g_attn_norm.shape[0] == 1
    l = 0
    row = lambda v: v.reshape(1, -1)
    x2 = x.reshape(T, D)

    proj = _inproj(x2, row(g_attn_norm[l]), w_in[l].astype(BF16))
    proj3 = proj.reshape(B, S, IN_W)
    o_sb = _sb_attention(proj3)
    o_cv = _conformer(proj3, row(b_glu[l]), w_dw[l].reshape(CONV_K, CV_W), row(b_dw[l]),
                      row(g_cv_ln[l]), row(b_cv_ln[l]), w_pw2[l].astype(BF16), row(b_pw2[l]))
    o_mx = _memx(mem, row(g_mem[l]), w_mem_kv[l].astype(BF16), proj3)

    h, hn, idx_pad, gate_pad = _outproj(
        o_sb.reshape(T, SB_W), o_cv.reshape(T, CV_W), o_mx.reshape(T, MX_W), x2,
        row(g_sb_out[l]), row(g_cv_out[l]), row(g_mx_out[l]), w_out[l].astype(BF16),
        row(g_ffn_norm[l]), w_router[l], row(b_router[l]))

    bexp, nvalid, src_tok, dst_row = _routing_tables(idx_pad[:, :TOP_K])
    ys4 = _experts(bexp, nvalid, src_tok, dst_row, hn, w_gu[l],
                   b_gu[l].reshape(N_EXPERTS, 1, 2 * D_FF), w_down[l],
                   b_down[l].reshape(N_EXPERTS, 1, D_MODEL))
    out = _combine(h, ys4, gate_pad, row(g_final))
    return out.reshape(B, S, D)
```

```python
import functools

import jax
import jax.numpy as jnp
from jax import lax
from jax.experimental import pallas as pl
from jax.experimental.pallas import tpu as pltpu

F32 = jnp.float32
BF16 = jnp.bfloat16

D_MODEL = 1024
HEAD_DIM = 64
SB_W = 512
CV_W = 256
MX_W = 256
IN_W = 3 * SB_W + 2 * CV_W + MX_W
CONV_K = 31
N_MEM = 256
N_EXPERTS = 32
TOP_K = 4
D_FF = 1024
SWIGLU_ALPHA = 1.702
SWIGLU_LIMIT = 7.0
EPS = 1e-6

LANES = 128
ROW_TILE = 512
SB_TILE = 256
SB_EXP_ZERO_BELOW = -104.0
CONV_PAD = 32
CONV_ROWS = 128
MX_ROWS = 256
MOE_BM = 256
DMA_UNROLL = 8
VMEM_LIMIT = 48 * 1024 * 1024


def _rms(x, g):
    return x * lax.rsqrt(jnp.mean(x * x, axis=-1, keepdims=True) + EPS) * g


def _inproj_kernel(x_ref, g_ref, w_ref, o_ref):
    xn = _rms(x_ref[...], g_ref[...]).astype(BF16)
    o_ref[...] = jnp.dot(xn, w_ref[...], preferred_element_type=F32).astype(BF16)


def _inproj(x2, g, w_bf):
    T = x2.shape[0]
    return pl.pallas_call(
        _inproj_kernel,
        grid=(T // ROW_TILE,),
        in_specs=[
            pl.BlockSpec((ROW_TILE, D_MODEL), lambda i: (i, 0)),
            pl.BlockSpec((1, D_MODEL), lambda i: (0, 0)),
            pl.BlockSpec((D_MODEL, IN_W), lambda i: (0, 0)),
        ],
        out_specs=pl.BlockSpec((ROW_TILE, IN_W), lambda i: (i, 0)),
        out_shape=jax.ShapeDtypeStruct((T, IN_W), BF16),
        compiler_params=pltpu.CompilerParams(
            dimension_semantics=("arbitrary",), vmem_limit_bytes=VMEM_LIMIT),
        name="inproj",
    )(x2, g, w_bf)


def _sb_kernel(q_ref, k_ref, v_ref, o_ref, *, seq):
    lane = lax.broadcasted_iota(jnp.int32, (SB_TILE, LANES), 1)
    head0 = lane < HEAD_DIM
    row = lax.broadcasted_iota(jnp.int32, (SB_TILE, SB_TILE), 0)
    col = lax.broadcasted_iota(jnp.int32, (SB_TILE, SB_TILE), 1)
    tri = jnp.where(row > col, 1.0, 0.0).astype(BF16)
    ones = jnp.ones((SB_TILE, LANES), BF16)
    dmask = col < row

    def tile(qh, s0, c, diag):
        kb = k_ref[pl.ds(s0, SB_TILE), :]
        z = lax.dot_general(qh, kb, (((1,), (1,)), ((), ())), preferred_element_type=F32)
        lb = jnp.minimum(z, 0.0) - jnp.log(1.0 + jnp.exp(-jnp.abs(z)))
        l1m = lb - z
        if diag:
            l1m = jnp.where(dmask, l1m, 0.0)
        hi = l1m.astype(BF16)
        lo = (l1m - hi.astype(F32)).astype(BF16)
        after = (jnp.dot(hi, tri, preferred_element_type=F32)
                 + jnp.dot(lo, tri, preferred_element_type=F32))
        rowsum = (jnp.dot(hi, ones, preferred_element_type=F32)
                  + jnp.dot(lo, ones, preferred_element_type=F32))
        arg = lb + after
        if c is not None:
            arg = arg + jnp.concatenate([c] * (SB_TILE // LANES), axis=1)
        a = jnp.exp(arg)
        if diag:
            a = jnp.where(dmask, a, 0.0)
        pv = jnp.dot(a.astype(BF16), v_ref[pl.ds(s0, SB_TILE), :], preferred_element_type=F32)
        return pv, rowsum

    def alive(c0, c1):
        return jnp.max(jnp.maximum(c0, c1)) >= SB_EXP_ZERO_BELOW

    def qtile(i, _):
        t0 = pl.multiple_of(i * SB_TILE, SB_TILE)
        q = q_ref[pl.ds(t0, SB_TILE), :] * (HEAD_DIM ** -0.5)
        q0 = jnp.where(head0, q, jnp.zeros_like(q))
        q1 = jnp.where(head0, jnp.zeros_like(q), q)
        acc0, c0 = tile(q0, t0, None, True)
        acc1, c1 = tile(q1, t0, None, True)

        def cond(st):
            return (st[0] >= 0) & st[5]

        def body(st):
            j, acc0, acc1, c0, c1, _ = st
            s0 = pl.multiple_of(j * SB_TILE, SB_TILE)
            pv0, r0 = tile(q0, s0, c0, False)
            pv1, r1 = tile(q1, s0, c1, False)
            c0 = c0 + r0
            c1 = c1 + r1
            return j - 1, acc0 + pv0, acc1 + pv1, c0, c1, alive(c0, c1)

        st = lax.while_loop(cond, body, (i - 1, acc0, acc1, c0, c1, alive(c0, c1)))
        o_ref[pl.ds(t0, SB_TILE), :] = jnp.where(head0, st[1], st[2]).astype(BF16)
        return 0

    lax.fori_loop(0, seq // SB_TILE, qtile, 0)


def _sb_attention(proj3):
    B, S, _ = proj3.shape
    pairs = SB_W // LANES
    return pl.pallas_call(
        functools.partial(_sb_kernel, seq=S),
        grid=(B, pairs),
        in_specs=[
            pl.BlockSpec((None, S, LANES), lambda b, p: (b, 0, p)),
            pl.BlockSpec((None, S, LANES), lambda b, p: (b, 0, pairs + p)),
            pl.BlockSpec((None, S, LANES), lambda b, p: (b, 0, 2 * pairs + p)),
        ],
        out_specs=pl.BlockSpec((None, S, LANES), lambda b, p: (b, 0, p)),
        out_shape=jax.ShapeDtypeStruct((B, S, SB_W), BF16),
        compiler_params=pltpu.CompilerParams(
            dimension_semantics=("arbitrary", "arbitrary"), vmem_limit_bytes=VMEM_LIMIT),
        name="sb_attention",
    )(proj3, proj3, proj3)


def _conv_kernel(glu_ref, bglu_ref, wdw_ref, bdw_ref, gln_ref, bln_ref, wpw_ref, bpw_ref,
                 o_ref, upad_ref, *, seq):
    upad_ref[0:CONV_PAD, :] = jnp.zeros((CONV_PAD, CV_W), F32)
    for c in range(seq // CONV_ROWS):
        r0 = c * CONV_ROWS
        g = glu_ref[r0:r0 + CONV_ROWS, :].astype(F32) + bglu_ref[...]
        upad_ref[CONV_PAD + r0:CONV_PAD + r0 + CONV_ROWS, :] = (
            g[:, :CV_W] * jax.nn.sigmoid(g[:, CV_W:]))
    for c in range(seq // CONV_ROWS):
        r0 = c * CONV_ROWS
        acc = jnp.zeros((CONV_ROWS, CV_W), F32) + bdw_ref[...]
        for k in range(CONV_K):
            off = CONV_PAD + r0 - (CONV_K - 1) + k
            acc = acc + upad_ref[off:off + CONV_ROWS, :] * wdw_ref[k:k + 1, :]
        mu = jnp.mean(acc, axis=-1, keepdims=True)
        d = acc - mu
        var = jnp.mean(d * d, axis=-1, keepdims=True)
        y = d * lax.rsqrt(var + EPS) * gln_ref[...] + bln_ref[...]
        y = y * jax.nn.sigmoid(y)
        out = jnp.dot(y.astype(BF16), wpw_ref[...], preferred_element_type=F32) + bpw_ref[...]
        o_ref[r0:r0 + CONV_ROWS, :] = out.astype(BF16)


def _conformer(proj3, b_glu, w_dw, b_dw, g_ln, b_ln, w_pw_bf, b_pw):
    B, S, _ = proj3.shape
    glu_block = (3 * SB_W) // (2 * CV_W)
    vec = lambda n: pl.BlockSpec((1, n), lambda b: (0, 0))
    return pl.pallas_call(
        functools.partial(_conv_kernel, seq=S),
        grid=(B,),
        in_specs=[
            pl.BlockSpec((None, S, 2 * CV_W), lambda b: (b, 0, glu_block)),
            vec(2 * CV_W),
            pl.BlockSpec((CONV_K, CV_W), lambda b: (0, 0)),
            vec(CV_W), vec(CV_W), vec(CV_W),
            pl.BlockSpec((CV_W, CV_W), lambda b: (0, 0)),
            vec(CV_W),
        ],
        out_specs=pl.BlockSpec((None, S, CV_W), lambda b: (b, 0, 0)),
        out_shape=jax.ShapeDtypeStruct((B, S, CV_W), BF16),
        scratch_shapes=[pltpu.VMEM((CONV_PAD + S, CV_W), F32)],
        compiler_params=pltpu.CompilerParams(
            dimension_semantics=("arbitrary",), vmem_limit_bytes=VMEM_LIMIT),
        name="conformer",
    )(proj3, b_glu, w_dw, b_dw, g_ln, b_ln, w_pw_bf, b_pw)


def _memx_kernel(mem_ref, gm_ref, wkv_ref, q_ref, o_ref, *, seq):
    scale = HEAD_DIM ** -0.5
    mn = _rms(mem_ref[...], gm_ref[...]).astype(BF16)
    kv = jnp.dot(mn, wkv_ref[...], preferred_element_type=F32)
    km = kv[:, :MX_W].astype(BF16)
    vm = kv[:, MX_W:].astype(BF16)
    lane = lax.broadcasted_iota(jnp.int32, (MX_ROWS, MX_W), 1)

    def chunk(c, _):
        r0 = pl.multiple_of(c * MX_ROWS, MX_ROWS)
        q = q_ref[pl.ds(r0, MX_ROWS), :]
        out = jnp.zeros((MX_ROWS, MX_W), F32)
        for h in range(MX_W // HEAD_DIM):
            head = (lane >= HEAD_DIM * h) & (lane < HEAD_DIM * (h + 1))
            qh = jnp.where(head, q, jnp.zeros_like(q))
            s = lax.dot_general(qh, km, (((1,), (1,)), ((), ())),
                                preferred_element_type=F32) * scale
            p = jnp.exp(s - jnp.max(s, axis=-1, keepdims=True))
            p = p / jnp.sum(p, axis=-1, keepdims=True)
            oh = jnp.dot(p.astype(BF16), vm, preferred_element_type=F32)
            out = jnp.where(head, oh, out)
        o_ref[pl.ds(r0, MX_ROWS), :] = out.astype(BF16)
        return 0

    lax.fori_loop(0, seq // MX_ROWS, chunk, 0)


def _memx(mem, g_mem, w_kv_bf, proj3):
    B, S, _ = proj3.shape
    q_block = (3 * SB_W + 2 * CV_W) // MX_W
    return pl.pallas_call(
        functools.partial(_memx_kernel, seq=S),
        grid=(B,),
        in_specs=[
            pl.BlockSpec((None, N_MEM, D_MODEL), lambda b: (b, 0, 0)),
            pl.BlockSpec((1, D_MODEL), lambda b: (0, 0)),
            pl.BlockSpec((D_MODEL, 2 * MX_W), lambda b: (0, 0)),
            pl.BlockSpec((None, S, MX_W), lambda b: (b, 0, q_block)),
        ],
        out_specs=pl.BlockSpec((None, S, MX_W), lambda b: (b, 0, 0)),
        out_shape=jax.ShapeDtypeStruct((B, S, MX_W), BF16),
        compiler_params=pltpu.CompilerParams(
            dimension_semantics=("arbitrary",), vmem_limit_bytes=VMEM_LIMIT),
        name="memx",
    )(mem, g_mem, w_kv_bf, proj3)


def _outproj_kernel(sb_ref, cv_ref, mx_ref, x_ref, gsb_ref, gcv_ref, gmx_ref, wo_ref,
                    gffn_ref, wr_ref, br_ref, h_ref, hn_ref, idx_ref, gate_ref):
    def normed(o_ref, g_ref):
        return _rms(o_ref[...].astype(F32), g_ref[...]).astype(BF16)

    mix = jnp.dot(normed(sb_ref, gsb_ref), wo_ref[0:SB_W, :], preferred_element_type=F32)
    mix += jnp.dot(normed(cv_ref, gcv_ref), wo_ref[SB_W:SB_W + CV_W, :],
                   preferred_element_type=F32)
    mix += jnp.dot(normed(mx_ref, gmx_ref), wo_ref[SB_W + CV_W:, :],
                   preferred_element_type=F32)
    h = x_ref[...] + mix
    h_ref[...] = h
    hn = _rms(h, gffn_ref[...])
    hn_ref[...] = hn
    logits = jnp.dot(hn, wr_ref[...], preferred_element_type=F32,
                     precision=lax.Precision.HIGHEST) + br_ref[...]
    eid = lax.broadcasted_iota(jnp.int32, logits.shape, 1)
    vals, idxs = [], []
    for _ in range(TOP_K):
        m = jnp.max(logits, axis=-1, keepdims=True)
        i = jnp.min(jnp.where(logits == m, eid, N_EXPERTS), axis=-1, keepdims=True)
        vals.append(m)
        idxs.append(i)
        logits = jnp.where(eid == i, -jnp.inf, logits)
    es = [jnp.exp(v - vals[0]) for v in vals]
    denom = es[0] + es[1] + es[2] + es[3]
    lane = lax.broadcasted_iota(jnp.int32, idx_ref.shape, 1)
    idx_out = jnp.zeros(idx_ref.shape, jnp.int32)
    gate_out = jnp.zeros(gate_ref.shape, F32)
    for k in range(TOP_K):
        idx_out = jnp.where(lane == k, idxs[k], idx_out)
        gate_out = jnp.where(lane == k, es[k] / denom, gate_out)
    idx_ref[...] = idx_out
    gate_ref[...] = gate_out


def _outproj(o_sb, o_cv, o_mx, x2, g_sb, g_cv, g_mx, w_out_bf, g_ffn, w_router, b_router):
    T = x2.shape[0]
    rows = lambda n: pl.BlockSpec((ROW_TILE, n), lambda i: (i, 0))
    full = lambda a, b: pl.BlockSpec((a, b), lambda i: (0, 0))
    return pl.pallas_call(
        _outproj_kernel,
        grid=(T // ROW_TILE,),
        in_specs=[
            rows(SB_W), rows(CV_W), rows(MX_W), rows(D_MODEL),
            full(1, SB_W), full(1, CV_W), full(1, MX_W),
            full(D_MODEL, D_MODEL), full(1, D_MODEL),
            full(D_MODEL, N_EXPERTS), full(1, N_EXPERTS),
        ],
        out_specs=[rows(D_MODEL), rows(D_MODEL), rows(LANES), rows(LANES)],
        out_shape=[
            jax.ShapeDtypeStruct((T, D_MODEL), F32),
            jax.ShapeDtypeStruct((T, D_MODEL), F32),
            jax.ShapeDtypeStruct((T, LANES), jnp.int32),
            jax.ShapeDtypeStruct((T, LANES), F32),
        ],
        compiler_params=pltpu.CompilerParams(
            dimension_semantics=("arbitrary",), vmem_limit_bytes=VMEM_LIMIT),
        name="outproj_router",
    )(o_sb, o_cv, o_mx, x2, g_sb, g_cv, g_mx, w_out_bf, g_ffn, w_router, b_router)


def _row_copy(src_ref, src_row, dst_ref, dst_row, sem):
    return pltpu.make_async_copy(src_ref.at[pl.ds(src_row, 1), :],
                                 dst_ref.at[pl.ds(dst_row, 1), :], sem)


def _experts_kernel(bexp_ref, nvalid_ref, src_ref, src_next_ref, dst_ref, hn_ref, wgu_ref,
                    bgu_ref, wd_ref, bd_ref, out_ref, xs_ref, ys_ref, wgu_bf_ref, wd_bf_ref,
                    gsem_ref, ssem_ref):
    b = pl.program_id(0)
    nb = pl.num_programs(0)
    slot = lax.rem(b, 2)
    nv = nvalid_ref[b]
    has_next = (b + 1 < nb) & (nvalid_ref[jnp.minimum(b + 1, nb - 1)] > 0)

    def start_gather(idx_ref, s):
        def group(g, _):
            r0 = pl.multiple_of(g * DMA_UNROLL, DMA_UNROLL)
            for u in range(DMA_UNROLL):
                _row_copy(hn_ref, idx_ref[0, 0, r0 + u], xs_ref.at[s], r0 + u,
                          gsem_ref.at[s]).start()
            return 0
        lax.fori_loop(0, MOE_BM // DMA_UNROLL, group, 0)

    def wait_scatter(s, n):
        p = MOE_BM
        while p >= 1:
            @pl.when((n & p) != 0)
            def _(p=p):
                pltpu.make_async_copy(ys_ref.at[s, pl.ds(0, p), :],
                                      out_ref.at[pl.ds(0, p), :], ssem_ref.at[s]).wait()
            p //= 2

    @pl.when(b == 0)
    def _():
        start_gather(src_ref, 0)

    @pl.when(has_next)
    def _():
        start_gather(src_next_ref, 1 - slot)

    @pl.when(nv > 0)
    def _():
        prev = jnp.maximum(b - 1, 0)
        @pl.when((b == 0) | (bexp_ref[b] != bexp_ref[prev]))
        def _():
            wgu_bf_ref[...] = wgu_ref[...].astype(BF16)
            wd_bf_ref[...] = wd_ref[...].astype(BF16)

        pltpu.make_async_copy(hn_ref.at[pl.ds(0, MOE_BM), :], xs_ref.at[slot],
                              gsem_ref.at[slot]).wait()

        x = xs_ref[slot].astype(BF16)
        gate = jnp.dot(x, wgu_bf_ref[:, :D_FF], preferred_element_type=F32) + bgu_ref[:, :D_FF]
        up = jnp.dot(x, wgu_bf_ref[:, D_FF:], preferred_element_type=F32) + bgu_ref[:, D_FF:]
        gate = jnp.minimum(gate, SWIGLU_LIMIT)
        up = jnp.clip(up, -SWIGLU_LIMIT, SWIGLU_LIMIT)
        hmid = (up + 1.0) * (gate * jax.nn.sigmoid(SWIGLU_ALPHA * gate))
        ys_ref[slot] = (jnp.dot(hmid.astype(BF16), wd_bf_ref[...], preferred_element_type=F32)
                        + bd_ref[...])

        def scatter_group(g, _):
            r0 = pl.multiple_of(g * DMA_UNROLL, DMA_UNROLL)
            for u in range(DMA_UNROLL):
                @pl.when(r0 + u < nv)
                def _(u=u):
                    _row_copy(ys_ref.at[slot], r0 + u, out_ref, dst_ref[0, 0, r0 + u],
                              ssem_ref.at[slot]).start()
            return 0
        lax.fori_loop(0, MOE_BM // DMA_UNROLL, scatter_group, 0)

        @pl.when(b >= 1)
        def _():
            wait_scatter(1 - slot, nvalid_ref[prev])

        @pl.when(jnp.logical_not(has_next))
        def _():
            wait_scatter(slot, nv)


def _experts(block_exp, nvalid, src_tok, dst_row, hn, w_gu, b_gu, w_down, b_down):
    T = hn.shape[0]
    NB = block_exp.shape[0]
    idx_spec = pl.BlockSpec((1, 1, MOE_BM), lambda b, be, nv: (b, 0, 0),
                            memory_space=pltpu.SMEM)
    idx_next_spec = pl.BlockSpec((1, 1, MOE_BM),
                                 lambda b, be, nv: (jnp.minimum(b + 1, NB - 1), 0, 0),
                                 memory_space=pltpu.SMEM)
    grid_spec = pltpu.PrefetchScalarGridSpec(
        num_scalar_prefetch=2,
        grid=(NB,),
        in_specs=[
            idx_spec, idx_next_spec, idx_spec,
            pl.BlockSpec(memory_space=pl.ANY),
            pl.BlockSpec((None, D_MODEL, 2 * D_FF), lambda b, be, nv: (be[b], 0, 0)),
            pl.BlockSpec((None, 1, 2 * D_FF), lambda b, be, nv: (be[b], 0, 0)),
            pl.BlockSpec((None, D_FF, D_MODEL), lambda b, be, nv: (be[b], 0, 0)),
            pl.BlockSpec((None, 1, D_MODEL), lambda b, be, nv: (be[b], 0, 0)),
        ],
        out_specs=pl.BlockSpec(memory_space=pl.ANY),
        scratch_shapes=[
            pltpu.VMEM((2, MOE_BM, D_MODEL), F32),
            pltpu.VMEM((2, MOE_BM, D_MODEL), F32),
            pltpu.VMEM((D_MODEL, 2 * D_FF), BF16),
            pltpu.VMEM((D_FF, D_MODEL), BF16),
            pltpu.SemaphoreType.DMA((2,)),
            pltpu.SemaphoreType.DMA((2,)),
        ],
    )
    return pl.pallas_call(
        _experts_kernel,
        grid_spec=grid_spec,
        out_shape=jax.ShapeDtypeStruct((TOP_K * T, D_MODEL), F32),
        compiler_params=pltpu.CompilerParams(
            dimension_semantics=("arbitrary",), vmem_limit_bytes=VMEM_LIMIT),
        name="experts",
    )(block_exp, nvalid, src_tok, src_tok, dst_row, hn, w_gu, b_gu, w_down, b_down)


def _routing_tables(top_idx):
    T = top_idx.shape[0]
    M = T * TOP_K
    NB = -(-M // MOE_BM) + N_EXPERTS
    e_flat = top_idx.reshape(M)
    order = jnp.argsort(e_flat, stable=True).astype(jnp.int32)
    counts = jnp.sum((e_flat[:, None] == jnp.arange(N_EXPERTS, dtype=jnp.int32)[None, :])
                     .astype(jnp.int32), axis=0)
    starts = jnp.cumsum(counts) - counts
    nblk_e = (counts + MOE_BM - 1) // MOE_BM
    blk_end = jnp.cumsum(nblk_e)
    blk_start = blk_end - nblk_e
    blk = jnp.arange(NB, dtype=jnp.int32)
    used = blk < blk_end[-1]
    bexp = jnp.minimum(jnp.searchsorted(blk_end, blk, side='right'), N_EXPERTS - 1)
    bexp = jnp.where(used, bexp, bexp[jnp.maximum(blk_end[-1] - 1, 0)]).astype(jnp.int32)
    local = (blk - blk_start[bexp]) * MOE_BM
    nvalid = jnp.where(used, jnp.clip(counts[bexp] - local, 0, MOE_BM), 0).astype(jnp.int32)
    r = jnp.arange(MOE_BM, dtype=jnp.int32)[None, :]
    pos = (starts[bexp] + local)[:, None] + r
    valid = r < nvalid[:, None]
    a = order[jnp.clip(pos, 0, M - 1)]
    tok = a // TOP_K
    slot = a % TOP_K
    src_tok = jnp.where(valid, tok, 0).astype(jnp.int32)
    dst_row = jnp.where(valid, slot * T + tok, -1).astype(jnp.int32)
    return bexp, nvalid, src_tok.reshape(NB, 1, MOE_BM), dst_row.reshape(NB, 1, MOE_BM)


def _combine_kernel(h_ref, y0_ref, y1_ref, y2_ref, y3_ref, gate_ref, g_ref, o_ref):
    acc = h_ref[...]
    gates = gate_ref[...]
    for k, y_ref in enumerate((y0_ref, y1_ref, y2_ref, y3_ref)):
        acc = acc + gates[:, k:k + 1] * y_ref[...]
    o_ref[...] = _rms(acc, g_ref[...])


def _combine(h, ys4, gates, g_final):
    T = h.shape[0]
    ys4 = ys4.reshape(TOP_K, T, D_MODEL)
    y_spec = lambda k: pl.BlockSpec((None, ROW_TILE, D_MODEL), lambda i: (k, i, 0))
    return pl.pallas_call(
        _combine_kernel,
        grid=(T // ROW_TILE,),
        in_specs=[
            pl.BlockSpec((ROW_TILE, D_MODEL), lambda i: (i, 0)),
            y_spec(0), y_spec(1), y_spec(2), y_spec(3),
            pl.BlockSpec((ROW_TILE, LANES), lambda i: (i, 0)),
            pl.BlockSpec((1, D_MODEL), lambda i: (0, 0)),
        ],
        out_specs=pl.BlockSpec((ROW_TILE, D_MODEL), lambda i: (i, 0)),
        out_shape=jax.ShapeDtypeStruct((T, D_MODEL), F32),
        compiler_params=pltpu.CompilerParams(
            dimension_semantics=("arbitrary",), vmem_limit_bytes=VMEM_LIMIT),
        name="combine",
    )(h, ys4, ys4, ys4, ys4, gates, g_final)


def kernel(x, mem, g_attn_norm, w_in, b_glu, w_dw, b_dw, g_cv_ln, b_cv_ln, w_pw2, b_pw2, g_mem, w_mem_kv, g_sb_out, g_cv_out, g_mx_out, w_out, g_ffn_norm, w_router, b_router, w_gu, b_gu, w_down, b_down, g_final):
    B, S, D = x.shape
    T = B * S
    assert D == D_MODEL and S % ROW_TILE == 0 and g_attn_norm.shape[0] == 1
    l = 0
    row = lambda v: v.reshape(1, -1)
    x2 = x.reshape(T, D)

    proj = _inproj(x2, row(g_attn_norm[l]), w_in[l].astype(BF16))
    proj3 = proj.reshape(B, S, IN_W)
    o_sb = _sb_attention(proj3)
    o_cv = _conformer(proj3, row(b_glu[l]), w_dw[l].reshape(CONV_K, CV_W), row(b_dw[l]),
                      row(g_cv_ln[l]), row(b_cv_ln[l]), w_pw2[l].astype(BF16), row(b_pw2[l]))
    o_mx = _memx(mem, row(g_mem[l]), w_mem_kv[l].astype(BF16), proj3)

    h, hn, idx_pad, gate_pad = _outproj(
        o_sb.reshape(T, SB_W), o_cv.reshape(T, CV_W), o_mx.reshape(T, MX_W), x2,
        row(g_sb_out[l]), row(g_cv_out[l]), row(g_mx_out[l]), w_out[l].astype(BF16),
        row(g_ffn_norm[l]), w_router[l], row(b_router[l]))

    bexp, nvalid, src_tok, dst_row = _routing_tables(idx_pad[:, :TOP_K])
    ys4 = _experts(bexp, nvalid, src_tok, dst_row, hn, w_gu[l],
                   b_gu[l].reshape(N_EXPERTS, 1, 2 * D_FF), w_down[l],
                   b_down[l].reshape(N_EXPERTS, 1, D_MODEL))
    out = _combine(h, ys4, gate_pad, row(g_final))
    return out.reshape(B, S, D)
```

```python
import functools

import jax
import jax.numpy as jnp
from jax import lax
from jax.experimental import pallas as pl
from jax.experimental.pallas import tpu as pltpu

F32 = jnp.float32
BF16 = jnp.bfloat16

D_MODEL = 1024
HEAD_DIM = 64
SB_W = 512
CV_W = 256
MX_W = 256
IN_W = 3 * SB_W + 2 * CV_W + MX_W
CONV_K = 31
N_MEM = 256
N_EXPERTS = 32
TOP_K = 4
D_FF = 1024
SWIGLU_ALPHA = 1.702
SWIGLU_LIMIT = 7.0
EPS = 1e-6

LANES = 128
SUBLANES = 8
ROW_TILE = 512
SB_TILE = 256
SB_EXP_ZERO_BELOW = -104.0
CONV_PAD = 32
CONV_ROWS = 128
MX_ROWS = 256
MOE_BM = 256
COMBINE_TILE = 256
DMA_UNROLL = 8
VMEM_LIMIT = 48 * 1024 * 1024


def _rms(x, g):
    return x * lax.rsqrt(jnp.mean(x * x, axis=-1, keepdims=True) + EPS) * g


def _inproj_kernel(x_ref, g_ref, w_ref, o_ref):
    xn = _rms(x_ref[...], g_ref[...]).astype(BF16)
    o_ref[...] = jnp.dot(xn, w_ref[...], preferred_element_type=F32).astype(BF16)


def _inproj(x2, g, w_bf):
    T = x2.shape[0]
    return pl.pallas_call(
        _inproj_kernel,
        grid=(T // ROW_TILE,),
        in_specs=[
            pl.BlockSpec((ROW_TILE, D_MODEL), lambda i: (i, 0)),
            pl.BlockSpec((1, D_MODEL), lambda i: (0, 0)),
            pl.BlockSpec((D_MODEL, IN_W), lambda i: (0, 0)),
        ],
        out_specs=pl.BlockSpec((ROW_TILE, IN_W), lambda i: (i, 0)),
        out_shape=jax.ShapeDtypeStruct((T, IN_W), BF16),
        compiler_params=pltpu.CompilerParams(
            dimension_semantics=("arbitrary",), vmem_limit_bytes=VMEM_LIMIT),
        name="inproj",
    )(x2, g, w_bf)


def _sb_kernel(q_ref, k_ref, v_ref, o_ref, *, seq):
    lane = lax.broadcasted_iota(jnp.int32, (SB_TILE, LANES), 1)
    head0 = lane < HEAD_DIM
    row = lax.broadcasted_iota(jnp.int32, (SB_TILE, SB_TILE), 0)
    col = lax.broadcasted_iota(jnp.int32, (SB_TILE, SB_TILE), 1)
    tri = jnp.where(row > col, 1.0, 0.0).astype(BF16)
    ones = jnp.ones((SB_TILE, LANES), BF16)
    dmask = col < row

    def tile(qh, s0, c, diag):
        kb = k_ref[pl.ds(s0, SB_TILE), :]
        z = lax.dot_general(qh, kb, (((1,), (1,)), ((), ())), preferred_element_type=F32)
        lb = jnp.minimum(z, 0.0) - jnp.log(1.0 + jnp.exp(-jnp.abs(z)))
        l1m = lb - z
        if diag:
            l1m = jnp.where(dmask, l1m, 0.0)
        hi = l1m.astype(BF16)
        lo = (l1m - hi.astype(F32)).astype(BF16)
        after = (jnp.dot(hi, tri, preferred_element_type=F32)
                 + jnp.dot(lo, tri, preferred_element_type=F32))
        rowsum = (jnp.dot(hi, ones, preferred_element_type=F32)
                  + jnp.dot(lo, ones, preferred_element_type=F32))
        arg = lb + after
        if c is not None:
            arg = arg + jnp.concatenate([c] * (SB_TILE // LANES), axis=1)
        a = jnp.exp(arg)
        if diag:
            a = jnp.where(dmask, a, 0.0)
        pv = jnp.dot(a.astype(BF16), v_ref[pl.ds(s0, SB_TILE), :], preferred_element_type=F32)
        return pv, rowsum

    def alive(c0, c1):
        return jnp.max(jnp.maximum(c0, c1)) >= SB_EXP_ZERO_BELOW

    def qtile(i, _):
        t0 = pl.multiple_of(i * SB_TILE, SB_TILE)
        q = q_ref[pl.ds(t0, SB_TILE), :] * (HEAD_DIM ** -0.5)
        q0 = jnp.where(head0, q, jnp.zeros_like(q))
        q1 = jnp.where(head0, jnp.zeros_like(q), q)
        acc0, c0 = tile(q0, t0, None, True)
        acc1, c1 = tile(q1, t0, None, True)

        def cond(st):
            return (st[0] >= 0) & st[5]

        def body(st):
            j, acc0, acc1, c0, c1, _ = st
            s0 = pl.multiple_of(j * SB_TILE, SB_TILE)
            pv0, r0 = tile(q0, s0, c0, False)
            pv1, r1 = tile(q1, s0, c1, False)
            c0 = c0 + r0
            c1 = c1 + r1
            return j - 1, acc0 + pv0, acc1 + pv1, c0, c1, alive(c0, c1)

        st = lax.while_loop(cond, body, (i - 1, acc0, acc1, c0, c1, alive(c0, c1)))
        o_ref[pl.ds(t0, SB_TILE), :] = jnp.where(head0, st[1], st[2]).astype(BF16)
        return 0

    lax.fori_loop(0, seq // SB_TILE, qtile, 0)


def _sb_attention(proj3):
    B, S, _ = proj3.shape
    pairs = SB_W // LANES
    return pl.pallas_call(
        functools.partial(_sb_kernel, seq=S),
        grid=(B, pairs),
        in_specs=[
            pl.BlockSpec((None, S, LANES), lambda b, p: (b, 0, p)),
            pl.BlockSpec((None, S, LANES), lambda b, p: (b, 0, pairs + p)),
            pl.BlockSpec((None, S, LANES), lambda b, p: (b, 0, 2 * pairs + p)),
        ],
        out_specs=pl.BlockSpec((None, S, LANES), lambda b, p: (b, 0, p)),
        out_shape=jax.ShapeDtypeStruct((B, S, SB_W), BF16),
        compiler_params=pltpu.CompilerParams(
            dimension_semantics=("arbitrary", "arbitrary"), vmem_limit_bytes=VMEM_LIMIT),
        name="sb_attention",
    )(proj3, proj3, proj3)


def _conv_kernel(glu_ref, bglu_ref, wdw_ref, bdw_ref, gln_ref, bln_ref, wpw_ref, bpw_ref,
                 o_ref, upad_ref, *, seq):
    upad_ref[0:CONV_PAD, :] = jnp.zeros((CONV_PAD, CV_W), F32)
    for c in range(seq // CONV_ROWS):
        r0 = c * CONV_ROWS
        g = glu_ref[r0:r0 + CONV_ROWS, :].astype(F32) + bglu_ref[...]
        upad_ref[CONV_PAD + r0:CONV_PAD + r0 + CONV_ROWS, :] = (
            g[:, :CV_W] * jax.nn.sigmoid(g[:, CV_W:]))
    for c in range(seq // CONV_ROWS):
        r0 = c * CONV_ROWS
        acc = jnp.zeros((CONV_ROWS, CV_W), F32) + bdw_ref[...]
        for k in range(CONV_K):
            off = CONV_PAD + r0 - (CONV_K - 1) + k
            acc = acc + upad_ref[off:off + CONV_ROWS, :] * wdw_ref[k:k + 1, :]
        mu = jnp.mean(acc, axis=-1, keepdims=True)
        d = acc - mu
        var = jnp.mean(d * d, axis=-1, keepdims=True)
        y = d * lax.rsqrt(var + EPS) * gln_ref[...] + bln_ref[...]
        y = y * jax.nn.sigmoid(y)
        out = jnp.dot(y.astype(BF16), wpw_ref[...], preferred_element_type=F32) + bpw_ref[...]
        o_ref[r0:r0 + CONV_ROWS, :] = out.astype(BF16)


def _conformer(proj3, b_glu, w_dw, b_dw, g_ln, b_ln, w_pw_bf, b_pw):
    B, S, _ = proj3.shape
    glu_block = (3 * SB_W) // (2 * CV_W)
    vec = lambda n: pl.BlockSpec((1, n), lambda b: (0, 0))
    return pl.pallas_call(
        functools.partial(_conv_kernel, seq=S),
        grid=(B,),
        in_specs=[
            pl.BlockSpec((None, S, 2 * CV_W), lambda b: (b, 0, glu_block)),
            vec(2 * CV_W),
            pl.BlockSpec((CONV_K, CV_W), lambda b: (0, 0)),
            vec(CV_W), vec(CV_W), vec(CV_W),
            pl.BlockSpec((CV_W, CV_W), lambda b: (0, 0)),
            vec(CV_W),
        ],
        out_specs=pl.BlockSpec((None, S, CV_W), lambda b: (b, 0, 0)),
        out_shape=jax.ShapeDtypeStruct((B, S, CV_W), BF16),
        scratch_shapes=[pltpu.VMEM((CONV_PAD + S, CV_W), F32)],
        compiler_params=pltpu.CompilerParams(
            dimension_semantics=("arbitrary",), vmem_limit_bytes=VMEM_LIMIT),
        name="conformer",
    )(proj3, b_glu, w_dw, b_dw, g_ln, b_ln, w_pw_bf, b_pw)


def _memx_kernel(mem_ref, gm_ref, wkv_ref, q_ref, o_ref, *, seq):
    scale = HEAD_DIM ** -0.5
    mn = _rms(mem_ref[...], gm_ref[...]).astype(BF16)
    kv = jnp.dot(mn, wkv_ref[...], preferred_element_type=F32)
    km = kv[:, :MX_W].astype(BF16)
    vm = kv[:, MX_W:].astype(BF16)
    lane = lax.broadcasted_iota(jnp.int32, (MX_ROWS, MX_W), 1)

    def chunk(c, _):
        r0 = pl.multiple_of(c * MX_ROWS, MX_ROWS)
        q = q_ref[pl.ds(r0, MX_ROWS), :]
        out = jnp.zeros((MX_ROWS, MX_W), F32)
        for h in range(MX_W // HEAD_DIM):
            head = (lane >= HEAD_DIM * h) & (lane < HEAD_DIM * (h + 1))
            qh = jnp.where(head, q, jnp.zeros_like(q))
            s = lax.dot_general(qh, km, (((1,), (1,)), ((), ())),
                                preferred_element_type=F32) * scale
            p = jnp.exp(s - jnp.max(s, axis=-1, keepdims=True))
            p = p / jnp.sum(p, axis=-1, keepdims=True)
            oh = jnp.dot(p.astype(BF16), vm, preferred_element_type=F32)
            out = jnp.where(head, oh, out)
        o_ref[pl.ds(r0, MX_ROWS), :] = out.astype(BF16)
        return 0

    lax.fori_loop(0, seq // MX_ROWS, chunk, 0)


def _memx(mem, g_mem, w_kv_bf, proj3):
    B, S, _ = proj3.shape
    q_block = (3 * SB_W + 2 * CV_W) // MX_W
    return pl.pallas_call(
        functools.partial(_memx_kernel, seq=S),
        grid=(B,),
        in_specs=[
            pl.BlockSpec((None, N_MEM, D_MODEL), lambda b: (b, 0, 0)),
            pl.BlockSpec((1, D_MODEL), lambda b: (0, 0)),
            pl.BlockSpec((D_MODEL, 2 * MX_W), lambda b: (0, 0)),
            pl.BlockSpec((None, S, MX_W), lambda b: (b, 0, q_block)),
        ],
        out_specs=pl.BlockSpec((None, S, MX_W), lambda b: (b, 0, 0)),
        out_shape=jax.ShapeDtypeStruct((B, S, MX_W), BF16),
        compiler_params=pltpu.CompilerParams(
            dimension_semantics=("arbitrary",), vmem_limit_bytes=VMEM_LIMIT),
        name="memx",
    )(mem, g_mem, w_kv_bf, proj3)


def _outproj_kernel(sb_ref, cv_ref, mx_ref, x_ref, gsb_ref, gcv_ref, gmx_ref, wo_ref,
                    gffn_ref, wr_ref, br_ref, h_ref, hn3_ref, idx_ref, rank_ref, gate_ref,
                    cnt_ref, run_ref):
    def normed(o_ref, g_ref):
        return _rms(o_ref[...].astype(F32), g_ref[...]).astype(BF16)

    mix = jnp.dot(normed(sb_ref, gsb_ref), wo_ref[0:SB_W, :], preferred_element_type=F32)
    mix += jnp.dot(normed(cv_ref, gcv_ref), wo_ref[SB_W:SB_W + CV_W, :],
                   preferred_element_type=F32)
    mix += jnp.dot(normed(mx_ref, gmx_ref), wo_ref[SB_W + CV_W:, :],
                   preferred_element_type=F32)
    h = x_ref[...] + mix
    h_ref[...] = h
    hn = _rms(h, gffn_ref[...])
    for s in range(D_MODEL // LANES):
        hn3_ref[pl.ds(s, ROW_TILE, stride=SUBLANES), :] = hn[:, s * LANES:(s + 1) * LANES]
    wr = wr_ref[...]
    hn_hi = hn.astype(BF16)
    hn_lo = (hn - hn_hi.astype(F32)).astype(BF16)
    wr_hi = wr.astype(BF16)
    wr_lo = (wr - wr_hi.astype(F32)).astype(BF16)
    logits = (jnp.dot(hn_hi, wr_hi, preferred_element_type=F32)
              + jnp.dot(hn_lo, wr_hi, preferred_element_type=F32)
              + jnp.dot(hn_hi, wr_lo, preferred_element_type=F32)) + br_ref[...]
    eid = lax.broadcasted_iota(jnp.int32, logits.shape, 1)
    vals, idxs = [], []
    for _ in range(TOP_K):
        m = jnp.max(logits, axis=-1, keepdims=True)
        i = jnp.min(jnp.where(logits == m, eid, N_EXPERTS), axis=-1, keepdims=True)
        vals.append(m)
        idxs.append(i)
        logits = jnp.where(eid == i, -jnp.inf, logits)
    es = [jnp.exp(v - vals[0]) for v in vals]
    denom = es[0] + es[1] + es[2] + es[3]

    @pl.when(pl.program_id(0) == 0)
    def _():
        run_ref[...] = jnp.zeros_like(run_ref)
    tm = logits.shape[0]
    row = lax.broadcasted_iota(jnp.int32, (tm, tm), 0)
    col = lax.broadcasted_iota(jnp.int32, (tm, tm), 1)
    before = jnp.where(col < row, 1.0, 0.0).astype(BF16)
    base = run_ref[...]
    ranks = []
    for k in range(TOP_K):
        onehot = jnp.where(eid == idxs[k], 1.0, 0.0)
        prefix = jnp.dot(before, onehot.astype(BF16), preferred_element_type=F32)
        ranks.append(jnp.sum(onehot * (prefix + base), axis=-1, keepdims=True))
        base = base + jnp.sum(onehot, axis=0, keepdims=True)
    run_ref[...] = base
    cnt_ref[...] = jnp.broadcast_to(base, cnt_ref.shape)

    lane = lax.broadcasted_iota(jnp.int32, idx_ref.shape, 1)
    idx_out = jnp.zeros(idx_ref.shape, jnp.int32)
    rank_out = jnp.zeros(rank_ref.shape, jnp.int32)
    gate_out = jnp.zeros(gate_ref.shape, F32)
    for k in range(TOP_K):
        idx_out = jnp.where(lane == k, idxs[k], idx_out)
        rank_out = jnp.where(lane == k, ranks[k].astype(jnp.int32), rank_out)
        gate_out = jnp.where(lane == k, es[k] / denom, gate_out)
    idx_ref[...] = idx_out
    rank_ref[...] = rank_out
    gate_ref[...] = gate_out


def _outproj(o_sb, o_cv, o_mx, x2, g_sb, g_cv, g_mx, w_out_bf, g_ffn, w_router, b_router):
    T = x2.shape[0]
    rows = lambda n: pl.BlockSpec((ROW_TILE, n), lambda i: (i, 0))
    full = lambda a, b: pl.BlockSpec((a, b), lambda i: (0, 0))
    return pl.pallas_call(
        _outproj_kernel,
        grid=(T // ROW_TILE,),
        in_specs=[
            rows(SB_W), rows(CV_W), rows(MX_W), rows(D_MODEL),
            full(1, SB_W), full(1, CV_W), full(1, MX_W),
            full(D_MODEL, D_MODEL), full(1, D_MODEL),
            full(D_MODEL, N_EXPERTS), full(1, N_EXPERTS),
        ],
        out_specs=[rows(D_MODEL),
                   pl.BlockSpec((ROW_TILE * SUBLANES, LANES), lambda i: (i, 0)),
                   rows(LANES), rows(LANES), rows(LANES),
                   full(SUBLANES, N_EXPERTS)],
        out_shape=[
            jax.ShapeDtypeStruct((T, D_MODEL), F32),
            jax.ShapeDtypeStruct((T * SUBLANES, LANES), F32),
            jax.ShapeDtypeStruct((T, LANES), jnp.int32),
            jax.ShapeDtypeStruct((T, LANES), jnp.int32),
            jax.ShapeDtypeStruct((T, LANES), F32),
            jax.ShapeDtypeStruct((SUBLANES, N_EXPERTS), F32),
        ],
        scratch_shapes=[pltpu.VMEM((1, N_EXPERTS), F32)],
        compiler_params=pltpu.CompilerParams(
            dimension_semantics=("arbitrary",), vmem_limit_bytes=VMEM_LIMIT),
        name="outproj_router",
    )(o_sb, o_cv, o_mx, x2, g_sb, g_cv, g_mx, w_out_bf, g_ffn, w_router, b_router)


def _rows_copy(src_ref, src_row, dst_ref, dst_row, n, sem):
    src = src_ref.at[pl.ds(pl.multiple_of(src_row * SUBLANES, SUBLANES), n * SUBLANES), :]
    dst = dst_ref.at[pl.ds(pl.multiple_of(dst_row * SUBLANES, SUBLANES), n * SUBLANES), :]
    return pltpu.make_async_copy(src, dst, sem)


def _dispatch_kernel(fill_start_ref, fill_rows_ref, dest_ref, hn3_ref, xs3_ref,
                     zero_ref, sem_ref, fill_sem_ref, *, tm):
    i = pl.program_id(0)
    n = pl.num_programs(0)
    slot = lax.rem(i, 2)
    n_copies = tm * TOP_K
    pieces = [1 << p for p in range(MOE_BM.bit_length())]

    def fill(wait):
        for e in range(2 * N_EXPERTS):
            rows = fill_rows_ref[e]
            start = fill_start_ref[e]
            for p in pieces:
                @pl.when((rows & p) != 0)
                def _(p=p, start=start, rows=rows):
                    cp = _rows_copy(zero_ref, 0, xs3_ref, start + (rows & (p - 1)), p,
                                    fill_sem_ref.at[0])
                    if wait:
                        cp.wait()
                    else:
                        cp.start()

    @pl.when(i == 0)
    def _():
        zero_ref[...] = jnp.zeros_like(zero_ref)
        fill(wait=False)

    t0 = i * tm

    def group(g, _):
        m0 = pl.multiple_of(g * DMA_UNROLL, DMA_UNROLL)
        dests = [dest_ref[0, 0, m0 + u] for u in range(DMA_UNROLL)]
        for u in range(DMA_UNROLL):
            tok = t0 + g * (DMA_UNROLL // TOP_K) + u // TOP_K
            _rows_copy(hn3_ref, tok, xs3_ref, dests[u], 1, sem_ref.at[slot]).start()
        return 0
    lax.fori_loop(0, n_copies // DMA_UNROLL, group, 0)

    @pl.when(i == 0)
    def _():
        fill(wait=True)

    @pl.when(i >= 1)
    def _():
        _rows_copy(hn3_ref, 0, xs3_ref, 0, n_copies, sem_ref.at[1 - slot]).wait()

    @pl.when(i == n - 1)
    def _():
        _rows_copy(hn3_ref, 0, xs3_ref, 0, n_copies, sem_ref.at[slot]).wait()


def _dispatch(fill_start, fill_rows, dest, hn3, n_rows):
    T = hn3.shape[0] // SUBLANES
    tm = ROW_TILE
    grid_spec = pltpu.PrefetchScalarGridSpec(
        num_scalar_prefetch=2,
        grid=(T // tm,),
        in_specs=[
            pl.BlockSpec((1, 1, tm * TOP_K), lambda i, fs, fr: (i, 0, 0),
                         memory_space=pltpu.SMEM),
            pl.BlockSpec(memory_space=pl.ANY),
        ],
        out_specs=pl.BlockSpec(memory_space=pl.ANY),
        scratch_shapes=[
            pltpu.VMEM((MOE_BM * SUBLANES, LANES), F32),
            pltpu.SemaphoreType.DMA((2,)),
            pltpu.SemaphoreType.DMA((1,)),
        ],
    )
    return pl.pallas_call(
        functools.partial(_dispatch_kernel, tm=tm),
        grid_spec=grid_spec,
        out_shape=jax.ShapeDtypeStruct((n_rows * SUBLANES, LANES), F32),
        compiler_params=pltpu.CompilerParams(
            dimension_semantics=("arbitrary",), vmem_limit_bytes=VMEM_LIMIT),
        name="dispatch",
    )(fill_start, fill_rows, dest.reshape(T // tm, 1, tm * TOP_K), hn3)


def _experts_kernel(bexp_ref, nused_ref, x_ref, wgu_ref, bgu_ref, wd_ref, bd_ref, y_ref,
                    xb_ref, wgu_bf_ref, wd_bf_ref):
    b = pl.program_id(0)

    @pl.when(b >= nused_ref[0])
    def _():
        y_ref[...] = jnp.zeros_like(y_ref)

    @pl.when(b < nused_ref[0])
    def _():
        prev = jnp.maximum(b - 1, 0)
        @pl.when((b == 0) | (bexp_ref[b] != bexp_ref[prev]))
        def _():
            wgu_bf_ref[...] = wgu_ref[...].astype(BF16)
            wd_bf_ref[...] = wd_ref[...].astype(BF16)

        for s in range(SUBLANES):
            xb_ref[:, s * LANES:(s + 1) * LANES] = (
                x_ref[pl.ds(s, MOE_BM, stride=SUBLANES), :].astype(BF16))
        x = xb_ref[...]
        gate = jnp.dot(x, wgu_bf_ref[:, :D_FF], preferred_element_type=F32) + bgu_ref[:, :D_FF]
        up = jnp.dot(x, wgu_bf_ref[:, D_FF:], preferred_element_type=F32) + bgu_ref[:, D_FF:]
        gate = jnp.minimum(gate, SWIGLU_LIMIT)
        up = jnp.clip(up, -SWIGLU_LIMIT, SWIGLU_LIMIT)
        hmid = (up + 1.0) * (gate * jax.nn.sigmoid(SWIGLU_ALPHA * gate))
        y = jnp.dot(hmid.astype(BF16), wd_bf_ref[...], preferred_element_type=F32) + bd_ref[...]
        for s in range(SUBLANES):
            y_ref[pl.ds(s, MOE_BM, stride=SUBLANES), :] = y[:, s * LANES:(s + 1) * LANES]


def _experts(block_exp, n_used, xs3, w_gu, b_gu, w_down, b_down):
    NB = block_exp.shape[0]
    used = lambda b, nu: jnp.minimum(b, nu[0] - 1)
    grid_spec = pltpu.PrefetchScalarGridSpec(
        num_scalar_prefetch=2,
        grid=(NB,),
        in_specs=[
            pl.BlockSpec((MOE_BM * SUBLANES, LANES), lambda b, be, nu: (used(b, nu), 0)),
            pl.BlockSpec((None, D_MODEL, 2 * D_FF), lambda b, be, nu: (be[b], 0, 0)),
            pl.BlockSpec((None, 1, 2 * D_FF), lambda b, be, nu: (be[b], 0, 0)),
            pl.BlockSpec((None, D_FF, D_MODEL), lambda b, be, nu: (be[b], 0, 0)),
            pl.BlockSpec((None, 1, D_MODEL), lambda b, be, nu: (be[b], 0, 0)),
        ],
        out_specs=pl.BlockSpec((MOE_BM * SUBLANES, LANES), lambda b, be, nu: (b, 0)),
        scratch_shapes=[
            pltpu.VMEM((MOE_BM, D_MODEL), BF16),
            pltpu.VMEM((D_MODEL, 2 * D_FF), BF16),
            pltpu.VMEM((D_FF, D_MODEL), BF16),
        ],
    )
    return pl.pallas_call(
        _experts_kernel,
        grid_spec=grid_spec,
        out_shape=jax.ShapeDtypeStruct(xs3.shape, F32),
        compiler_params=pltpu.CompilerParams(
            dimension_semantics=("arbitrary",), vmem_limit_bytes=VMEM_LIMIT),
        name="experts",
    )(block_exp, n_used, xs3, w_gu, b_gu, w_down, b_down)


def _routing_tables(counts, top_idx, rank):
    M = top_idx.shape[0] * TOP_K
    NB = -(-M // MOE_BM) + N_EXPERTS
    nblk_e = (counts + MOE_BM - 1) // MOE_BM
    blk_end = jnp.cumsum(nblk_e)
    row_start = (blk_end - nblk_e) * MOE_BM
    n_used = blk_end[-1]
    blk = jnp.minimum(jnp.arange(NB, dtype=jnp.int32), n_used - 1)
    bexp = jnp.sum((blk[:, None] >= blk_end[None, :]).astype(jnp.int32), axis=1)
    experts = jnp.arange(N_EXPERTS, dtype=jnp.int32)
    dest = rank + jnp.sum(jnp.where(top_idx[:, :, None] == experts, row_start, 0), axis=-1)
    tail = NB - N_EXPERTS + experts
    fill_start = jnp.concatenate([row_start + counts, tail * MOE_BM])
    fill_rows = jnp.concatenate([nblk_e * MOE_BM - counts,
                                 jnp.where(tail >= n_used, MOE_BM, 0)])
    return (bexp.astype(jnp.int32), n_used.reshape(1).astype(jnp.int32),
            dest.astype(jnp.int32), fill_start.astype(jnp.int32),
            fill_rows.astype(jnp.int32), NB * MOE_BM)


def _combine_kernel(dest_ref, dest_next_ref, h_ref, gate_ref, g_ref, ys3_ref, o_ref,
                    buf_ref, sem_ref, *, tm):
    i = pl.program_id(0)
    n = pl.num_programs(0)
    slot = lax.rem(i, 2)

    def start_gather(idx_ref, s):
        def group(g, _):
            m0 = pl.multiple_of(g * DMA_UNROLL, DMA_UNROLL)
            srcs = [idx_ref[0, 0, m0 + u] for u in range(DMA_UNROLL)]
            for u in range(DMA_UNROLL):
                r = g * (DMA_UNROLL // TOP_K) + u // TOP_K
                _rows_copy(ys3_ref, srcs[u], buf_ref.at[s], (u % TOP_K) * tm + r, 1,
                           sem_ref.at[s]).start()
            return 0
        lax.fori_loop(0, tm * TOP_K // DMA_UNROLL, group, 0)

    @pl.when(i == 0)
    def _():
        start_gather(dest_ref, 0)

    @pl.when(i + 1 < n)
    def _():
        start_gather(dest_next_ref, 1 - slot)

    _rows_copy(ys3_ref, 0, buf_ref.at[slot], 0, tm * TOP_K, sem_ref.at[slot]).wait()

    gates = gate_ref[...]
    chunks = []
    ssq = jnp.zeros((tm, 1), F32)
    for s in range(SUBLANES):
        acc = h_ref[:, s * LANES:(s + 1) * LANES]
        for k in range(TOP_K):
            acc = acc + gates[:, k:k + 1] * buf_ref[
                slot, pl.ds(k * tm * SUBLANES + s, tm, stride=SUBLANES), :]
        chunks.append(acc)
        ssq = ssq + jnp.sum(acc * acc, axis=-1, keepdims=True)
    inv = lax.rsqrt(ssq * (1.0 / D_MODEL) + EPS)
    for s in range(SUBLANES):
        o_ref[:, s * LANES:(s + 1) * LANES] = chunks[s] * inv * g_ref[:, s * LANES:(s + 1) * LANES]


def _combine(h, ys3, dest, gates, g_final):
    T = h.shape[0]
    tm = COMBINE_TILE
    n = T // tm
    table = dest.reshape(n, 1, tm * TOP_K)
    idx_spec = lambda f: pl.BlockSpec((1, 1, tm * TOP_K), lambda i: (f(i), 0, 0),
                                      memory_space=pltpu.SMEM)
    return pl.pallas_call(
        functools.partial(_combine_kernel, tm=tm),
        grid=(n,),
        in_specs=[
            idx_spec(lambda i: i), idx_spec(lambda i: jnp.minimum(i + 1, n - 1)),
            pl.BlockSpec((tm, D_MODEL), lambda i: (i, 0)),
            pl.BlockSpec((tm, LANES), lambda i: (i, 0)),
            pl.BlockSpec((1, D_MODEL), lambda i: (0, 0)),
            pl.BlockSpec(memory_space=pl.ANY),
        ],
        out_specs=pl.BlockSpec((tm, D_MODEL), lambda i: (i, 0)),
        out_shape=jax.ShapeDtypeStruct((T, D_MODEL), F32),
        scratch_shapes=[
            pltpu.VMEM((2, tm * TOP_K * SUBLANES, LANES), F32),
            pltpu.SemaphoreType.DMA((2,)),
        ],
        compiler_params=pltpu.CompilerParams(
            dimension_semantics=("arbitrary",), vmem_limit_bytes=VMEM_LIMIT),
        name="combine",
    )(table, table, h, gates, g_final, ys3)


def kernel(x, mem, g_attn_norm, w_in, b_glu, w_dw, b_dw, g_cv_ln, b_cv_ln, w_pw2, b_pw2, g_mem, w_mem_kv, g_sb_out, g_cv_out, g_mx_out, w_out, g_ffn_norm, w_router, b_router, w_gu, b_gu, w_down, b_down, g_final):
    B, S, D = x.shape
    T = B * S
    assert D == D_MODEL and S % ROW_TILE == 0 and g_attn_norm.shape[0] == 1
    l = 0
    row = lambda v: v.reshape(1, -1)
    x2 = x.reshape(T, D)

    proj = _inproj(x2, row(g_attn_norm[l]), w_in[l].astype(BF16))
    proj3 = proj.reshape(B, S, IN_W)
    o_sb = _sb_attention(proj3)
    o_cv = _conformer(proj3, row(b_glu[l]), w_dw[l].reshape(CONV_K, CV_W), row(b_dw[l]),
                      row(g_cv_ln[l]), row(b_cv_ln[l]), w_pw2[l].astype(BF16), row(b_pw2[l]))
    o_mx = _memx(mem, row(g_mem[l]), w_mem_kv[l].astype(BF16), proj3)

    h, hn3, idx_pad, rank_pad, gate_pad, counts = _outproj(
        o_sb.reshape(T, SB_W), o_cv.reshape(T, CV_W), o_mx.reshape(T, MX_W), x2,
        row(g_sb_out[l]), row(g_cv_out[l]), row(g_mx_out[l]), w_out[l].astype(BF16),
        row(g_ffn_norm[l]), w_router[l], row(b_router[l]))

    bexp, n_used, dest, fill_start, fill_rows, n_rows = _routing_tables(
        counts[0].astype(jnp.int32), idx_pad[:, :TOP_K], rank_pad[:, :TOP_K])
    xs3 = _dispatch(fill_start, fill_rows, dest, hn3, n_rows)
    ys3 = _experts(bexp, n_used, xs3, w_gu[l], b_gu[l].reshape(N_EXPERTS, 1, 2 * D_FF),
                   w_down[l], b_down[l].reshape(N_EXPERTS, 1, D_MODEL))
    out = _combine(h, ys3, dest, gate_pad, row(g_final))
    return out.reshape(B, S, D)
```

```python
import functools

import jax
import jax.numpy as jnp
from jax import lax
from jax.experimental import pallas as pl
from jax.experimental.pallas import tpu as pltpu

F32 = jnp.float32
BF16 = jnp.bfloat16

D_MODEL = 1024
HEAD_DIM = 64
SB_W = 512
CV_W = 256
MX_W = 256
IN_W = 3 * SB_W + 2 * CV_W + MX_W
CONV_K = 31
N_MEM = 256
N_EXPERTS = 32
TOP_K = 4
D_FF = 1024
SWIGLU_ALPHA = 1.702
SWIGLU_LIMIT = 7.0
EPS = 1e-6

LANES = 128
SUBLANES = 8
ROW_TILE = 512
SB_TILE = 256
SB_EXP_ZERO_BELOW = -104.0
CONV_PAD = 32
CONV_ROWS = 128
MX_ROWS = 256
MOE_BM = 512
FF_CHUNK = 512
COMBINE_TILE = 256
DMA_UNROLL = 8
VMEM_LIMIT = 56 * 1024 * 1024


def _rms(x, g):
    return x * lax.rsqrt(jnp.mean(x * x, axis=-1, keepdims=True) + EPS) * g


def _inproj_kernel(x_ref, g_ref, w_ref, o_ref):
    xn = _rms(x_ref[...], g_ref[...]).astype(BF16)
    o_ref[...] = jnp.dot(xn, w_ref[...], preferred_element_type=F32).astype(BF16)


def _inproj(x2, g, w_bf):
    T = x2.shape[0]
    return pl.pallas_call(
        _inproj_kernel,
        grid=(T // ROW_TILE,),
        in_specs=[
            pl.BlockSpec((ROW_TILE, D_MODEL), lambda i: (i, 0)),
            pl.BlockSpec((1, D_MODEL), lambda i: (0, 0)),
            pl.BlockSpec((D_MODEL, IN_W), lambda i: (0, 0)),
        ],
        out_specs=pl.BlockSpec((ROW_TILE, IN_W), lambda i: (i, 0)),
        out_shape=jax.ShapeDtypeStruct((T, IN_W), BF16),
        compiler_params=pltpu.CompilerParams(
            dimension_semantics=("arbitrary",), vmem_limit_bytes=VMEM_LIMIT),
        name="inproj",
    )(x2, g, w_bf)


def _sb_kernel(q_ref, k_ref, v_ref, o_ref, *, seq):
    lane = lax.broadcasted_iota(jnp.int32, (SB_TILE, LANES), 1)
    head0 = lane < HEAD_DIM
    row = lax.broadcasted_iota(jnp.int32, (SB_TILE, SB_TILE), 0)
    col = lax.broadcasted_iota(jnp.int32, (SB_TILE, SB_TILE), 1)
    tri = jnp.where(row > col, 1.0, 0.0).astype(BF16)
    dmask = col < row

    def tile(qh, s0, c, diag):
        kb = k_ref[pl.ds(s0, SB_TILE), :]
        z = lax.dot_general(qh, kb, (((1,), (1,)), ((), ())), preferred_element_type=F32)
        lb = jnp.minimum(z, 0.0) - jnp.log(1.0 + jnp.exp(-jnp.abs(z)))
        l1m = lb - z
        if diag:
            l1m = jnp.where(dmask, l1m, 0.0)
        after = jnp.dot(l1m.astype(BF16), tri, preferred_element_type=F32)
        rowsum = jnp.broadcast_to(jnp.sum(l1m, axis=1, keepdims=True), (SB_TILE, LANES))
        arg = lb + after
        if c is not None:
            arg = arg + jnp.concatenate([c] * (SB_TILE // LANES), axis=1)
        a = jnp.exp(arg)
        if diag:
            a = jnp.where(dmask, a, 0.0)
        pv = jnp.dot(a.astype(BF16), v_ref[pl.ds(s0, SB_TILE), :], preferred_element_type=F32)
        return pv, rowsum

    def alive(c0, c1):
        return jnp.max(jnp.maximum(c0, c1)) >= SB_EXP_ZERO_BELOW

    def qtile(i, _):
        t0 = pl.multiple_of(i * SB_TILE, SB_TILE)
        q = q_ref[pl.ds(t0, SB_TILE), :] * (HEAD_DIM ** -0.5)
        q0 = jnp.where(head0, q, jnp.zeros_like(q))
        q1 = jnp.where(head0, jnp.zeros_like(q), q)
        acc0, c0 = tile(q0, t0, None, True)
        acc1, c1 = tile(q1, t0, None, True)

        def cond(st):
            return (st[0] >= 0) & st[5]

        def body(st):
            j, acc0, acc1, c0, c1, _ = st
            s0 = pl.multiple_of(j * SB_TILE, SB_TILE)
            pv0, r0 = tile(q0, s0, c0, False)
            pv1, r1 = tile(q1, s0, c1, False)
            c0 = c0 + r0
            c1 = c1 + r1
            return j - 1, acc0 + pv0, acc1 + pv1, c0, c1, alive(c0, c1)

        st = lax.while_loop(cond, body, (i - 1, acc0, acc1, c0, c1, alive(c0, c1)))
        o_ref[pl.ds(t0, SB_TILE), :] = jnp.where(head0, st[1], st[2]).astype(BF16)
        return 0

    lax.fori_loop(0, seq // SB_TILE, qtile, 0)


def _sb_attention(proj3):
    B, S, _ = proj3.shape
    pairs = SB_W // LANES
    return pl.pallas_call(
        functools.partial(_sb_kernel, seq=S),
        grid=(B, pairs),
        in_specs=[
            pl.BlockSpec((None, S, LANES), lambda b, p: (b, 0, p)),
            pl.BlockSpec((None, S, LANES), lambda b, p: (b, 0, pairs + p)),
            pl.BlockSpec((None, S, LANES), lambda b, p: (b, 0, 2 * pairs + p)),
        ],
        out_specs=pl.BlockSpec((None, S, LANES), lambda b, p: (b, 0, p)),
        out_shape=jax.ShapeDtypeStruct((B, S, SB_W), BF16),
        compiler_params=pltpu.CompilerParams(
            dimension_semantics=("arbitrary", "arbitrary"), vmem_limit_bytes=VMEM_LIMIT),
        name="sb_attention",
    )(proj3, proj3, proj3)


def _conv_kernel(glu_ref, bglu_ref, wdw_ref, bdw_ref, gln_ref, bln_ref, wpw_ref, bpw_ref,
                 o_ref, upad_ref, *, seq):
    upad_ref[0:CONV_PAD, :] = jnp.zeros((CONV_PAD, CV_W), F32)
    for c in range(seq // CONV_ROWS):
        r0 = c * CONV_ROWS
        g = glu_ref[r0:r0 + CONV_ROWS, :].astype(F32) + bglu_ref[...]
        upad_ref[CONV_PAD + r0:CONV_PAD + r0 + CONV_ROWS, :] = (
            g[:, :CV_W] * jax.nn.sigmoid(g[:, CV_W:]))
    for c in range(seq // CONV_ROWS):
        r0 = c * CONV_ROWS
        acc = jnp.zeros((CONV_ROWS, CV_W), F32) + bdw_ref[...]
        for k in range(CONV_K):
            off = CONV_PAD + r0 - (CONV_K - 1) + k
            acc = acc + upad_ref[off:off + CONV_ROWS, :] * wdw_ref[k:k + 1, :]
        mu = jnp.mean(acc, axis=-1, keepdims=True)
        d = acc - mu
        var = jnp.mean(d * d, axis=-1, keepdims=True)
        y = d * lax.rsqrt(var + EPS) * gln_ref[...] + bln_ref[...]
        y = y * jax.nn.sigmoid(y)
        out = jnp.dot(y.astype(BF16), wpw_ref[...], preferred_element_type=F32) + bpw_ref[...]
        o_ref[r0:r0 + CONV_ROWS, :] = out.astype(BF16)


def _conformer(proj3, b_glu, w_dw, b_dw, g_ln, b_ln, w_pw_bf, b_pw):
    B, S, _ = proj3.shape
    glu_block = (3 * SB_W) // (2 * CV_W)
    vec = lambda n: pl.BlockSpec((1, n), lambda b: (0, 0))
    return pl.pallas_call(
        functools.partial(_conv_kernel, seq=S),
        grid=(B,),
        in_specs=[
            pl.BlockSpec((None, S, 2 * CV_W), lambda b: (b, 0, glu_block)),
            vec(2 * CV_W),
            pl.BlockSpec((CONV_K, CV_W), lambda b: (0, 0)),
            vec(CV_W), vec(CV_W), vec(CV_W),
            pl.BlockSpec((CV_W, CV_W), lambda b: (0, 0)),
            vec(CV_W),
        ],
        out_specs=pl.BlockSpec((None, S, CV_W), lambda b: (b, 0, 0)),
        out_shape=jax.ShapeDtypeStruct((B, S, CV_W), BF16),
        scratch_shapes=[pltpu.VMEM((CONV_PAD + S, CV_W), F32)],
        compiler_params=pltpu.CompilerParams(
            dimension_semantics=("arbitrary",), vmem_limit_bytes=VMEM_LIMIT),
        name="conformer",
    )(proj3, b_glu, w_dw, b_dw, g_ln, b_ln, w_pw_bf, b_pw)


def _memx_kernel(mem_ref, gm_ref, wkv_ref, q_ref, o_ref, *, seq):
    scale = HEAD_DIM ** -0.5
    mn = _rms(mem_ref[...], gm_ref[...]).astype(BF16)
    kv = jnp.dot(mn, wkv_ref[...], preferred_element_type=F32)
    km = kv[:, :MX_W].astype(BF16)
    vm = kv[:, MX_W:].astype(BF16)
    lane = lax.broadcasted_iota(jnp.int32, (MX_ROWS, MX_W), 1)

    def chunk(c, _):
        r0 = pl.multiple_of(c * MX_ROWS, MX_ROWS)
        q = q_ref[pl.ds(r0, MX_ROWS), :]
        out = jnp.zeros((MX_ROWS, MX_W), F32)
        for h in range(MX_W // HEAD_DIM):
            head = (lane >= HEAD_DIM * h) & (lane < HEAD_DIM * (h + 1))
            qh = jnp.where(head, q, jnp.zeros_like(q))
            s = lax.dot_general(qh, km, (((1,), (1,)), ((), ())),
                                preferred_element_type=F32) * scale
            p = jnp.exp(s - jnp.max(s, axis=-1, keepdims=True))
            p = p / jnp.sum(p, axis=-1, keepdims=True)
            oh = jnp.dot(p.astype(BF16), vm, preferred_element_type=F32)
            out = jnp.where(head, oh, out)
        o_ref[pl.ds(r0, MX_ROWS), :] = out.astype(BF16)
        return 0

    lax.fori_loop(0, seq // MX_ROWS, chunk, 0)


def _memx(mem, g_mem, w_kv_bf, proj3):
    B, S, _ = proj3.shape
    q_block = (3 * SB_W + 2 * CV_W) // MX_W
    return pl.pallas_call(
        functools.partial(_memx_kernel, seq=S),
        grid=(B,),
        in_specs=[
            pl.BlockSpec((None, N_MEM, D_MODEL), lambda b: (b, 0, 0)),
            pl.BlockSpec((1, D_MODEL), lambda b: (0, 0)),
            pl.BlockSpec((D_MODEL, 2 * MX_W), lambda b: (0, 0)),
            pl.BlockSpec((None, S, MX_W), lambda b: (b, 0, q_block)),
        ],
        out_specs=pl.BlockSpec((None, S, MX_W), lambda b: (b, 0, 0)),
        out_shape=jax.ShapeDtypeStruct((B, S, MX_W), BF16),
        compiler_params=pltpu.CompilerParams(
            dimension_semantics=("arbitrary",), vmem_limit_bytes=VMEM_LIMIT),
        name="memx",
    )(mem, g_mem, w_kv_bf, proj3)


def _outproj_kernel(sb_ref, cv_ref, mx_ref, x_ref, gsb_ref, gcv_ref, gmx_ref, wo_ref,
                    gffn_ref, wr_ref, br_ref, h_ref, hn3_ref, idx_ref, rank_ref, gate_ref,
                    cnt_ref, run_ref):
    def normed(o_ref, g_ref):
        return _rms(o_ref[...].astype(F32), g_ref[...]).astype(BF16)

    mix = jnp.dot(normed(sb_ref, gsb_ref), wo_ref[0:SB_W, :], preferred_element_type=F32)
    mix += jnp.dot(normed(cv_ref, gcv_ref), wo_ref[SB_W:SB_W + CV_W, :],
                   preferred_element_type=F32)
    mix += jnp.dot(normed(mx_ref, gmx_ref), wo_ref[SB_W + CV_W:, :],
                   preferred_element_type=F32)
    h = x_ref[...] + mix
    h_ref[...] = h
    hn = _rms(h, gffn_ref[...])
    for s in range(D_MODEL // LANES):
        hn3_ref[pl.ds(s, ROW_TILE, stride=SUBLANES), :] = hn[:, s * LANES:(s + 1) * LANES]
    wr = wr_ref[...]
    hn_hi = hn.astype(BF16)
    hn_lo = (hn - hn_hi.astype(F32)).astype(BF16)
    wr_hi = wr.astype(BF16)
    wr_lo = (wr - wr_hi.astype(F32)).astype(BF16)
    logits = (jnp.dot(hn_hi, wr_hi, preferred_element_type=F32)
              + jnp.dot(hn_lo, wr_hi, preferred_element_type=F32)
              + jnp.dot(hn_hi, wr_lo, preferred_element_type=F32)) + br_ref[...]
    eid = lax.broadcasted_iota(jnp.int32, logits.shape, 1)
    vals, idxs = [], []
    for _ in range(TOP_K):
        m = jnp.max(logits, axis=-1, keepdims=True)
        i = jnp.min(jnp.where(logits == m, eid, N_EXPERTS), axis=-1, keepdims=True)
        vals.append(m)
        idxs.append(i)
        logits = jnp.where(eid == i, -jnp.inf, logits)
    es = [jnp.exp(v - vals[0]) for v in vals]
    denom = es[0] + es[1] + es[2] + es[3]

    @pl.when(pl.program_id(0) == 0)
    def _():
        run_ref[...] = jnp.zeros_like(run_ref)
    tm = logits.shape[0]
    row = lax.broadcasted_iota(jnp.int32, (tm, tm), 0)
    col = lax.broadcasted_iota(jnp.int32, (tm, tm), 1)
    before = jnp.where(col < row, 1.0, 0.0).astype(BF16)
    base = run_ref[...]
    ranks = []
    for k in range(TOP_K):
        onehot = jnp.where(eid == idxs[k], 1.0, 0.0)
        prefix = jnp.dot(before, onehot.astype(BF16), preferred_element_type=F32)
        ranks.append(jnp.sum(onehot * (prefix + base), axis=-1, keepdims=True))
        base = base + jnp.sum(onehot, axis=0, keepdims=True)
    run_ref[...] = base
    cnt_ref[...] = jnp.broadcast_to(base, cnt_ref.shape)

    lane = lax.broadcasted_iota(jnp.int32, idx_ref.shape, 1)
    idx_out = jnp.zeros(idx_ref.shape, jnp.int32)
    rank_out = jnp.zeros(rank_ref.shape, jnp.int32)
    gate_out = jnp.zeros(gate_ref.shape, F32)
    for k in range(TOP_K):
        idx_out = jnp.where(lane == k, idxs[k], idx_out)
        rank_out = jnp.where(lane == k, ranks[k].astype(jnp.int32), rank_out)
        gate_out = jnp.where(lane == k, es[k] / denom, gate_out)
    idx_ref[...] = idx_out
    rank_ref[...] = rank_out
    gate_ref[...] = gate_out


def _outproj(o_sb, o_cv, o_mx, x2, g_sb, g_cv, g_mx, w_out_bf, g_ffn, w_router, b_router):
    T = x2.shape[0]
    rows = lambda n: pl.BlockSpec((ROW_TILE, n), lambda i: (i, 0))
    full = lambda a, b: pl.BlockSpec((a, b), lambda i: (0, 0))
    return pl.pallas_call(
        _outproj_kernel,
        grid=(T // ROW_TILE,),
        in_specs=[
            rows(SB_W), rows(CV_W), rows(MX_W), rows(D_MODEL),
            full(1, SB_W), full(1, CV_W), full(1, MX_W),
            full(D_MODEL, D_MODEL), full(1, D_MODEL),
            full(D_MODEL, N_EXPERTS), full(1, N_EXPERTS),
        ],
        out_specs=[rows(D_MODEL),
                   pl.BlockSpec((ROW_TILE * SUBLANES, LANES), lambda i: (i, 0)),
                   rows(LANES), rows(LANES), rows(LANES),
                   full(SUBLANES, N_EXPERTS)],
        out_shape=[
            jax.ShapeDtypeStruct((T, D_MODEL), F32),
            jax.ShapeDtypeStruct((T * SUBLANES, LANES), F32),
            jax.ShapeDtypeStruct((T, LANES), jnp.int32),
            jax.ShapeDtypeStruct((T, LANES), jnp.int32),
            jax.ShapeDtypeStruct((T, LANES), F32),
            jax.ShapeDtypeStruct((SUBLANES, N_EXPERTS), F32),
        ],
        scratch_shapes=[pltpu.VMEM((1, N_EXPERTS), F32)],
        compiler_params=pltpu.CompilerParams(
            dimension_semantics=("arbitrary",), vmem_limit_bytes=VMEM_LIMIT),
        name="outproj_router",
    )(o_sb, o_cv, o_mx, x2, g_sb, g_cv, g_mx, w_out_bf, g_ffn, w_router, b_router)


def _rows_copy(src_ref, src_row, dst_ref, dst_row, n, sem):
    src = src_ref.at[pl.ds(pl.multiple_of(src_row * SUBLANES, SUBLANES), n * SUBLANES), :]
    dst = dst_ref.at[pl.ds(pl.multiple_of(dst_row * SUBLANES, SUBLANES), n * SUBLANES), :]
    return pltpu.make_async_copy(src, dst, sem)


def _dispatch_kernel(fill_start_ref, fill_rows_ref, dest_ref, hn3_ref, xs3_ref,
                     tile_ref, zero_ref, sem_ref, load_sem_ref, fill_sem_ref, *, tm):
    i = pl.program_id(0)
    n = pl.num_programs(0)
    slot = lax.rem(i, 3)
    slot_next = lax.rem(i + 1, 3)
    n_copies = tm * TOP_K
    pieces = [1 << p for p in range(MOE_BM.bit_length())]

    def fill(wait):
        for e in range(2 * N_EXPERTS):
            rows = fill_rows_ref[e]
            start = fill_start_ref[e]
            for p in pieces:
                @pl.when((rows & p) != 0)
                def _(p=p, start=start, rows=rows):
                    cp = _rows_copy(zero_ref, 0, xs3_ref, start + (rows & (p - 1)), p,
                                    fill_sem_ref.at[0])
                    if wait:
                        cp.wait()
                    else:
                        cp.start()

    @pl.when(i == 0)
    def _():
        zero_ref[...] = jnp.zeros_like(zero_ref)
        fill(wait=False)

    def tile_load(j, s):
        return _rows_copy(hn3_ref, j * tm, tile_ref.at[s], 0, tm, load_sem_ref.at[s])

    def rows_wait(s):
        _rows_copy(tile_ref.at[s], 0, xs3_ref, 0, n_copies, sem_ref.at[s]).wait()

    @pl.when(i == 0)
    def _():
        tile_load(0, 0).start()

    @pl.when(i >= 2)
    def _():
        rows_wait(slot_next)

    @pl.when(i + 1 < n)
    def _():
        tile_load(i + 1, slot_next).start()

    tile_load(i, slot).wait()

    def group(g, _):
        m0 = pl.multiple_of(g * DMA_UNROLL, DMA_UNROLL)
        dests = [dest_ref[0, 0, m0 + u] for u in range(DMA_UNROLL)]
        for u in range(DMA_UNROLL):
            r = g * (DMA_UNROLL // TOP_K) + u // TOP_K
            _rows_copy(tile_ref.at[slot], r, xs3_ref, dests[u], 1, sem_ref.at[slot]).start()
        return 0
    lax.fori_loop(0, n_copies // DMA_UNROLL, group, 0)

    @pl.when(i == 0)
    def _():
        fill(wait=True)

    @pl.when(i == n - 1)
    def _():
        @pl.when(i >= 1)
        def _():
            rows_wait(lax.rem(i + 2, 3))
        rows_wait(slot)


def _dispatch(fill_start, fill_rows, dest, hn3, n_rows):
    T = hn3.shape[0] // SUBLANES
    tm = ROW_TILE
    grid_spec = pltpu.PrefetchScalarGridSpec(
        num_scalar_prefetch=2,
        grid=(T // tm,),
        in_specs=[
            pl.BlockSpec((1, 1, tm * TOP_K), lambda i, fs, fr: (i, 0, 0),
                         memory_space=pltpu.SMEM),
            pl.BlockSpec(memory_space=pl.ANY),
        ],
        out_specs=pl.BlockSpec(memory_space=pl.ANY),
        scratch_shapes=[
            pltpu.VMEM((3, tm * SUBLANES, LANES), F32),
            pltpu.VMEM((MOE_BM * SUBLANES, LANES), F32),
            pltpu.SemaphoreType.DMA((3,)),
            pltpu.SemaphoreType.DMA((3,)),
            pltpu.SemaphoreType.DMA((1,)),
        ],
    )
    return pl.pallas_call(
        functools.partial(_dispatch_kernel, tm=tm),
        grid_spec=grid_spec,
        out_shape=jax.ShapeDtypeStruct((n_rows * SUBLANES, LANES), F32),
        compiler_params=pltpu.CompilerParams(
            dimension_semantics=("arbitrary",), vmem_limit_bytes=VMEM_LIMIT),
        name="dispatch",
    )(fill_start, fill_rows, dest.reshape(T // tm, 1, tm * TOP_K), hn3)


def _experts_kernel(bexp_ref, nused_ref, x_ref, wgu_ref, bgu_ref, wd_ref, bd_ref, y_ref,
                    xb_ref, hm_ref, wgu_bf_ref, wd_bf_ref):
    b = pl.program_id(0)

    @pl.when(b >= nused_ref[0])
    def _():
        y_ref[...] = jnp.zeros_like(y_ref)

    @pl.when(b < nused_ref[0])
    def _():
        prev = jnp.maximum(b - 1, 0)
        @pl.when((b == 0) | (bexp_ref[b] != bexp_ref[prev]))
        def _():
            wgu_bf_ref[...] = wgu_ref[...].astype(BF16)
            wd_bf_ref[...] = wd_ref[...].astype(BF16)

        for s in range(SUBLANES):
            xb_ref[:, s * LANES:(s + 1) * LANES] = (
                x_ref[pl.ds(s, MOE_BM, stride=SUBLANES), :].astype(BF16))
        x = xb_ref[...]
        for c0 in range(0, D_FF, FF_CHUNK):
            g_cols = slice(c0, c0 + FF_CHUNK)
            u_cols = slice(D_FF + c0, D_FF + c0 + FF_CHUNK)
            gate = jnp.dot(x, wgu_bf_ref[:, g_cols], preferred_element_type=F32) + bgu_ref[:, g_cols]
            up = jnp.dot(x, wgu_bf_ref[:, u_cols], preferred_element_type=F32) + bgu_ref[:, u_cols]
            gate = jnp.minimum(gate, SWIGLU_LIMIT)
            up = jnp.clip(up, -SWIGLU_LIMIT, SWIGLU_LIMIT)
            hm_ref[:, g_cols] = ((up + 1.0) * (gate * jax.nn.sigmoid(SWIGLU_ALPHA * gate))
                                 ).astype(BF16)
        y = jnp.dot(hm_ref[...], wd_bf_ref[...], preferred_element_type=F32) + bd_ref[...]
        for s in range(SUBLANES):
            y_ref[pl.ds(s, MOE_BM, stride=SUBLANES), :] = y[:, s * LANES:(s + 1) * LANES]


def _experts(block_exp, n_used, xs3, w_gu, b_gu, w_down, b_down):
    NB = block_exp.shape[0]
    used = lambda b, nu: jnp.minimum(b, nu[0] - 1)
    grid_spec = pltpu.PrefetchScalarGridSpec(
        num_scalar_prefetch=2,
        grid=(NB,),
        in_specs=[
            pl.BlockSpec((MOE_BM * SUBLANES, LANES), lambda b, be, nu: (used(b, nu), 0)),
            pl.BlockSpec((None, D_MODEL, 2 * D_FF), lambda b, be, nu: (be[b], 0, 0)),
            pl.BlockSpec((None, 1, 2 * D_FF), lambda b, be, nu: (be[b], 0, 0)),
            pl.BlockSpec((None, D_FF, D_MODEL), lambda b, be, nu: (be[b], 0, 0)),
            pl.BlockSpec((None, 1, D_MODEL), lambda b, be, nu: (be[b], 0, 0)),
        ],
        out_specs=pl.BlockSpec((MOE_BM * SUBLANES, LANES), lambda b, be, nu: (b, 0)),
        scratch_shapes=[
            pltpu.VMEM((MOE_BM, D_MODEL), BF16),
            pltpu.VMEM((MOE_BM, D_FF), BF16),
            pltpu.VMEM((D_MODEL, 2 * D_FF), BF16),
            pltpu.VMEM((D_FF, D_MODEL), BF16),
        ],
    )
    return pl.pallas_call(
        _experts_kernel,
        grid_spec=grid_spec,
        out_shape=jax.ShapeDtypeStruct(xs3.shape, F32),
        compiler_params=pltpu.CompilerParams(
            dimension_semantics=("arbitrary",), vmem_limit_bytes=VMEM_LIMIT),
        name="experts",
    )(block_exp, n_used, xs3, w_gu, b_gu, w_down, b_down)


def _routing_tables(counts, top_idx, rank):
    M = top_idx.shape[0] * TOP_K
    NB = -(-M // MOE_BM) + N_EXPERTS
    nblk_e = (counts + MOE_BM - 1) // MOE_BM
    blk_end = jnp.cumsum(nblk_e)
    row_start = (blk_end - nblk_e) * MOE_BM
    n_used = blk_end[-1]
    blk = jnp.minimum(jnp.arange(NB, dtype=jnp.int32), n_used - 1)
    bexp = jnp.sum((blk[:, None] >= blk_end[None, :]).astype(jnp.int32), axis=1)
    experts = jnp.arange(N_EXPERTS, dtype=jnp.int32)
    dest = rank + jnp.sum(jnp.where(top_idx[:, :, None] == experts, row_start, 0), axis=-1)
    tail = NB - N_EXPERTS + experts
    fill_start = jnp.concatenate([row_start + counts, tail * MOE_BM])
    fill_rows = jnp.concatenate([nblk_e * MOE_BM - counts,
                                 jnp.where(tail >= n_used, MOE_BM, 0)])
    return (bexp.astype(jnp.int32), n_used.reshape(1).astype(jnp.int32),
            dest.astype(jnp.int32), fill_start.astype(jnp.int32),
            fill_rows.astype(jnp.int32), NB * MOE_BM)


def _combine_kernel(dest_ref, dest_next_ref, h_ref, gate_ref, g_ref, ys3_ref, o_ref,
                    buf_ref, sem_ref, *, tm):
    i = pl.program_id(0)
    n = pl.num_programs(0)
    slot = lax.rem(i, 2)

    def start_gather(idx_ref, s):
        def group(g, _):
            m0 = pl.multiple_of(g * DMA_UNROLL, DMA_UNROLL)
            srcs = [idx_ref[0, 0, m0 + u] for u in range(DMA_UNROLL)]
            for u in range(DMA_UNROLL):
                r = g * (DMA_UNROLL // TOP_K) + u // TOP_K
                _rows_copy(ys3_ref, srcs[u], buf_ref.at[s], (u % TOP_K) * tm + r, 1,
                           sem_ref.at[s]).start()
            return 0
        lax.fori_loop(0, tm * TOP_K // DMA_UNROLL, group, 0)

    @pl.when(i == 0)
    def _():
        start_gather(dest_ref, 0)

    @pl.when(i + 1 < n)
    def _():
        start_gather(dest_next_ref, 1 - slot)

    _rows_copy(ys3_ref, 0, buf_ref.at[slot], 0, tm * TOP_K, sem_ref.at[slot]).wait()

    gates = gate_ref[...]
    chunks = []
    ssq = jnp.zeros((tm, 1), F32)
    for s in range(SUBLANES):
        acc = h_ref[:, s * LANES:(s + 1) * LANES]
        for k in range(TOP_K):
            acc = acc + gates[:, k:k + 1] * buf_ref[
                slot, pl.ds(k * tm * SUBLANES + s, tm, stride=SUBLANES), :]
        chunks.append(acc)
        ssq = ssq + jnp.sum(acc * acc, axis=-1, keepdims=True)
    inv = lax.rsqrt(ssq * (1.0 / D_MODEL) + EPS)
    for s in range(SUBLANES):
        o_ref[:, s * LANES:(s + 1) * LANES] = chunks[s] * inv * g_ref[:, s * LANES:(s + 1) * LANES]


def _combine(h, ys3, dest, gates, g_final):
    T = h.shape[0]
    tm = COMBINE_TILE
    n = T // tm
    table = dest.reshape(n, 1, tm * TOP_K)
    idx_spec = lambda f: pl.BlockSpec((1, 1, tm * TOP_K), lambda i: (f(i), 0, 0),
                                      memory_space=pltpu.SMEM)
    return pl.pallas_call(
        functools.partial(_combine_kernel, tm=tm),
        grid=(n,),
        in_specs=[
            idx_spec(lambda i: i), idx_spec(lambda i: jnp.minimum(i + 1, n - 1)),
            pl.BlockSpec((tm, D_MODEL), lambda i: (i, 0)),
            pl.BlockSpec((tm, LANES), lambda i: (i, 0)),
            pl.BlockSpec((1, D_MODEL), lambda i: (0, 0)),
            pl.BlockSpec(memory_space=pl.ANY),
        ],
        out_specs=pl.BlockSpec((tm, D_MODEL), lambda i: (i, 0)),
        out_shape=jax.ShapeDtypeStruct((T, D_MODEL), F32),
        scratch_shapes=[
            pltpu.VMEM((2, tm * TOP_K * SUBLANES, LANES), F32),
            pltpu.SemaphoreType.DMA((2,)),
        ],
        compiler_params=pltpu.CompilerParams(
            dimension_semantics=("arbitrary",), vmem_limit_bytes=VMEM_LIMIT),
        name="combine",
    )(table, table, h, gates, g_final, ys3)


def kernel(x, mem, g_attn_norm, w_in, b_glu, w_dw, b_dw, g_cv_ln, b_cv_ln, w_pw2, b_pw2, g_mem, w_mem_kv, g_sb_out, g_cv_out, g_mx_out, w_out, g_ffn_norm, w_router, b_router, w_gu, b_gu, w_down, b_down, g_final):
    B, S, D = x.shape
    T = B * S
    assert D == D_MODEL and S % ROW_TILE == 0 and g_attn_norm.shape[0] == 1
    l = 0
    row = lambda v: v.reshape(1, -1)
    x2 = x.reshape(T, D)

    proj = _inproj(x2, row(g_attn_norm[l]), w_in[l].astype(BF16))
    proj3 = proj.reshape(B, S, IN_W)
    o_sb = _sb_attention(proj3)
    o_cv = _conformer(proj3, row(b_glu[l]), w_dw[l].reshape(CONV_K, CV_W), row(b_dw[l]),
                      row(g_cv_ln[l]), row(b_cv_ln[l]), w_pw2[l].astype(BF16), row(b_pw2[l]))
    o_mx = _memx(mem, row(g_mem[l]), w_mem_kv[l].astype(BF16), proj3)

    h, hn3, idx_pad, rank_pad, gate_pad, counts = _outproj(
        o_sb.reshape(T, SB_W), o_cv.reshape(T, CV_W), o_mx.reshape(T, MX_W), x2,
        row(g_sb_out[l]), row(g_cv_out[l]), row(g_mx_out[l]), w_out[l].astype(BF16),
        row(g_ffn_norm[l]), w_router[l], row(b_router[l]))

    bexp, n_used, dest, fill_start, fill_rows, n_rows = _routing_tables(
        counts[0].astype(jnp.int32), idx_pad[:, :TOP_K], rank_pad[:, :TOP_K])
    xs3 = _dispatch(fill_start, fill_rows, dest, hn3, n_rows)
    ys3 = _experts(bexp, n_used, xs3, w_gu[l], b_gu[l].reshape(N_EXPERTS, 1, 2 * D_FF),
                   w_down[l], b_down[l].reshape(N_EXPERTS, 1, D_MODEL))
    out = _combine(h, ys3, dest, gate_pad, row(g_final))
    return out.reshape(B, S, D)
```

```python
import functools

import jax
import jax.numpy as jnp
from jax import lax
from jax.experimental import pallas as pl
from jax.experimental.pallas import tpu as pltpu

F32 = jnp.float32
BF16 = jnp.bfloat16

D_MODEL = 1024
HEAD_DIM = 64
SB_W = 512
CV_W = 256
MX_W = 256
IN_W = 3 * SB_W + 2 * CV_W + MX_W
CONV_K = 31
N_MEM = 256
N_EXPERTS = 32
TOP_K = 4
D_FF = 1024
SWIGLU_ALPHA = 1.702
SWIGLU_LIMIT = 7.0
EPS = 1e-6

LANES = 128
SUBLANES = 8
ROW_TILE = 512
SB_TILE = 256
SB_EXP_ZERO_BELOW = -104.0
CONV_PAD = 32
CONV_ROWS = 128
MX_ROWS = 256
MOE_BM = 512
FF_CHUNK = 512
COMBINE_TILE = 256
DMA_UNROLL = 8
DMA_PRIORITIES = 2
VMEM_LIMIT = 56 * 1024 * 1024


def _rms(x, g):
    return x * lax.rsqrt(jnp.mean(x * x, axis=-1, keepdims=True) + EPS) * g


def _inproj_kernel(x_ref, g_ref, w_ref, o_ref):
    xn = _rms(x_ref[...], g_ref[...]).astype(BF16)
    o_ref[...] = jnp.dot(xn, w_ref[...], preferred_element_type=F32).astype(BF16)


def _inproj(x2, g, w_bf):
    T = x2.shape[0]
    return pl.pallas_call(
        _inproj_kernel,
        grid=(T // ROW_TILE,),
        in_specs=[
            pl.BlockSpec((ROW_TILE, D_MODEL), lambda i: (i, 0)),
            pl.BlockSpec((1, D_MODEL), lambda i: (0, 0)),
            pl.BlockSpec((D_MODEL, IN_W), lambda i: (0, 0)),
        ],
        out_specs=pl.BlockSpec((ROW_TILE, IN_W), lambda i: (i, 0)),
        out_shape=jax.ShapeDtypeStruct((T, IN_W), BF16),
        compiler_params=pltpu.CompilerParams(
            dimension_semantics=("arbitrary",), vmem_limit_bytes=VMEM_LIMIT),
        name="inproj",
    )(x2, g, w_bf)


def _sb_kernel(q_ref, k_ref, v_ref, o_ref, *, seq):
    lane = lax.broadcasted_iota(jnp.int32, (SB_TILE, LANES), 1)
    head0 = lane < HEAD_DIM
    row = lax.broadcasted_iota(jnp.int32, (SB_TILE, SB_TILE), 0)
    col = lax.broadcasted_iota(jnp.int32, (SB_TILE, SB_TILE), 1)
    tri = jnp.where(row > col, 1.0, 0.0).astype(BF16)
    dmask = col < row

    def tile(qh, s0, c, diag):
        kb = k_ref[pl.ds(s0, SB_TILE), :]
        z = lax.dot_general(qh, kb, (((1,), (1,)), ((), ())), preferred_element_type=F32)
        lb = jnp.minimum(z, 0.0) - jnp.log(1.0 + jnp.exp(-jnp.abs(z)))
        l1m = lb - z
        if diag:
            l1m = jnp.where(dmask, l1m, 0.0)
        after = jnp.dot(l1m.astype(BF16), tri, preferred_element_type=F32)
        rowsum = jnp.broadcast_to(jnp.sum(l1m, axis=1, keepdims=True), (SB_TILE, LANES))
        arg = lb + after
        if c is not None:
            arg = arg + jnp.concatenate([c] * (SB_TILE // LANES), axis=1)
        a = jnp.exp(arg)
        if diag:
            a = jnp.where(dmask, a, 0.0)
        pv = jnp.dot(a.astype(BF16), v_ref[pl.ds(s0, SB_TILE), :], preferred_element_type=F32)
        return pv, rowsum

    def alive(c0, c1):
        return jnp.max(jnp.maximum(c0, c1)) >= SB_EXP_ZERO_BELOW

    def qtile(i, _):
        t0 = pl.multiple_of(i * SB_TILE, SB_TILE)
        q = q_ref[pl.ds(t0, SB_TILE), :] * (HEAD_DIM ** -0.5)
        q0 = jnp.where(head0, q, jnp.zeros_like(q))
        q1 = jnp.where(head0, jnp.zeros_like(q), q)
        acc0, c0 = tile(q0, t0, None, True)
        acc1, c1 = tile(q1, t0, None, True)

        def cond(st):
            return (st[0] >= 0) & st[5]

        def body(st):
            j, acc0, acc1, c0, c1, _ = st
            s0 = pl.multiple_of(j * SB_TILE, SB_TILE)
            pv0, r0 = tile(q0, s0, c0, False)
            pv1, r1 = tile(q1, s0, c1, False)
            c0 = c0 + r0
            c1 = c1 + r1
            return j - 1, acc0 + pv0, acc1 + pv1, c0, c1, alive(c0, c1)

        st = lax.while_loop(cond, body, (i - 1, acc0, acc1, c0, c1, alive(c0, c1)))
        o_ref[pl.ds(t0, SB_TILE), :] = jnp.where(head0, st[1], st[2]).astype(BF16)
        return 0

    lax.fori_loop(0, seq // SB_TILE, qtile, 0)


def _sb_attention(proj3):
    B, S, _ = proj3.shape
    pairs = SB_W // LANES
    return pl.pallas_call(
        functools.partial(_sb_kernel, seq=S),
        grid=(B, pairs),
        in_specs=[
            pl.BlockSpec((None, S, LANES), lambda b, p: (b, 0, p)),
            pl.BlockSpec((None, S, LANES), lambda b, p: (b, 0, pairs + p)),
            pl.BlockSpec((None, S, LANES), lambda b, p: (b, 0, 2 * pairs + p)),
        ],
        out_specs=pl.BlockSpec((None, S, LANES), lambda b, p: (b, 0, p)),
        out_shape=jax.ShapeDtypeStruct((B, S, SB_W), BF16),
        compiler_params=pltpu.CompilerParams(
            dimension_semantics=("arbitrary", "arbitrary"), vmem_limit_bytes=VMEM_LIMIT),
        name="sb_attention",
    )(proj3, proj3, proj3)


def _conv_kernel(glu_ref, bglu_ref, wdw_ref, bdw_ref, gln_ref, bln_ref, wpw_ref, bpw_ref,
                 o_ref, upad_ref, shift_ref, *, seq):
    upad_ref[0:CONV_PAD, :] = jnp.zeros((CONV_PAD, CV_W), F32)
    for c in range(seq // CONV_ROWS):
        r0 = c * CONV_ROWS
        g = glu_ref[r0:r0 + CONV_ROWS, :].astype(F32) + bglu_ref[...]
        upad_ref[CONV_PAD + r0:CONV_PAD + r0 + CONV_ROWS, :] = (
            g[:, :CV_W] * jax.nn.sigmoid(g[:, CV_W:]))
    for c in range(seq // CONV_ROWS):
        r0 = c * CONV_ROWS
        acc = jnp.zeros((CONV_ROWS, CV_W), F32) + bdw_ref[...]
        span = CONV_ROWS + CONV_PAD - SUBLANES
        for r in range(1, SUBLANES):
            shift_ref[r, 0:span, :] = upad_ref[r0 + r:r0 + r + span, :]
        for k in range(CONV_K):
            off = CONV_PAD - (CONV_K - 1) + k
            r = off % SUBLANES
            a = off - r
            if r == 0:
                tap = upad_ref[r0 + a:r0 + a + CONV_ROWS, :]
            else:
                tap = shift_ref[r, a:a + CONV_ROWS, :]
            acc = acc + tap * wdw_ref[k:k + 1, :]
        mu = jnp.mean(acc, axis=-1, keepdims=True)
        d = acc - mu
        var = jnp.mean(d * d, axis=-1, keepdims=True)
        y = d * lax.rsqrt(var + EPS) * gln_ref[...] + bln_ref[...]
        y = y * jax.nn.sigmoid(y)
        out = jnp.dot(y.astype(BF16), wpw_ref[...], preferred_element_type=F32) + bpw_ref[...]
        o_ref[r0:r0 + CONV_ROWS, :] = out.astype(BF16)


def _conformer(proj3, b_glu, w_dw, b_dw, g_ln, b_ln, w_pw_bf, b_pw):
    B, S, _ = proj3.shape
    glu_block = (3 * SB_W) // (2 * CV_W)
    vec = lambda n: pl.BlockSpec((1, n), lambda b: (0, 0))
    return pl.pallas_call(
        functools.partial(_conv_kernel, seq=S),
        grid=(B,),
        in_specs=[
            pl.BlockSpec((None, S, 2 * CV_W), lambda b: (b, 0, glu_block)),
            vec(2 * CV_W),
            pl.BlockSpec((CONV_K, CV_W), lambda b: (0, 0)),
            vec(CV_W), vec(CV_W), vec(CV_W),
            pl.BlockSpec((CV_W, CV_W), lambda b: (0, 0)),
            vec(CV_W),
        ],
        out_specs=pl.BlockSpec((None, S, CV_W), lambda b: (b, 0, 0)),
        out_shape=jax.ShapeDtypeStruct((B, S, CV_W), BF16),
        scratch_shapes=[pltpu.VMEM((CONV_PAD + S, CV_W), F32),
                        pltpu.VMEM((SUBLANES, CONV_ROWS + CONV_PAD, CV_W), F32)],
        compiler_params=pltpu.CompilerParams(
            dimension_semantics=("arbitrary",), vmem_limit_bytes=VMEM_LIMIT),
        name="conformer",
    )(proj3, b_glu, w_dw, b_dw, g_ln, b_ln, w_pw_bf, b_pw)


def _memx_kernel(mem_ref, gm_ref, wkv_ref, q_ref, o_ref, *, seq):
    scale = HEAD_DIM ** -0.5
    mn = _rms(mem_ref[...], gm_ref[...]).astype(BF16)
    kv = jnp.dot(mn, wkv_ref[...], preferred_element_type=F32)
    km = kv[:, :MX_W].astype(BF16)
    vm = kv[:, MX_W:].astype(BF16)
    lane = lax.broadcasted_iota(jnp.int32, (MX_ROWS, MX_W), 1)

    def chunk(c, _):
        r0 = pl.multiple_of(c * MX_ROWS, MX_ROWS)
        q = q_ref[pl.ds(r0, MX_ROWS), :]
        out = jnp.zeros((MX_ROWS, MX_W), F32)
        for h in range(MX_W // HEAD_DIM):
            head = (lane >= HEAD_DIM * h) & (lane < HEAD_DIM * (h + 1))
            qh = jnp.where(head, q, jnp.zeros_like(q))
            s = lax.dot_general(qh, km, (((1,), (1,)), ((), ())),
                                preferred_element_type=F32) * scale
            p = jnp.exp(s - jnp.max(s, axis=-1, keepdims=True))
            p = p / jnp.sum(p, axis=-1, keepdims=True)
            oh = jnp.dot(p.astype(BF16), vm, preferred_element_type=F32)
            out = jnp.where(head, oh, out)
        o_ref[pl.ds(r0, MX_ROWS), :] = out.astype(BF16)
        return 0

    lax.fori_loop(0, seq // MX_ROWS, chunk, 0)


def _memx(mem, g_mem, w_kv_bf, proj3):
    B, S, _ = proj3.shape
    q_block = (3 * SB_W + 2 * CV_W) // MX_W
    return pl.pallas_call(
        functools.partial(_memx_kernel, seq=S),
        grid=(B,),
        in_specs=[
            pl.BlockSpec((None, N_MEM, D_MODEL), lambda b: (b, 0, 0)),
            pl.BlockSpec((1, D_MODEL), lambda b: (0, 0)),
            pl.BlockSpec((D_MODEL, 2 * MX_W), lambda b: (0, 0)),
            pl.BlockSpec((None, S, MX_W), lambda b: (b, 0, q_block)),
        ],
        out_specs=pl.BlockSpec((None, S, MX_W), lambda b: (b, 0, 0)),
        out_shape=jax.ShapeDtypeStruct((B, S, MX_W), BF16),
        compiler_params=pltpu.CompilerParams(
            dimension_semantics=("arbitrary",), vmem_limit_bytes=VMEM_LIMIT),
        name="memx",
    )(mem, g_mem, w_kv_bf, proj3)


def _outproj_kernel(sb_ref, cv_ref, mx_ref, x_ref, gsb_ref, gcv_ref, gmx_ref, wo_ref,
                    gffn_ref, wr_ref, br_ref, h_ref, hn3_ref, idx_ref, rank_ref, gate_ref,
                    cnt_ref, run_ref):
    def normed(o_ref, g_ref):
        return _rms(o_ref[...].astype(F32), g_ref[...]).astype(BF16)

    mix = jnp.dot(normed(sb_ref, gsb_ref), wo_ref[0:SB_W, :], preferred_element_type=F32)
    mix += jnp.dot(normed(cv_ref, gcv_ref), wo_ref[SB_W:SB_W + CV_W, :],
                   preferred_element_type=F32)
    mix += jnp.dot(normed(mx_ref, gmx_ref), wo_ref[SB_W + CV_W:, :],
                   preferred_element_type=F32)
    h = x_ref[...] + mix
    h_ref[...] = h
    hn = _rms(h, gffn_ref[...])
    for s in range(D_MODEL // LANES):
        hn3_ref[pl.ds(s, ROW_TILE, stride=SUBLANES), :] = hn[:, s * LANES:(s + 1) * LANES]
    wr = wr_ref[...]
    hn_hi = hn.astype(BF16)
    hn_lo = (hn - hn_hi.astype(F32)).astype(BF16)
    wr_hi = wr.astype(BF16)
    wr_lo = (wr - wr_hi.astype(F32)).astype(BF16)
    logits = (jnp.dot(hn_hi, wr_hi, preferred_element_type=F32)
              + jnp.dot(hn_lo, wr_hi, preferred_element_type=F32)
              + jnp.dot(hn_hi, wr_lo, preferred_element_type=F32)) + br_ref[...]
    eid = lax.broadcasted_iota(jnp.int32, logits.shape, 1)
    vals, idxs = [], []
    for _ in range(TOP_K):
        m = jnp.max(logits, axis=-1, keepdims=True)
        i = jnp.min(jnp.where(logits == m, eid, N_EXPERTS), axis=-1, keepdims=True)
        vals.append(m)
        idxs.append(i)
        logits = jnp.where(eid == i, -jnp.inf, logits)
    es = [jnp.exp(v - vals[0]) for v in vals]
    denom = es[0] + es[1] + es[2] + es[3]

    @pl.when(pl.program_id(0) == 0)
    def _():
        run_ref[...] = jnp.zeros_like(run_ref)
    tm = logits.shape[0]
    row = lax.broadcasted_iota(jnp.int32, (tm, tm), 0)
    col = lax.broadcasted_iota(jnp.int32, (tm, tm), 1)
    before = jnp.where(col < row, 1.0, 0.0).astype(BF16)
    base = run_ref[...]
    ranks = []
    for k in range(TOP_K):
        onehot = jnp.where(eid == idxs[k], 1.0, 0.0)
        prefix = jnp.dot(before, onehot.astype(BF16), preferred_element_type=F32)
        ranks.append(jnp.sum(onehot * (prefix + base), axis=-1, keepdims=True))
        base = base + jnp.sum(onehot, axis=0, keepdims=True)
    run_ref[...] = base
    cnt_ref[...] = jnp.broadcast_to(base, cnt_ref.shape)

    lane = lax.broadcasted_iota(jnp.int32, idx_ref.shape, 1)
    idx_out = jnp.zeros(idx_ref.shape, jnp.int32)
    rank_out = jnp.zeros(rank_ref.shape, jnp.int32)
    gate_out = jnp.zeros(gate_ref.shape, F32)
    for k in range(TOP_K):
        idx_out = jnp.where(lane == k, idxs[k], idx_out)
        rank_out = jnp.where(lane == k, ranks[k].astype(jnp.int32), rank_out)
        gate_out = jnp.where(lane == k, es[k] / denom, gate_out)
    idx_ref[...] = idx_out
    rank_ref[...] = rank_out
    gate_ref[...] = gate_out


def _outproj(o_sb, o_cv, o_mx, x2, g_sb, g_cv, g_mx, w_out_bf, g_ffn, w_router, b_router):
    T = x2.shape[0]
    rows = lambda n: pl.BlockSpec((ROW_TILE, n), lambda i: (i, 0))
    full = lambda a, b: pl.BlockSpec((a, b), lambda i: (0, 0))
    return pl.pallas_call(
        _outproj_kernel,
        grid=(T // ROW_TILE,),
        in_specs=[
            rows(SB_W), rows(CV_W), rows(MX_W), rows(D_MODEL),
            full(1, SB_W), full(1, CV_W), full(1, MX_W),
            full(D_MODEL, D_MODEL), full(1, D_MODEL),
            full(D_MODEL, N_EXPERTS), full(1, N_EXPERTS),
        ],
        out_specs=[rows(D_MODEL),
                   pl.BlockSpec((ROW_TILE * SUBLANES, LANES), lambda i: (i, 0)),
                   rows(LANES), rows(LANES), rows(LANES),
                   full(SUBLANES, N_EXPERTS)],
        out_shape=[
            jax.ShapeDtypeStruct((T, D_MODEL), F32),
            jax.ShapeDtypeStruct((T * SUBLANES, LANES), F32),
            jax.ShapeDtypeStruct((T, LANES), jnp.int32),
            jax.ShapeDtypeStruct((T, LANES), jnp.int32),
            jax.ShapeDtypeStruct((T, LANES), F32),
            jax.ShapeDtypeStruct((SUBLANES, N_EXPERTS), F32),
        ],
        scratch_shapes=[pltpu.VMEM((1, N_EXPERTS), F32)],
        compiler_params=pltpu.CompilerParams(
            dimension_semantics=("arbitrary",), vmem_limit_bytes=VMEM_LIMIT),
        name="outproj_router",
    )(o_sb, o_cv, o_mx, x2, g_sb, g_cv, g_mx, w_out_bf, g_ffn, w_router, b_router)


def _rows_copy(src_ref, src_row, dst_ref, dst_row, n, sem):
    src = src_ref.at[pl.ds(pl.multiple_of(src_row * SUBLANES, SUBLANES), n * SUBLANES), :]
    dst = dst_ref.at[pl.ds(pl.multiple_of(dst_row * SUBLANES, SUBLANES), n * SUBLANES), :]
    return pltpu.make_async_copy(src, dst, sem)


def _dispatch_kernel(fill_start_ref, fill_rows_ref, dest_ref, hn3_ref, xs3_ref,
                     tile_ref, zero_ref, sem_ref, load_sem_ref, fill_sem_ref, *, tm):
    i = pl.program_id(0)
    n = pl.num_programs(0)
    slot = lax.rem(i, 3)
    slot_next = lax.rem(i + 1, 3)
    n_copies = tm * TOP_K
    pieces = [1 << p for p in range(MOE_BM.bit_length())]

    def fill(wait):
        for e in range(2 * N_EXPERTS):
            rows = fill_rows_ref[e]
            start = fill_start_ref[e]
            for p in pieces:
                @pl.when((rows & p) != 0)
                def _(p=p, start=start, rows=rows):
                    cp = _rows_copy(zero_ref, 0, xs3_ref, start + (rows & (p - 1)), p,
                                    fill_sem_ref.at[0])
                    if wait:
                        cp.wait()
                    else:
                        cp.start()

    @pl.when(i == 0)
    def _():
        zero_ref[...] = jnp.zeros_like(zero_ref)
        fill(wait=False)

    def tile_load(j, s):
        return _rows_copy(hn3_ref, j * tm, tile_ref.at[s], 0, tm, load_sem_ref.at[s])

    def rows_wait(s):
        _rows_copy(tile_ref.at[s], 0, xs3_ref, 0, n_copies, sem_ref.at[s]).wait()

    @pl.when(i == 0)
    def _():
        tile_load(0, 0).start()

    @pl.when(i >= 2)
    def _():
        rows_wait(slot_next)

    @pl.when(i + 1 < n)
    def _():
        tile_load(i + 1, slot_next).start()

    tile_load(i, slot).wait()

    def group(g, _):
        m0 = pl.multiple_of(g * DMA_UNROLL, DMA_UNROLL)
        dests = [dest_ref[0, 0, m0 + u] for u in range(DMA_UNROLL)]
        for u in range(DMA_UNROLL):
            r = g * (DMA_UNROLL // TOP_K) + u // TOP_K
            _rows_copy(tile_ref.at[slot], r, xs3_ref, dests[u], 1,
                       sem_ref.at[slot]).start(priority=u % DMA_PRIORITIES)
        return 0
    lax.fori_loop(0, n_copies // DMA_UNROLL, group, 0)

    @pl.when(i == 0)
    def _():
        fill(wait=True)

    @pl.when(i == n - 1)
    def _():
        @pl.when(i >= 1)
        def _():
            rows_wait(lax.rem(i + 2, 3))
        rows_wait(slot)


def _dispatch(fill_start, fill_rows, dest, hn3, n_rows):
    T = hn3.shape[0] // SUBLANES
    tm = ROW_TILE
    grid_spec = pltpu.PrefetchScalarGridSpec(
        num_scalar_prefetch=2,
        grid=(T // tm,),
        in_specs=[
            pl.BlockSpec((1, 1, tm * TOP_K), lambda i, fs, fr: (i, 0, 0),
                         memory_space=pltpu.SMEM),
            pl.BlockSpec(memory_space=pl.ANY),
        ],
        out_specs=pl.BlockSpec(memory_space=pl.ANY),
        scratch_shapes=[
            pltpu.VMEM((3, tm * SUBLANES, LANES), F32),
            pltpu.VMEM((MOE_BM * SUBLANES, LANES), F32),
            pltpu.SemaphoreType.DMA((3,)),
            pltpu.SemaphoreType.DMA((3,)),
            pltpu.SemaphoreType.DMA((1,)),
        ],
    )
    return pl.pallas_call(
        functools.partial(_dispatch_kernel, tm=tm),
        grid_spec=grid_spec,
        out_shape=jax.ShapeDtypeStruct((n_rows * SUBLANES, LANES), F32),
        compiler_params=pltpu.CompilerParams(
            dimension_semantics=("arbitrary",), vmem_limit_bytes=VMEM_LIMIT),
        name="dispatch",
    )(fill_start, fill_rows, dest.reshape(T // tm, 1, tm * TOP_K), hn3)


def _experts_kernel(bexp_ref, nused_ref, x_ref, wgu_ref, bgu_ref, wd_ref, bd_ref, y_ref,
                    xb_ref, hm_ref, wgu_bf_ref, wd_bf_ref):
    b = pl.program_id(0)

    @pl.when(b >= nused_ref[0])
    def _():
        y_ref[...] = jnp.zeros_like(y_ref)

    @pl.when(b < nused_ref[0])
    def _():
        prev = jnp.maximum(b - 1, 0)
        @pl.when((b == 0) | (bexp_ref[b] != bexp_ref[prev]))
        def _():
            wgu_bf_ref[...] = wgu_ref[...].astype(BF16)
            wd_bf_ref[...] = wd_ref[...].astype(BF16)

        for s in range(SUBLANES):
            xb_ref[:, s * LANES:(s + 1) * LANES] = (
                x_ref[pl.ds(s, MOE_BM, stride=SUBLANES), :].astype(BF16))
        x = xb_ref[...]
        for c0 in range(0, D_FF, FF_CHUNK):
            g_cols = slice(c0, c0 + FF_CHUNK)
            u_cols = slice(D_FF + c0, D_FF + c0 + FF_CHUNK)
            gate = jnp.dot(x, wgu_bf_ref[:, g_cols], preferred_element_type=F32) + bgu_ref[:, g_cols]
            up = jnp.dot(x, wgu_bf_ref[:, u_cols], preferred_element_type=F32) + bgu_ref[:, u_cols]
            gate = jnp.minimum(gate, SWIGLU_LIMIT)
            up = jnp.clip(up, -SWIGLU_LIMIT, SWIGLU_LIMIT)
            hm_ref[:, g_cols] = ((up + 1.0) * (gate * jax.nn.sigmoid(SWIGLU_ALPHA * gate))
                                 ).astype(BF16)
        y = jnp.dot(hm_ref[...], wd_bf_ref[...], preferred_element_type=F32) + bd_ref[...]
        for s in range(SUBLANES):
            y_ref[pl.ds(s, MOE_BM, stride=SUBLANES), :] = y[:, s * LANES:(s + 1) * LANES]


def _experts(block_exp, n_used, xs3, w_gu, b_gu, w_down, b_down):
    NB = block_exp.shape[0]
    used = lambda b, nu: jnp.minimum(b, nu[0] - 1)
    grid_spec = pltpu.PrefetchScalarGridSpec(
        num_scalar_prefetch=2,
        grid=(NB,),
        in_specs=[
            pl.BlockSpec((MOE_BM * SUBLANES, LANES), lambda b, be, nu: (used(b, nu), 0)),
            pl.BlockSpec((None, D_MODEL, 2 * D_FF), lambda b, be, nu: (be[b], 0, 0)),
            pl.BlockSpec((None, 1, 2 * D_FF), lambda b, be, nu: (be[b], 0, 0)),
            pl.BlockSpec((None, D_FF, D_MODEL), lambda b, be, nu: (be[b], 0, 0)),
            pl.BlockSpec((None, 1, D_MODEL), lambda b, be, nu: (be[b], 0, 0)),
        ],
        out_specs=pl.BlockSpec((MOE_BM * SUBLANES, LANES), lambda b, be, nu: (b, 0)),
        scratch_shapes=[
            pltpu.VMEM((MOE_BM, D_MODEL), BF16),
            pltpu.VMEM((MOE_BM, D_FF), BF16),
            pltpu.VMEM((D_MODEL, 2 * D_FF), BF16),
            pltpu.VMEM((D_FF, D_MODEL), BF16),
        ],
    )
    return pl.pallas_call(
        _experts_kernel,
        grid_spec=grid_spec,
        out_shape=jax.ShapeDtypeStruct(xs3.shape, F32),
        compiler_params=pltpu.CompilerParams(
            dimension_semantics=("arbitrary",), vmem_limit_bytes=VMEM_LIMIT),
        name="experts",
    )(block_exp, n_used, xs3, w_gu, b_gu, w_down, b_down)


def _routing_tables(counts, top_idx, rank):
    M = top_idx.shape[0] * TOP_K
    NB = -(-M // MOE_BM) + N_EXPERTS
    nblk_e = (counts + MOE_BM - 1) // MOE_BM
    blk_end = jnp.cumsum(nblk_e)
    row_start = (blk_end - nblk_e) * MOE_BM
    n_used = blk_end[-1]
    blk = jnp.minimum(jnp.arange(NB, dtype=jnp.int32), n_used - 1)
    bexp = jnp.sum((blk[:, None] >= blk_end[None, :]).astype(jnp.int32), axis=1)
    experts = jnp.arange(N_EXPERTS, dtype=jnp.int32)
    dest = rank + jnp.sum(jnp.where(top_idx[:, :, None] == experts, row_start, 0), axis=-1)
    tail = NB - N_EXPERTS + experts
    fill_start = jnp.concatenate([row_start + counts, tail * MOE_BM])
    fill_rows = jnp.concatenate([nblk_e * MOE_BM - counts,
                                 jnp.where(tail >= n_used, MOE_BM, 0)])
    return (bexp.astype(jnp.int32), n_used.reshape(1).astype(jnp.int32),
            dest.astype(jnp.int32), fill_start.astype(jnp.int32),
            fill_rows.astype(jnp.int32), NB * MOE_BM)


def _combine_kernel(dest_ref, dest_next_ref, h_ref, gate_ref, g_ref, ys3_ref, o_ref,
                    buf_ref, sem_ref, *, tm):
    i = pl.program_id(0)
    n = pl.num_programs(0)
    slot = lax.rem(i, 2)

    def start_gather(idx_ref, s):
        def group(g, _):
            m0 = pl.multiple_of(g * DMA_UNROLL, DMA_UNROLL)
            srcs = [idx_ref[0, 0, m0 + u] for u in range(DMA_UNROLL)]
            for u in range(DMA_UNROLL):
                r = g * (DMA_UNROLL // TOP_K) + u // TOP_K
                _rows_copy(ys3_ref, srcs[u], buf_ref.at[s], (u % TOP_K) * tm + r, 1,
                           sem_ref.at[s]).start(priority=u % DMA_PRIORITIES)
            return 0
        lax.fori_loop(0, tm * TOP_K // DMA_UNROLL, group, 0)

    @pl.when(i == 0)
    def _():
        start_gather(dest_ref, 0)

    @pl.when(i + 1 < n)
    def _():
        start_gather(dest_next_ref, 1 - slot)

    _rows_copy(ys3_ref, 0, buf_ref.at[slot], 0, tm * TOP_K, sem_ref.at[slot]).wait()

    gates = gate_ref[...]
    chunks = []
    ssq = jnp.zeros((tm, 1), F32)
    for s in range(SUBLANES):
        acc = h_ref[:, s * LANES:(s + 1) * LANES]
        for k in range(TOP_K):
            acc = acc + gates[:, k:k + 1] * buf_ref[
                slot, pl.ds(k * tm * SUBLANES + s, tm, stride=SUBLANES), :]
        chunks.append(acc)
        ssq = ssq + jnp.sum(acc * acc, axis=-1, keepdims=True)
    inv = lax.rsqrt(ssq * (1.0 / D_MODEL) + EPS)
    for s in range(SUBLANES):
        o_ref[:, s * LANES:(s + 1) * LANES] = chunks[s] * inv * g_ref[:, s * LANES:(s + 1) * LANES]


def _combine(h, ys3, dest, gates, g_final):
    T = h.shape[0]
    tm = COMBINE_TILE
    n = T // tm
    table = dest.reshape(n, 1, tm * TOP_K)
    idx_spec = lambda f: pl.BlockSpec((1, 1, tm * TOP_K), lambda i: (f(i), 0, 0),
                                      memory_space=pltpu.SMEM)
    return pl.pallas_call(
        functools.partial(_combine_kernel, tm=tm),
        grid=(n,),
        in_specs=[
            idx_spec(lambda i: i), idx_spec(lambda i: jnp.minimum(i + 1, n - 1)),
            pl.BlockSpec((tm, D_MODEL), lambda i: (i, 0)),
            pl.BlockSpec((tm, LANES), lambda i: (i, 0)),
            pl.BlockSpec((1, D_MODEL), lambda i: (0, 0)),
            pl.BlockSpec(memory_space=pl.ANY),
        ],
        out_specs=pl.BlockSpec((tm, D_MODEL), lambda i: (i, 0)),
        out_shape=jax.ShapeDtypeStruct((T, D_MODEL), F32),
        scratch_shapes=[
            pltpu.VMEM((2, tm * TOP_K * SUBLANES, LANES), F32),
            pltpu.SemaphoreType.DMA((2,)),
        ],
        compiler_params=pltpu.CompilerParams(
            dimension_semantics=("arbitrary",), vmem_limit_bytes=VMEM_LIMIT),
        name="combine",
    )(table, table, h, gates, g_final, ys3)


def kernel(x, mem, g_attn_norm, w_in, b_glu, w_dw, b_dw, g_cv_ln, b_cv_ln, w_pw2, b_pw2, g_mem, w_mem_kv, g_sb_out, g_cv_out, g_mx_out, w_out, g_ffn_norm, w_router, b_router, w_gu, b_gu, w_down, b_down, g_final):
    B, S, D = x.shape
    T = B * S
    assert D == D_MODEL and S % ROW_TILE == 0 and g_attn_norm.shape[0] == 1
    l = 0
    row = lambda v: v.reshape(1, -1)
    x2 = x.reshape(T, D)

    proj = _inproj(x2, row(g_attn_norm[l]), w_in[l].astype(BF16))
    proj3 = proj.reshape(B, S, IN_W)
    o_sb = _sb_attention(proj3)
    o_cv = _conformer(proj3, row(b_glu[l]), w_dw[l].reshape(CONV_K, CV_W), row(b_dw[l]),
                      row(g_cv_ln[l]), row(b_cv_ln[l]), w_pw2[l].astype(BF16), row(b_pw2[l]))
    o_mx = _memx(mem, row(g_mem[l]), w_mem_kv[l].astype(BF16), proj3)

    h, hn3, idx_pad, rank_pad, gate_pad, counts = _outproj(
        o_sb.reshape(T, SB_W), o_cv.reshape(T, CV_W), o_mx.reshape(T, MX_W), x2,
        row(g_sb_out[l]), row(g_cv_out[l]), row(g_mx_out[l]), w_out[l].astype(BF16),
        row(g_ffn_norm[l]), w_router[l], row(b_router[l]))

    bexp, n_used, dest, fill_start, fill_rows, n_rows = _routing_tables(
        counts[0].astype(jnp.int32), idx_pad[:, :TOP_K], rank_pad[:, :TOP_K])
    xs3 = _dispatch(fill_start, fill_rows, dest, hn3, n_rows)
    ys3 = _experts(bexp, n_used, xs3, w_gu[l], b_gu[l].reshape(N_EXPERTS, 1, 2 * D_FF),
                   w_down[l], b_down[l].reshape(N_EXPERTS, 1, D_MODEL))
    out = _combine(h, ys3, dest, gate_pad, row(g_final))
    return out.reshape(B, S, D)
```

```python
import functools

import jax
import jax.numpy as jnp
from jax import lax
from jax.experimental import pallas as pl
from jax.experimental.pallas import tpu as pltpu

F32 = jnp.float32
BF16 = jnp.bfloat16

D_MODEL = 1024
HEAD_DIM = 64
SB_W = 512
CV_W = 256
MX_W = 256
IN_W = 3 * SB_W + 2 * CV_W + MX_W
CONV_K = 31
N_MEM = 256
N_EXPERTS = 32
TOP_K = 4
D_FF = 1024
SWIGLU_ALPHA = 1.702
SWIGLU_LIMIT = 7.0
EPS = 1e-6

LANES = 128
SUBLANES = 8
ROW_TILE = 512
SB_TILE = 256
SB_EXP_ZERO_BELOW = -104.0
CONV_PAD = 32
CONV_ROWS = 128
MX_ROWS = 512
MOE_BM = 512
FF_CHUNK = 512
COMBINE_TILE = 256
DMA_UNROLL = 8
DMA_PRIORITIES = 2
VMEM_LIMIT = 56 * 1024 * 1024


def _rms(x, g):
    return x * lax.rsqrt(jnp.mean(x * x, axis=-1, keepdims=True) + EPS) * g


def _inproj_kernel(x_ref, g_ref, w_ref, o_ref):
    xn = _rms(x_ref[...], g_ref[...]).astype(BF16)
    o_ref[...] = jnp.dot(xn, w_ref[...], preferred_element_type=F32).astype(BF16)


def _inproj(x2, g, w_bf):
    T = x2.shape[0]
    return pl.pallas_call(
        _inproj_kernel,
        grid=(T // ROW_TILE,),
        in_specs=[
            pl.BlockSpec((ROW_TILE, D_MODEL), lambda i: (i, 0)),
            pl.BlockSpec((1, D_MODEL), lambda i: (0, 0)),
            pl.BlockSpec((D_MODEL, IN_W), lambda i: (0, 0)),
        ],
        out_specs=pl.BlockSpec((ROW_TILE, IN_W), lambda i: (i, 0)),
        out_shape=jax.ShapeDtypeStruct((T, IN_W), BF16),
        compiler_params=pltpu.CompilerParams(
            dimension_semantics=("arbitrary",), vmem_limit_bytes=VMEM_LIMIT),
        name="inproj",
    )(x2, g, w_bf)


def _sb_kernel(q_ref, k_ref, v_ref, o_ref, *, seq):
    lane = lax.broadcasted_iota(jnp.int32, (SB_TILE, LANES), 1)
    head0 = lane < HEAD_DIM
    row = lax.broadcasted_iota(jnp.int32, (SB_TILE, SB_TILE), 0)
    col = lax.broadcasted_iota(jnp.int32, (SB_TILE, SB_TILE), 1)
    tri = jnp.where(row > col, 1.0, 0.0).astype(BF16)
    dmask = col < row

    def tile(qh, s0, c, diag):
        kb = k_ref[pl.ds(s0, SB_TILE), :]
        z = lax.dot_general(qh, kb, (((1,), (1,)), ((), ())), preferred_element_type=F32)
        lb = jnp.minimum(z, 0.0) - jnp.log(1.0 + jnp.exp(-jnp.abs(z)))
        l1m = lb - z
        if diag:
            l1m = jnp.where(dmask, l1m, 0.0)
        after = jnp.dot(l1m.astype(BF16), tri, preferred_element_type=F32)
        rowsum = jnp.broadcast_to(jnp.sum(l1m, axis=1, keepdims=True), (SB_TILE, LANES))
        arg = lb + after
        if c is not None:
            arg = arg + jnp.concatenate([c] * (SB_TILE // LANES), axis=1)
        a = jnp.exp(arg)
        if diag:
            a = jnp.where(dmask, a, 0.0)
        pv = jnp.dot(a.astype(BF16), v_ref[pl.ds(s0, SB_TILE), :], preferred_element_type=F32)
        return pv, rowsum

    def alive(c0, c1):
        return jnp.max(jnp.maximum(c0, c1)) >= SB_EXP_ZERO_BELOW

    def heads(t0):
        q = q_ref[pl.ds(t0, SB_TILE), :] * (HEAD_DIM ** -0.5)
        return jnp.where(head0, q, jnp.zeros_like(q)), jnp.where(head0, jnp.zeros_like(q), q)

    def store(t0, acc0, acc1):
        o_ref[pl.ds(t0, SB_TILE), :] = jnp.where(head0, acc0, acc1).astype(BF16)

    q0, q1 = heads(0)
    store(0, tile(q0, 0, None, True)[0], tile(q1, 0, None, True)[0])

    def qtile(i, _):
        t0 = pl.multiple_of(i * SB_TILE, SB_TILE)
        s0 = pl.multiple_of((i - 1) * SB_TILE, SB_TILE)
        q0, q1 = heads(t0)
        acc0, c0 = tile(q0, t0, None, True)
        acc1, c1 = tile(q1, t0, None, True)
        pv0, r0 = tile(q0, s0, c0, False)
        pv1, r1 = tile(q1, s0, c1, False)
        acc0, acc1, c0, c1 = acc0 + pv0, acc1 + pv1, c0 + r0, c1 + r1

        def cond(st):
            return (st[0] >= 0) & st[5]

        def body(st):
            j, acc0, acc1, c0, c1, _ = st
            s0 = pl.multiple_of(j * SB_TILE, SB_TILE)
            pv0, r0 = tile(q0, s0, c0, False)
            pv1, r1 = tile(q1, s0, c1, False)
            c0 = c0 + r0
            c1 = c1 + r1
            return j - 1, acc0 + pv0, acc1 + pv1, c0, c1, alive(c0, c1)

        st = lax.while_loop(cond, body, (i - 2, acc0, acc1, c0, c1, alive(c0, c1)))
        store(t0, st[1], st[2])
        return 0

    lax.fori_loop(1, seq // SB_TILE, qtile, 0)


def _sb_attention(proj3):
    B, S, _ = proj3.shape
    pairs = SB_W // LANES
    return pl.pallas_call(
        functools.partial(_sb_kernel, seq=S),
        grid=(B, pairs),
        in_specs=[
            pl.BlockSpec((None, S, LANES), lambda b, p: (b, 0, p)),
            pl.BlockSpec((None, S, LANES), lambda b, p: (b, 0, pairs + p)),
            pl.BlockSpec((None, S, LANES), lambda b, p: (b, 0, 2 * pairs + p)),
        ],
        out_specs=pl.BlockSpec((None, S, LANES), lambda b, p: (b, 0, p)),
        out_shape=jax.ShapeDtypeStruct((B, S, SB_W), BF16),
        compiler_params=pltpu.CompilerParams(
            dimension_semantics=("arbitrary", "arbitrary"), vmem_limit_bytes=VMEM_LIMIT),
        name="sb_attention",
    )(proj3, proj3, proj3)


def _conv_kernel(glu_ref, bglu_ref, wdw_ref, bdw_ref, gln_ref, bln_ref, wpw_ref, bpw_ref,
                 o_ref, upad_ref, shift_ref, *, seq):
    upad_ref[0:CONV_PAD, :] = jnp.zeros((CONV_PAD, CV_W), F32)
    for c in range(seq // CONV_ROWS):
        r0 = c * CONV_ROWS
        g = glu_ref[r0:r0 + CONV_ROWS, :].astype(F32) + bglu_ref[...]
        upad_ref[CONV_PAD + r0:CONV_PAD + r0 + CONV_ROWS, :] = (
            g[:, :CV_W] * jax.nn.sigmoid(g[:, CV_W:]))
    for c in range(seq // CONV_ROWS):
        r0 = c * CONV_ROWS
        acc = jnp.zeros((CONV_ROWS, CV_W), F32) + bdw_ref[...]
        span = CONV_ROWS + CONV_PAD - SUBLANES
        for r in range(1, SUBLANES):
            shift_ref[r, 0:span, :] = upad_ref[r0 + r:r0 + r + span, :]
        for k in range(CONV_K):
            off = CONV_PAD - (CONV_K - 1) + k
            r = off % SUBLANES
            a = off - r
            if r == 0:
                tap = upad_ref[r0 + a:r0 + a + CONV_ROWS, :]
            else:
                tap = shift_ref[r, a:a + CONV_ROWS, :]
            acc = acc + tap * wdw_ref[k:k + 1, :]
        mu = jnp.mean(acc, axis=-1, keepdims=True)
        d = acc - mu
        var = jnp.mean(d * d, axis=-1, keepdims=True)
        y = d * lax.rsqrt(var + EPS) * gln_ref[...] + bln_ref[...]
        y = y * jax.nn.sigmoid(y)
        out = jnp.dot(y.astype(BF16), wpw_ref[...], preferred_element_type=F32) + bpw_ref[...]
        o_ref[r0:r0 + CONV_ROWS, :] = out.astype(BF16)


def _conformer(proj3, b_glu, w_dw, b_dw, g_ln, b_ln, w_pw_bf, b_pw):
    B, S, _ = proj3.shape
    glu_block = (3 * SB_W) // (2 * CV_W)
    vec = lambda n: pl.BlockSpec((1, n), lambda b: (0, 0))
    return pl.pallas_call(
        functools.partial(_conv_kernel, seq=S),
        grid=(B,),
        in_specs=[
            pl.BlockSpec((None, S, 2 * CV_W), lambda b: (b, 0, glu_block)),
            vec(2 * CV_W),
            pl.BlockSpec((CONV_K, CV_W), lambda b: (0, 0)),
            vec(CV_W), vec(CV_W), vec(CV_W),
            pl.BlockSpec((CV_W, CV_W), lambda b: (0, 0)),
            vec(CV_W),
        ],
        out_specs=pl.BlockSpec((None, S, CV_W), lambda b: (b, 0, 0)),
        out_shape=jax.ShapeDtypeStruct((B, S, CV_W), BF16),
        scratch_shapes=[pltpu.VMEM((CONV_PAD + S, CV_W), F32),
                        pltpu.VMEM((SUBLANES, CONV_ROWS + CONV_PAD, CV_W), F32)],
        compiler_params=pltpu.CompilerParams(
            dimension_semantics=("arbitrary",), vmem_limit_bytes=VMEM_LIMIT),
        name="conformer",
    )(proj3, b_glu, w_dw, b_dw, g_ln, b_ln, w_pw_bf, b_pw)


def _memx_kernel(mem_ref, gm_ref, wkv_ref, q_ref, o_ref, *, seq):
    scale = HEAD_DIM ** -0.5
    mn = _rms(mem_ref[...], gm_ref[...]).astype(BF16)
    kv = jnp.dot(mn, wkv_ref[...], preferred_element_type=F32)
    km = kv[:, :MX_W].astype(BF16)
    vm = kv[:, MX_W:].astype(BF16)
    lane = lax.broadcasted_iota(jnp.int32, (MX_ROWS, MX_W), 1)

    def chunk(c, _):
        r0 = pl.multiple_of(c * MX_ROWS, MX_ROWS)
        q = q_ref[pl.ds(r0, MX_ROWS), :]
        out = jnp.zeros((MX_ROWS, MX_W), F32)
        for h in range(MX_W // HEAD_DIM):
            head = (lane >= HEAD_DIM * h) & (lane < HEAD_DIM * (h + 1))
            qh = jnp.where(head, q, jnp.zeros_like(q))
            s = lax.dot_general(qh, km, (((1,), (1,)), ((), ())),
                                preferred_element_type=F32) * scale
            p = jnp.exp(s - jnp.max(s, axis=-1, keepdims=True))
            p = p / jnp.sum(p, axis=-1, keepdims=True)
            oh = jnp.dot(p.astype(BF16), vm, preferred_element_type=F32)
            out = jnp.where(head, oh, out)
        o_ref[pl.ds(r0, MX_ROWS), :] = out.astype(BF16)
        return 0

    lax.fori_loop(0, seq // MX_ROWS, chunk, 0)


def _memx(mem, g_mem, w_kv_bf, proj3):
    B, S, _ = proj3.shape
    q_block = (3 * SB_W + 2 * CV_W) // MX_W
    return pl.pallas_call(
        functools.partial(_memx_kernel, seq=S),
        grid=(B,),
        in_specs=[
            pl.BlockSpec((None, N_MEM, D_MODEL), lambda b: (b, 0, 0)),
            pl.BlockSpec((1, D_MODEL), lambda b: (0, 0)),
            pl.BlockSpec((D_MODEL, 2 * MX_W), lambda b: (0, 0)),
            pl.BlockSpec((None, S, MX_W), lambda b: (b, 0, q_block)),
        ],
        out_specs=pl.BlockSpec((None, S, MX_W), lambda b: (b, 0, 0)),
        out_shape=jax.ShapeDtypeStruct((B, S, MX_W), BF16),
        compiler_params=pltpu.CompilerParams(
            dimension_semantics=("arbitrary",), vmem_limit_bytes=VMEM_LIMIT),
        name="memx",
    )(mem, g_mem, w_kv_bf, proj3)


def _outproj_kernel(sb_ref, cv_ref, mx_ref, x_ref, gsb_ref, gcv_ref, gmx_ref, wo_ref,
                    gffn_ref, wr_ref, br_ref, h_ref, hn3_ref, idx_ref, rank_ref, gate_ref,
                    cnt_ref, run_ref):
    def normed(o_ref, g_ref):
        return _rms(o_ref[...].astype(F32), g_ref[...]).astype(BF16)

    mix = jnp.dot(normed(sb_ref, gsb_ref), wo_ref[0:SB_W, :], preferred_element_type=F32)
    mix += jnp.dot(normed(cv_ref, gcv_ref), wo_ref[SB_W:SB_W + CV_W, :],
                   preferred_element_type=F32)
    mix += jnp.dot(normed(mx_ref, gmx_ref), wo_ref[SB_W + CV_W:, :],
                   preferred_element_type=F32)
    h = x_ref[...] + mix
    h_ref[...] = h
    hn = _rms(h, gffn_ref[...])
    for s in range(D_MODEL // LANES):
        hn3_ref[pl.ds(s, ROW_TILE, stride=SUBLANES), :] = hn[:, s * LANES:(s + 1) * LANES]
    wr = wr_ref[...]
    hn_hi = hn.astype(BF16)
    hn_lo = (hn - hn_hi.astype(F32)).astype(BF16)
    wr_hi = wr.astype(BF16)
    wr_lo = (wr - wr_hi.astype(F32)).astype(BF16)
    logits = (jnp.dot(hn_hi, wr_hi, preferred_element_type=F32)
              + jnp.dot(hn_lo, wr_hi, preferred_element_type=F32)
              + jnp.dot(hn_hi, wr_lo, preferred_element_type=F32)) + br_ref[...]
    eid = lax.broadcasted_iota(jnp.int32, logits.shape, 1)
    vals, idxs = [], []
    for _ in range(TOP_K):
        m = jnp.max(logits, axis=-1, keepdims=True)
        i = jnp.min(jnp.where(logits == m, eid, N_EXPERTS), axis=-1, keepdims=True)
        vals.append(m)
        idxs.append(i)
        logits = jnp.where(eid == i, -jnp.inf, logits)
    es = [jnp.exp(v - vals[0]) for v in vals]
    denom = es[0] + es[1] + es[2] + es[3]

    @pl.when(pl.program_id(0) == 0)
    def _():
        run_ref[...] = jnp.zeros_like(run_ref)
    tm = logits.shape[0]
    row = lax.broadcasted_iota(jnp.int32, (tm, tm), 0)
    col = lax.broadcasted_iota(jnp.int32, (tm, tm), 1)
    before = jnp.where(col < row, 1.0, 0.0).astype(BF16)
    base = run_ref[...]
    ranks = []
    for k in range(TOP_K):
        onehot = jnp.where(eid == idxs[k], 1.0, 0.0)
        prefix = jnp.dot(before, onehot.astype(BF16), preferred_element_type=F32)
        ranks.append(jnp.sum(onehot * (prefix + base), axis=-1, keepdims=True))
        base = base + jnp.sum(onehot, axis=0, keepdims=True)
    run_ref[...] = base
    cnt_ref[...] = jnp.broadcast_to(base, cnt_ref.shape)

    lane = lax.broadcasted_iota(jnp.int32, idx_ref.shape, 1)
    idx_out = jnp.zeros(idx_ref.shape, jnp.int32)
    rank_out = jnp.zeros(rank_ref.shape, jnp.int32)
    gate_out = jnp.zeros(gate_ref.shape, F32)
    for k in range(TOP_K):
        idx_out = jnp.where(lane == k, idxs[k], idx_out)
        rank_out = jnp.where(lane == k, ranks[k].astype(jnp.int32), rank_out)
        gate_out = jnp.where(lane == k, es[k] / denom, gate_out)
    idx_ref[...] = idx_out
    rank_ref[...] = rank_out
    gate_ref[...] = gate_out


def _outproj(o_sb, o_cv, o_mx, x2, g_sb, g_cv, g_mx, w_out_bf, g_ffn, w_router, b_router):
    T = x2.shape[0]
    rows = lambda n: pl.BlockSpec((ROW_TILE, n), lambda i: (i, 0))
    full = lambda a, b: pl.BlockSpec((a, b), lambda i: (0, 0))
    return pl.pallas_call(
        _outproj_kernel,
        grid=(T // ROW_TILE,),
        in_specs=[
            rows(SB_W), rows(CV_W), rows(MX_W), rows(D_MODEL),
            full(1, SB_W), full(1, CV_W), full(1, MX_W),
            full(D_MODEL, D_MODEL), full(1, D_MODEL),
            full(D_MODEL, N_EXPERTS), full(1, N_EXPERTS),
        ],
        out_specs=[rows(D_MODEL),
                   pl.BlockSpec((ROW_TILE * SUBLANES, LANES), lambda i: (i, 0)),
                   rows(LANES), rows(LANES), rows(LANES),
                   full(SUBLANES, N_EXPERTS)],
        out_shape=[
            jax.ShapeDtypeStruct((T, D_MODEL), F32),
            jax.ShapeDtypeStruct((T * SUBLANES, LANES), F32),
            jax.ShapeDtypeStruct((T, LANES), jnp.int32),
            jax.ShapeDtypeStruct((T, LANES), jnp.int32),
            jax.ShapeDtypeStruct((T, LANES), F32),
            jax.ShapeDtypeStruct((SUBLANES, N_EXPERTS), F32),
        ],
        scratch_shapes=[pltpu.VMEM((1, N_EXPERTS), F32)],
        compiler_params=pltpu.CompilerParams(
            dimension_semantics=("arbitrary",), vmem_limit_bytes=VMEM_LIMIT),
        name="outproj_router",
    )(o_sb, o_cv, o_mx, x2, g_sb, g_cv, g_mx, w_out_bf, g_ffn, w_router, b_router)


def _rows_copy(src_ref, src_row, dst_ref, dst_row, n, sem):
    src = src_ref.at[pl.ds(pl.multiple_of(src_row * SUBLANES, SUBLANES), n * SUBLANES), :]
    dst = dst_ref.at[pl.ds(pl.multiple_of(dst_row * SUBLANES, SUBLANES), n * SUBLANES), :]
    return pltpu.make_async_copy(src, dst, sem)


def _dispatch_kernel(fill_start_ref, fill_rows_ref, dest_ref, hn3_ref, xs3_ref,
                     tile_ref, zero_ref, sem_ref, load_sem_ref, fill_sem_ref, *, tm):
    i = pl.program_id(0)
    n = pl.num_programs(0)
    slot = lax.rem(i, 3)
    slot_next = lax.rem(i + 1, 3)
    n_copies = tm * TOP_K
    pieces = [1 << p for p in range(MOE_BM.bit_length())]

    def fill(wait):
        for e in range(2 * N_EXPERTS):
            rows = fill_rows_ref[e]
            start = fill_start_ref[e]
            for p in pieces:
                @pl.when((rows & p) != 0)
                def _(p=p, start=start, rows=rows):
                    cp = _rows_copy(zero_ref, 0, xs3_ref, start + (rows & (p - 1)), p,
                                    fill_sem_ref.at[0])
                    if wait:
                        cp.wait()
                    else:
                        cp.start()

    @pl.when(i == 0)
    def _():
        zero_ref[...] = jnp.zeros_like(zero_ref)
        fill(wait=False)

    def tile_load(j, s):
        return _rows_copy(hn3_ref, j * tm, tile_ref.at[s], 0, tm, load_sem_ref.at[s])

    def rows_wait(s):
        _rows_copy(tile_ref.at[s], 0, xs3_ref, 0, n_copies, sem_ref.at[s]).wait()

    @pl.when(i == 0)
    def _():
        tile_load(0, 0).start()

    @pl.when(i >= 2)
    def _():
        rows_wait(slot_next)

    @pl.when(i + 1 < n)
    def _():
        tile_load(i + 1, slot_next).start()

    tile_load(i, slot).wait()

    def group(g, _):
        m0 = pl.multiple_of(g * DMA_UNROLL, DMA_UNROLL)
        dests = [dest_ref[0, 0, m0 + u] for u in range(DMA_UNROLL)]
        for u in range(DMA_UNROLL):
            r = g * (DMA_UNROLL // TOP_K) + u // TOP_K
            _rows_copy(tile_ref.at[slot], r, xs3_ref, dests[u], 1,
                       sem_ref.at[slot]).start(priority=u % DMA_PRIORITIES)
        return 0
    lax.fori_loop(0, n_copies // DMA_UNROLL, group, 0)

    @pl.when(i == 0)
    def _():
        fill(wait=True)

    @pl.when(i == n - 1)
    def _():
        @pl.when(i >= 1)
        def _():
            rows_wait(lax.rem(i + 2, 3))
        rows_wait(slot)


def _dispatch(fill_start, fill_rows, dest, hn3, n_rows):
    T = hn3.shape[0] // SUBLANES
    tm = ROW_TILE
    grid_spec = pltpu.PrefetchScalarGridSpec(
        num_scalar_prefetch=2,
        grid=(T // tm,),
        in_specs=[
            pl.BlockSpec((1, 1, tm * TOP_K), lambda i, fs, fr: (i, 0, 0),
                         memory_space=pltpu.SMEM),
            pl.BlockSpec(memory_space=pl.ANY),
        ],
        out_specs=pl.BlockSpec(memory_space=pl.ANY),
        scratch_shapes=[
            pltpu.VMEM((3, tm * SUBLANES, LANES), F32),
            pltpu.VMEM((MOE_BM * SUBLANES, LANES), F32),
            pltpu.SemaphoreType.DMA((3,)),
            pltpu.SemaphoreType.DMA((3,)),
            pltpu.SemaphoreType.DMA((1,)),
        ],
    )
    return pl.pallas_call(
        functools.partial(_dispatch_kernel, tm=tm),
        grid_spec=grid_spec,
        out_shape=jax.ShapeDtypeStruct((n_rows * SUBLANES, LANES), F32),
        compiler_params=pltpu.CompilerParams(
            dimension_semantics=("arbitrary",), vmem_limit_bytes=VMEM_LIMIT),
        name="dispatch",
    )(fill_start, fill_rows, dest.reshape(T // tm, 1, tm * TOP_K), hn3)


def _experts_kernel(bexp_ref, nused_ref, x_ref, wgu_ref, bgu_ref, wd_ref, bd_ref, y_ref,
                    xb_ref, hm_ref, wgu_bf_ref, wd_bf_ref):
    b = pl.program_id(0)

    @pl.when(b >= nused_ref[0])
    def _():
        y_ref[...] = jnp.zeros_like(y_ref)

    @pl.when(b < nused_ref[0])
    def _():
        prev = jnp.maximum(b - 1, 0)
        @pl.when((b == 0) | (bexp_ref[b] != bexp_ref[prev]))
        def _():
            wgu_bf_ref[...] = wgu_ref[...].astype(BF16)
            wd_bf_ref[...] = wd_ref[...].astype(BF16)

        for s in range(SUBLANES):
            xb_ref[:, s * LANES:(s + 1) * LANES] = (
                x_ref[pl.ds(s, MOE_BM, stride=SUBLANES), :].astype(BF16))
        x = xb_ref[...]
        for c0 in range(0, D_FF, FF_CHUNK):
            g_cols = slice(c0, c0 + FF_CHUNK)
            u_cols = slice(D_FF + c0, D_FF + c0 + FF_CHUNK)
            gate = jnp.dot(x, wgu_bf_ref[:, g_cols], preferred_element_type=F32) + bgu_ref[:, g_cols]
            up = jnp.dot(x, wgu_bf_ref[:, u_cols], preferred_element_type=F32) + bgu_ref[:, u_cols]
            gate = jnp.minimum(gate, SWIGLU_LIMIT)
            up = jnp.clip(up, -SWIGLU_LIMIT, SWIGLU_LIMIT)
            hm_ref[:, g_cols] = ((up + 1.0) * (gate * jax.nn.sigmoid(SWIGLU_ALPHA * gate))
                                 ).astype(BF16)
        y = jnp.dot(hm_ref[...], wd_bf_ref[...], preferred_element_type=F32) + bd_ref[...]
        for s in range(SUBLANES):
            y_ref[pl.ds(s, MOE_BM, stride=SUBLANES), :] = y[:, s * LANES:(s + 1) * LANES]


def _experts(block_exp, n_used, xs3, w_gu, b_gu, w_down, b_down):
    NB = block_exp.shape[0]
    used = lambda b, nu: jnp.minimum(b, nu[0] - 1)
    grid_spec = pltpu.PrefetchScalarGridSpec(
        num_scalar_prefetch=2,
        grid=(NB,),
        in_specs=[
            pl.BlockSpec((MOE_BM * SUBLANES, LANES), lambda b, be, nu: (used(b, nu), 0)),
            pl.BlockSpec((None, D_MODEL, 2 * D_FF), lambda b, be, nu: (be[b], 0, 0)),
            pl.BlockSpec((None, 1, 2 * D_FF), lambda b, be, nu: (be[b], 0, 0)),
            pl.BlockSpec((None, D_FF, D_MODEL), lambda b, be, nu: (be[b], 0, 0)),
            pl.BlockSpec((None, 1, D_MODEL), lambda b, be, nu: (be[b], 0, 0)),
        ],
        out_specs=pl.BlockSpec((MOE_BM * SUBLANES, LANES), lambda b, be, nu: (b, 0)),
        scratch_shapes=[
            pltpu.VMEM((MOE_BM, D_MODEL), BF16),
            pltpu.VMEM((MOE_BM, D_FF), BF16),
            pltpu.VMEM((D_MODEL, 2 * D_FF), BF16),
            pltpu.VMEM((D_FF, D_MODEL), BF16),
        ],
    )
    return pl.pallas_call(
        _experts_kernel,
        grid_spec=grid_spec,
        out_shape=jax.ShapeDtypeStruct(xs3.shape, F32),
        compiler_params=pltpu.CompilerParams(
            dimension_semantics=("arbitrary",), vmem_limit_bytes=VMEM_LIMIT),
        name="experts",
    )(block_exp, n_used, xs3, w_gu, b_gu, w_down, b_down)


def _routing_tables(counts, top_idx, rank):
    M = top_idx.shape[0] * TOP_K
    NB = -(-M // MOE_BM) + N_EXPERTS
    nblk_e = (counts + MOE_BM - 1) // MOE_BM
    blk_end = jnp.cumsum(nblk_e)
    row_start = (blk_end - nblk_e) * MOE_BM
    n_used = blk_end[-1]
    blk = jnp.minimum(jnp.arange(NB, dtype=jnp.int32), n_used - 1)
    bexp = jnp.sum((blk[:, None] >= blk_end[None, :]).astype(jnp.int32), axis=1)
    experts = jnp.arange(N_EXPERTS, dtype=jnp.int32)
    dest = rank + jnp.sum(jnp.where(top_idx[:, :, None] == experts, row_start, 0), axis=-1)
    tail = NB - N_EXPERTS + experts
    fill_start = jnp.concatenate([row_start + counts, tail * MOE_BM])
    fill_rows = jnp.concatenate([nblk_e * MOE_BM - counts,
                                 jnp.where(tail >= n_used, MOE_BM, 0)])
    return (bexp.astype(jnp.int32), n_used.reshape(1).astype(jnp.int32),
            dest.astype(jnp.int32), fill_start.astype(jnp.int32),
            fill_rows.astype(jnp.int32), NB * MOE_BM)


def _combine_kernel(dest_ref, dest_next_ref, h_ref, gate_ref, g_ref, ys3_ref, o_ref,
                    buf_ref, sem_ref, *, tm):
    i = pl.program_id(0)
    n = pl.num_programs(0)
    slot = lax.rem(i, 2)

    def start_gather(idx_ref, s):
        def group(g, _):
            m0 = pl.multiple_of(g * DMA_UNROLL, DMA_UNROLL)
            srcs = [idx_ref[0, 0, m0 + u] for u in range(DMA_UNROLL)]
            for u in range(DMA_UNROLL):
                r = g * (DMA_UNROLL // TOP_K) + u // TOP_K
                _rows_copy(ys3_ref, srcs[u], buf_ref.at[s], (u % TOP_K) * tm + r, 1,
                           sem_ref.at[s]).start(priority=u % DMA_PRIORITIES)
            return 0
        lax.fori_loop(0, tm * TOP_K // DMA_UNROLL, group, 0)

    @pl.when(i == 0)
    def _():
        start_gather(dest_ref, 0)

    @pl.when(i + 1 < n)
    def _():
        start_gather(dest_next_ref, 1 - slot)

    _rows_copy(ys3_ref, 0, buf_ref.at[slot], 0, tm * TOP_K, sem_ref.at[slot]).wait()

    gates = gate_ref[...]
    chunks = []
    ssq = jnp.zeros((tm, 1), F32)
    for s in range(SUBLANES):
        acc = h_ref[:, s * LANES:(s + 1) * LANES]
        for k in range(TOP_K):
            acc = acc + gates[:, k:k + 1] * buf_ref[
                slot, pl.ds(k * tm * SUBLANES + s, tm, stride=SUBLANES), :]
        chunks.append(acc)
        ssq = ssq + jnp.sum(acc * acc, axis=-1, keepdims=True)
    inv = lax.rsqrt(ssq * (1.0 / D_MODEL) + EPS)
    for s in range(SUBLANES):
        o_ref[:, s * LANES:(s + 1) * LANES] = chunks[s] * inv * g_ref[:, s * LANES:(s + 1) * LANES]


def _combine(h, ys3, dest, gates, g_final):
    T = h.shape[0]
    tm = COMBINE_TILE
    n = T // tm
    table = dest.reshape(n, 1, tm * TOP_K)
    idx_spec = lambda f: pl.BlockSpec((1, 1, tm * TOP_K), lambda i: (f(i), 0, 0),
                                      memory_space=pltpu.SMEM)
    return pl.pallas_call(
        functools.partial(_combine_kernel, tm=tm),
        grid=(n,),
        in_specs=[
            idx_spec(lambda i: i), idx_spec(lambda i: jnp.minimum(i + 1, n - 1)),
            pl.BlockSpec((tm, D_MODEL), lambda i: (i, 0)),
            pl.BlockSpec((tm, LANES), lambda i: (i, 0)),
            pl.BlockSpec((1, D_MODEL), lambda i: (0, 0)),
            pl.BlockSpec(memory_space=pl.ANY),
        ],
        out_specs=pl.BlockSpec((tm, D_MODEL), lambda i: (i, 0)),
        out_shape=jax.ShapeDtypeStruct((T, D_MODEL), F32),
        scratch_shapes=[
            pltpu.VMEM((2, tm * TOP_K * SUBLANES, LANES), F32),
            pltpu.SemaphoreType.DMA((2,)),
        ],
        compiler_params=pltpu.CompilerParams(
            dimension_semantics=("arbitrary",), vmem_limit_bytes=VMEM_LIMIT),
        name="combine",
    )(table, table, h, gates, g_final, ys3)


def kernel(x, mem, g_attn_norm, w_in, b_glu, w_dw, b_dw, g_cv_ln, b_cv_ln, w_pw2, b_pw2, g_mem, w_mem_kv, g_sb_out, g_cv_out, g_mx_out, w_out, g_ffn_norm, w_router, b_router, w_gu, b_gu, w_down, b_down, g_final):
    B, S, D = x.shape
    T = B * S
    assert D == D_MODEL and S % ROW_TILE == 0 and g_attn_norm.shape[0] == 1
    l = 0
    row = lambda v: v.reshape(1, -1)
    x2 = x.reshape(T, D)

    proj = _inproj(x2, row(g_attn_norm[l]), w_in[l].astype(BF16))
    proj3 = proj.reshape(B, S, IN_W)
    o_sb = _sb_attention(proj3)
    o_cv = _conformer(proj3, row(b_glu[l]), w_dw[l].reshape(CONV_K, CV_W), row(b_dw[l]),
                      row(g_cv_ln[l]), row(b_cv_ln[l]), w_pw2[l].astype(BF16), row(b_pw2[l]))
    o_mx = _memx(mem, row(g_mem[l]), w_mem_kv[l].astype(BF16), proj3)

    h, hn3, idx_pad, rank_pad, gate_pad, counts = _outproj(
        o_sb.reshape(T, SB_W), o_cv.reshape(T, CV_W), o_mx.reshape(T, MX_W), x2,
        row(g_sb_out[l]), row(g_cv_out[l]), row(g_mx_out[l]), w_out[l].astype(BF16),
        row(g_ffn_norm[l]), w_router[l], row(b_router[l]))

    bexp, n_used, dest, fill_start, fill_rows, n_rows = _routing_tables(
        counts[0].astype(jnp.int32), idx_pad[:, :TOP_K], rank_pad[:, :TOP_K])
    xs3 = _dispatch(fill_start, fill_rows, dest, hn3, n_rows)
    ys3 = _experts(bexp, n_used, xs3, w_gu[l], b_gu[l].reshape(N_EXPERTS, 1, 2 * D_FF),
                   w_down[l], b_down[l].reshape(N_EXPERTS, 1, D_MODEL))
    out = _combine(h, ys3, dest, gate_pad, row(g_final))
    return out.reshape(B, S, D)
```

```python
import functools

import jax
import jax.numpy as jnp
from jax import lax
from jax.experimental import pallas as pl
from jax.experimental.pallas import tpu as pltpu

F32 = jnp.float32
BF16 = jnp.bfloat16

D_MODEL = 1024
HEAD_DIM = 64
SB_W = 512
CV_W = 256
MX_W = 256
IN_W = 3 * SB_W + 2 * CV_W + MX_W
CONV_K = 31
N_MEM = 256
N_EXPERTS = 32
TOP_K = 4
D_FF = 1024
SWIGLU_ALPHA = 1.702
SWIGLU_LIMIT = 7.0
EPS = 1e-6

LANES = 128
SUBLANES = 8
CHUNKS = D_MODEL // LANES
ROW_TILE = 512
SB_TILE = 256
SB_EXP_ZERO_BELOW = -104.0
CONV_PAD = 32
CONV_ROWS = 128
MX_ROWS = 512
MOE_BM = 512
FF_CHUNK = 512
COMBINE_TILE = 256
DMA_UNROLL = 16
DMA_PRIORITIES = 2
VMEM_LIMIT = 56 * 1024 * 1024


def _rms(x, g):
    return x * lax.rsqrt(jnp.mean(x * x, axis=-1, keepdims=True) + EPS) * g


def _inproj_kernel(x_ref, g_ref, w_ref, o_ref):
    xn = _rms(x_ref[...], g_ref[...]).astype(BF16)
    o_ref[...] = jnp.dot(xn, w_ref[...], preferred_element_type=F32).astype(BF16)


def _inproj(x2, g, w_bf):
    T = x2.shape[0]
    return pl.pallas_call(
        _inproj_kernel,
        grid=(T // ROW_TILE,),
        in_specs=[
            pl.BlockSpec((ROW_TILE, D_MODEL), lambda i: (i, 0)),
            pl.BlockSpec((1, D_MODEL), lambda i: (0, 0)),
            pl.BlockSpec((D_MODEL, IN_W), lambda i: (0, 0)),
        ],
        out_specs=pl.BlockSpec((ROW_TILE, IN_W), lambda i: (i, 0)),
        out_shape=jax.ShapeDtypeStruct((T, IN_W), BF16),
        compiler_params=pltpu.CompilerParams(
            dimension_semantics=("arbitrary",), vmem_limit_bytes=VMEM_LIMIT),
        name="inproj",
    )(x2, g, w_bf)


def _sb_kernel(q_ref, k_ref, v_ref, o_ref, *, seq):
    lane = lax.broadcasted_iota(jnp.int32, (SB_TILE, LANES), 1)
    head0 = lane < HEAD_DIM
    row = lax.broadcasted_iota(jnp.int32, (SB_TILE, SB_TILE), 0)
    col = lax.broadcasted_iota(jnp.int32, (SB_TILE, SB_TILE), 1)
    tri = jnp.where(row > col, 1.0, 0.0).astype(BF16)
    dmask = col < row

    def tile(qh, s0, c, diag):
        kb = k_ref[pl.ds(s0, SB_TILE), :]
        z = lax.dot_general(qh, kb, (((1,), (1,)), ((), ())), preferred_element_type=F32)
        lb = jnp.minimum(z, 0.0) - jnp.log(1.0 + jnp.exp(-jnp.abs(z)))
        l1m = lb - z
        if diag:
            l1m = jnp.where(dmask, l1m, 0.0)
        after = jnp.dot(l1m.astype(BF16), tri, preferred_element_type=F32)
        rowsum = jnp.broadcast_to(jnp.sum(l1m, axis=1, keepdims=True), (SB_TILE, LANES))
        arg = lb + after
        if c is not None:
            arg = arg + jnp.concatenate([c] * (SB_TILE // LANES), axis=1)
        a = jnp.exp(arg)
        if diag:
            a = jnp.where(dmask, a, 0.0)
        pv = jnp.dot(a.astype(BF16), v_ref[pl.ds(s0, SB_TILE), :], preferred_element_type=F32)
        return pv, rowsum

    def alive(c0, c1):
        return jnp.max(jnp.maximum(c0, c1)) >= SB_EXP_ZERO_BELOW

    def heads(t0):
        q = q_ref[pl.ds(t0, SB_TILE), :] * (HEAD_DIM ** -0.5)
        return jnp.where(head0, q, jnp.zeros_like(q)), jnp.where(head0, jnp.zeros_like(q), q)

    def store(t0, acc0, acc1):
        o_ref[pl.ds(t0, SB_TILE), :] = jnp.where(head0, acc0, acc1).astype(BF16)

    q0, q1 = heads(0)
    store(0, tile(q0, 0, None, True)[0], tile(q1, 0, None, True)[0])

    def qtile(i, _):
        t0 = pl.multiple_of(i * SB_TILE, SB_TILE)
        s0 = pl.multiple_of((i - 1) * SB_TILE, SB_TILE)
        q0, q1 = heads(t0)
        acc0, c0 = tile(q0, t0, None, True)
        acc1, c1 = tile(q1, t0, None, True)
        pv0, r0 = tile(q0, s0, c0, False)
        pv1, r1 = tile(q1, s0, c1, False)
        acc0, acc1, c0, c1 = acc0 + pv0, acc1 + pv1, c0 + r0, c1 + r1

        def cond(st):
            return (st[0] >= 0) & st[5]

        def body(st):
            j, acc0, acc1, c0, c1, _ = st
            s0 = pl.multiple_of(j * SB_TILE, SB_TILE)
            pv0, r0 = tile(q0, s0, c0, False)
            pv1, r1 = tile(q1, s0, c1, False)
            c0 = c0 + r0
            c1 = c1 + r1
            return j - 1, acc0 + pv0, acc1 + pv1, c0, c1, alive(c0, c1)

        st = lax.while_loop(cond, body, (i - 2, acc0, acc1, c0, c1, alive(c0, c1)))
        store(t0, st[1], st[2])
        return 0

    lax.fori_loop(1, seq // SB_TILE, qtile, 0)


def _sb_attention(proj3):
    B, S, _ = proj3.shape
    pairs = SB_W // LANES
    return pl.pallas_call(
        functools.partial(_sb_kernel, seq=S),
        grid=(B, pairs),
        in_specs=[
            pl.BlockSpec((None, S, LANES), lambda b, p: (b, 0, p)),
            pl.BlockSpec((None, S, LANES), lambda b, p: (b, 0, pairs + p)),
            pl.BlockSpec((None, S, LANES), lambda b, p: (b, 0, 2 * pairs + p)),
        ],
        out_specs=pl.BlockSpec((None, S, LANES), lambda b, p: (b, 0, p)),
        out_shape=jax.ShapeDtypeStruct((B, S, SB_W), BF16),
        compiler_params=pltpu.CompilerParams(
            dimension_semantics=("arbitrary", "arbitrary"), vmem_limit_bytes=VMEM_LIMIT),
        name="sb_attention",
    )(proj3, proj3, proj3)


def _conv_kernel(glu_ref, bglu_ref, wdw_ref, bdw_ref, gln_ref, bln_ref, wpw_ref, bpw_ref,
                 o_ref, upad_ref, shift_ref, *, seq):
    upad_ref[0:CONV_PAD, :] = jnp.zeros((CONV_PAD, CV_W), F32)
    for c in range(seq // CONV_ROWS):
        r0 = c * CONV_ROWS
        g = glu_ref[r0:r0 + CONV_ROWS, :].astype(F32) + bglu_ref[...]
        upad_ref[CONV_PAD + r0:CONV_PAD + r0 + CONV_ROWS, :] = (
            g[:, :CV_W] * jax.nn.sigmoid(g[:, CV_W:]))
    for c in range(seq // CONV_ROWS):
        r0 = c * CONV_ROWS
        acc = jnp.zeros((CONV_ROWS, CV_W), F32) + bdw_ref[...]
        span = CONV_ROWS + CONV_PAD - SUBLANES
        for r in range(1, SUBLANES):
            shift_ref[r, 0:span, :] = upad_ref[r0 + r:r0 + r + span, :]
        for k in range(CONV_K):
            off = CONV_PAD - (CONV_K - 1) + k
            r = off % SUBLANES
            a = off - r
            if r == 0:
                tap = upad_ref[r0 + a:r0 + a + CONV_ROWS, :]
            else:
                tap = shift_ref[r, a:a + CONV_ROWS, :]
            acc = acc + tap * wdw_ref[k:k + 1, :]
        mu = jnp.mean(acc, axis=-1, keepdims=True)
        d = acc - mu
        var = jnp.mean(d * d, axis=-1, keepdims=True)
        y = d * lax.rsqrt(var + EPS) * gln_ref[...] + bln_ref[...]
        y = y * jax.nn.sigmoid(y)
        out = jnp.dot(y.astype(BF16), wpw_ref[...], preferred_element_type=F32) + bpw_ref[...]
        o_ref[r0:r0 + CONV_ROWS, :] = out.astype(BF16)


def _conformer(proj3, b_glu, w_dw, b_dw, g_ln, b_ln, w_pw_bf, b_pw):
    B, S, _ = proj3.shape
    glu_block = (3 * SB_W) // (2 * CV_W)
    vec = lambda n: pl.BlockSpec((1, n), lambda b: (0, 0))
    return pl.pallas_call(
        functools.partial(_conv_kernel, seq=S),
        grid=(B,),
        in_specs=[
            pl.BlockSpec((None, S, 2 * CV_W), lambda b: (b, 0, glu_block)),
            vec(2 * CV_W),
            pl.BlockSpec((CONV_K, CV_W), lambda b: (0, 0)),
            vec(CV_W), vec(CV_W), vec(CV_W),
            pl.BlockSpec((CV_W, CV_W), lambda b: (0, 0)),
            vec(CV_W),
        ],
        out_specs=pl.BlockSpec((None, S, CV_W), lambda b: (b, 0, 0)),
        out_shape=jax.ShapeDtypeStruct((B, S, CV_W), BF16),
        scratch_shapes=[pltpu.VMEM((CONV_PAD + S, CV_W), F32),
                        pltpu.VMEM((SUBLANES, CONV_ROWS + CONV_PAD, CV_W), F32)],
        compiler_params=pltpu.CompilerParams(
            dimension_semantics=("arbitrary",), vmem_limit_bytes=VMEM_LIMIT),
        name="conformer",
    )(proj3, b_glu, w_dw, b_dw, g_ln, b_ln, w_pw_bf, b_pw)


def _memx_kernel(mem_ref, gm_ref, wkv_ref, q_ref, o_ref, *, seq):
    scale = HEAD_DIM ** -0.5
    mn = _rms(mem_ref[...], gm_ref[...]).astype(BF16)
    kv = jnp.dot(mn, wkv_ref[...], preferred_element_type=F32)
    km = kv[:, :MX_W].astype(BF16)
    vm = kv[:, MX_W:].astype(BF16)
    lane = lax.broadcasted_iota(jnp.int32, (MX_ROWS, MX_W), 1)

    def chunk(c, _):
        r0 = pl.multiple_of(c * MX_ROWS, MX_ROWS)
        q = q_ref[pl.ds(r0, MX_ROWS), :]
        out = jnp.zeros((MX_ROWS, MX_W), F32)
        for h in range(MX_W // HEAD_DIM):
            head = (lane >= HEAD_DIM * h) & (lane < HEAD_DIM * (h + 1))
            qh = jnp.where(head, q, jnp.zeros_like(q))
            s = lax.dot_general(qh, km, (((1,), (1,)), ((), ())),
                                preferred_element_type=F32) * scale
            p = jnp.exp(s - jnp.max(s, axis=-1, keepdims=True))
            p = p / jnp.sum(p, axis=-1, keepdims=True)
            oh = jnp.dot(p.astype(BF16), vm, preferred_element_type=F32)
            out = jnp.where(head, oh, out)
        o_ref[pl.ds(r0, MX_ROWS), :] = out.astype(BF16)
        return 0

    lax.fori_loop(0, seq // MX_ROWS, chunk, 0)


def _memx(mem, g_mem, w_kv_bf, proj3):
    B, S, _ = proj3.shape
    q_block = (3 * SB_W + 2 * CV_W) // MX_W
    return pl.pallas_call(
        functools.partial(_memx_kernel, seq=S),
        grid=(B,),
        in_specs=[
            pl.BlockSpec((None, N_MEM, D_MODEL), lambda b: (b, 0, 0)),
            pl.BlockSpec((1, D_MODEL), lambda b: (0, 0)),
            pl.BlockSpec((D_MODEL, 2 * MX_W), lambda b: (0, 0)),
            pl.BlockSpec((None, S, MX_W), lambda b: (b, 0, q_block)),
        ],
        out_specs=pl.BlockSpec((None, S, MX_W), lambda b: (b, 0, 0)),
        out_shape=jax.ShapeDtypeStruct((B, S, MX_W), BF16),
        compiler_params=pltpu.CompilerParams(
            dimension_semantics=("arbitrary",), vmem_limit_bytes=VMEM_LIMIT),
        name="memx",
    )(mem, g_mem, w_kv_bf, proj3)


def _outproj_kernel(sb_ref, cv_ref, mx_ref, x_ref, gsb_ref, gcv_ref, gmx_ref, wo_ref,
                    gffn_ref, wr_ref, br_ref, h_ref, hn3_ref, idx_ref, rank_ref, gate_ref,
                    cnt_ref, run_ref):
    def normed(o_ref, g_ref):
        return _rms(o_ref[...].astype(F32), g_ref[...]).astype(BF16)

    mix = jnp.dot(normed(sb_ref, gsb_ref), wo_ref[0:SB_W, :], preferred_element_type=F32)
    mix += jnp.dot(normed(cv_ref, gcv_ref), wo_ref[SB_W:SB_W + CV_W, :],
                   preferred_element_type=F32)
    mix += jnp.dot(normed(mx_ref, gmx_ref), wo_ref[SB_W + CV_W:, :],
                   preferred_element_type=F32)
    h = x_ref[...] + mix
    h_ref[...] = h
    hn = _rms(h, gffn_ref[...])
    for s in range(CHUNKS):
        hn3_ref[s] = hn[:, s * LANES:(s + 1) * LANES]
    wr = wr_ref[...]
    hn_hi = hn.astype(BF16)
    hn_lo = (hn - hn_hi.astype(F32)).astype(BF16)
    wr_hi = wr.astype(BF16)
    wr_lo = (wr - wr_hi.astype(F32)).astype(BF16)
    logits = (jnp.dot(hn_hi, wr_hi, preferred_element_type=F32)
              + jnp.dot(hn_lo, wr_hi, preferred_element_type=F32)
              + jnp.dot(hn_hi, wr_lo, preferred_element_type=F32)) + br_ref[...]
    eid = lax.broadcasted_iota(jnp.int32, logits.shape, 1)
    vals, idxs = [], []
    for _ in range(TOP_K):
        m = jnp.max(logits, axis=-1, keepdims=True)
        i = jnp.min(jnp.where(logits == m, eid, N_EXPERTS), axis=-1, keepdims=True)
        vals.append(m)
        idxs.append(i)
        logits = jnp.where(eid == i, -jnp.inf, logits)
    es = [jnp.exp(v - vals[0]) for v in vals]
    denom = es[0] + es[1] + es[2] + es[3]

    @pl.when(pl.program_id(0) == 0)
    def _():
        run_ref[...] = jnp.zeros_like(run_ref)
    tm = logits.shape[0]
    row = lax.broadcasted_iota(jnp.int32, (tm, tm), 0)
    col = lax.broadcasted_iota(jnp.int32, (tm, tm), 1)
    before = jnp.where(col < row, 1.0, 0.0).astype(BF16)
    base = run_ref[...]
    ranks = []
    for k in range(TOP_K):
        onehot = jnp.where(eid == idxs[k], 1.0, 0.0)
        prefix = jnp.dot(before, onehot.astype(BF16), preferred_element_type=F32)
        ranks.append(jnp.sum(onehot * (prefix + base), axis=-1, keepdims=True))
        base = base + jnp.sum(onehot, axis=0, keepdims=True)
    run_ref[...] = base
    cnt_ref[...] = jnp.broadcast_to(base, cnt_ref.shape)

    lane = lax.broadcasted_iota(jnp.int32, idx_ref.shape, 1)
    idx_out = jnp.zeros(idx_ref.shape, jnp.int32)
    rank_out = jnp.zeros(rank_ref.shape, jnp.int32)
    gate_out = jnp.zeros(gate_ref.shape, F32)
    for k in range(TOP_K):
        idx_out = jnp.where(lane == k, idxs[k], idx_out)
        rank_out = jnp.where(lane == k, ranks[k].astype(jnp.int32), rank_out)
        gate_out = jnp.where(lane == k, es[k] / denom, gate_out)
    idx_ref[...] = idx_out
    rank_ref[...] = rank_out
    gate_ref[...] = gate_out


def _outproj(o_sb, o_cv, o_mx, x2, g_sb, g_cv, g_mx, w_out_bf, g_ffn, w_router, b_router):
    T = x2.shape[0]
    rows = lambda n: pl.BlockSpec((ROW_TILE, n), lambda i: (i, 0))
    full = lambda a, b: pl.BlockSpec((a, b), lambda i: (0, 0))
    return pl.pallas_call(
        _outproj_kernel,
        grid=(T // ROW_TILE,),
        in_specs=[
            rows(SB_W), rows(CV_W), rows(MX_W), rows(D_MODEL),
            full(1, SB_W), full(1, CV_W), full(1, MX_W),
            full(D_MODEL, D_MODEL), full(1, D_MODEL),
            full(D_MODEL, N_EXPERTS), full(1, N_EXPERTS),
        ],
        out_specs=[rows(D_MODEL),
                   pl.BlockSpec((CHUNKS, ROW_TILE, LANES), lambda i: (0, i, 0)),
                   rows(LANES), rows(LANES), rows(LANES),
                   full(SUBLANES, N_EXPERTS)],
        out_shape=[
            jax.ShapeDtypeStruct((T, D_MODEL), F32),
            jax.ShapeDtypeStruct((CHUNKS, T, LANES), F32),
            jax.ShapeDtypeStruct((T, LANES), jnp.int32),
            jax.ShapeDtypeStruct((T, LANES), jnp.int32),
            jax.ShapeDtypeStruct((T, LANES), F32),
            jax.ShapeDtypeStruct((SUBLANES, N_EXPERTS), F32),
        ],
        scratch_shapes=[pltpu.VMEM((1, N_EXPERTS), F32)],
        compiler_params=pltpu.CompilerParams(
            dimension_semantics=("arbitrary",), vmem_limit_bytes=VMEM_LIMIT),
        name="outproj_router",
    )(o_sb, o_cv, o_mx, x2, g_sb, g_cv, g_mx, w_out_bf, g_ffn, w_router, b_router)


def _tile_rows(ref, row, n):
    return ref.at[:, pl.ds(row, n), :]


_sorted_rows = _tile_rows


def _dispatch_kernel(fill_start_ref, fill_rows_ref, dest_ref, hn3_ref, xs3_ref,
                     tile_ref, zero_ref, sem_ref, load_sem_ref, fill_sem_ref, *, tm):
    i = pl.program_id(0)
    n = pl.num_programs(0)
    slot = lax.rem(i, 3)
    slot_next = lax.rem(i + 1, 3)
    n_copies = tm * TOP_K
    pieces = [1 << p for p in range(MOE_BM.bit_length())]

    def fill(wait):
        for e in range(2 * N_EXPERTS):
            rows = fill_rows_ref[e]
            start = fill_start_ref[e]
            for p in pieces:
                @pl.when((rows & p) != 0)
                def _(p=p, start=start, rows=rows):
                    cp = pltpu.make_async_copy(
                        _tile_rows(zero_ref, 0, p),
                        _sorted_rows(xs3_ref, start + (rows & (p - 1)), p), fill_sem_ref.at[0])
                    if wait:
                        cp.wait()
                    else:
                        cp.start()

    @pl.when(i == 0)
    def _():
        zero_ref[...] = jnp.zeros_like(zero_ref)
        fill(wait=False)

    def tile_load(j, s):
        return pltpu.make_async_copy(_tile_rows(hn3_ref, pl.multiple_of(j * tm, tm), tm),
                                     tile_ref.at[s], load_sem_ref.at[s])

    def rows_wait(s):
        for _ in range(TOP_K):
            pltpu.make_async_copy(tile_ref.at[s], _sorted_rows(xs3_ref, 0, tm),
                                  sem_ref.at[s]).wait()

    @pl.when(i == 0)
    def _():
        tile_load(0, 0).start()

    @pl.when(i >= 2)
    def _():
        rows_wait(slot_next)

    @pl.when(i + 1 < n)
    def _():
        tile_load(i + 1, slot_next).start()

    tile_load(i, slot).wait()

    def group(g, _):
        m0 = pl.multiple_of(g * DMA_UNROLL, DMA_UNROLL)
        dests = [dest_ref[0, 0, m0 + u] for u in range(DMA_UNROLL)]
        for u in range(DMA_UNROLL):
            r = g * (DMA_UNROLL // TOP_K) + u // TOP_K
            pltpu.make_async_copy(
                _tile_rows(tile_ref.at[slot], r, 1), _sorted_rows(xs3_ref, dests[u], 1),
                sem_ref.at[slot]).start(priority=u % DMA_PRIORITIES)
        return 0
    lax.fori_loop(0, n_copies // DMA_UNROLL, group, 0)

    @pl.when(i == 0)
    def _():
        fill(wait=True)

    @pl.when(i == n - 1)
    def _():
        @pl.when(i >= 1)
        def _():
            rows_wait(lax.rem(i + 2, 3))
        rows_wait(slot)


def _dispatch(fill_start, fill_rows, dest, hn3, n_blocks):
    T = hn3.shape[1]
    tm = ROW_TILE
    assert tm <= MOE_BM and MOE_BM & (MOE_BM - 1) == 0
    grid_spec = pltpu.PrefetchScalarGridSpec(
        num_scalar_prefetch=2,
        grid=(T // tm,),
        in_specs=[
            pl.BlockSpec((1, 1, tm * TOP_K), lambda i, fs, fr: (i, 0, 0),
                         memory_space=pltpu.SMEM),
            pl.BlockSpec(memory_space=pl.ANY),
        ],
        out_specs=pl.BlockSpec(memory_space=pl.ANY),
        scratch_shapes=[
            pltpu.VMEM((3, CHUNKS, tm, LANES), F32),
            pltpu.VMEM((CHUNKS, MOE_BM, LANES), F32),
            pltpu.SemaphoreType.DMA((3,)),
            pltpu.SemaphoreType.DMA((3,)),
            pltpu.SemaphoreType.DMA((1,)),
        ],
    )
    return pl.pallas_call(
        functools.partial(_dispatch_kernel, tm=tm),
        grid_spec=grid_spec,
        out_shape=jax.ShapeDtypeStruct((CHUNKS, n_blocks * MOE_BM, LANES), F32),
        compiler_params=pltpu.CompilerParams(
            dimension_semantics=("arbitrary",), vmem_limit_bytes=VMEM_LIMIT),
        name="dispatch",
    )(fill_start, fill_rows, dest.reshape(T // tm, 1, tm * TOP_K), hn3)


def _experts_kernel(bexp_ref, nused_ref, x_ref, wgu_ref, bgu_ref, wd_ref, bd_ref, y_ref,
                    hm_ref, wgu_bf_ref, wd_bf_ref):
    b = pl.program_id(0)

    @pl.when(b >= nused_ref[0])
    def _():
        y_ref[...] = jnp.zeros_like(y_ref)

    @pl.when(b < nused_ref[0])
    def _():
        prev = jnp.maximum(b - 1, 0)
        @pl.when((b == 0) | (bexp_ref[b] != bexp_ref[prev]))
        def _():
            wgu_bf_ref[...] = wgu_ref[...].astype(BF16)
            wd_bf_ref[...] = wd_ref[...].astype(BF16)

        x = jnp.concatenate([x_ref[s].astype(BF16) for s in range(CHUNKS)], axis=1)
        for c0 in range(0, D_FF, FF_CHUNK):
            g_cols = slice(c0, c0 + FF_CHUNK)
            u_cols = slice(D_FF + c0, D_FF + c0 + FF_CHUNK)
            gate = jnp.dot(x, wgu_bf_ref[:, g_cols], preferred_element_type=F32) + bgu_ref[:, g_cols]
            up = jnp.dot(x, wgu_bf_ref[:, u_cols], preferred_element_type=F32) + bgu_ref[:, u_cols]
            gate = jnp.minimum(gate, SWIGLU_LIMIT)
            up = jnp.clip(up, -SWIGLU_LIMIT, SWIGLU_LIMIT)
            hm_ref[:, g_cols] = ((up + 1.0) * (gate * jax.nn.sigmoid(SWIGLU_ALPHA * gate))
                                 ).astype(BF16)
        y = jnp.dot(hm_ref[...], wd_bf_ref[...], preferred_element_type=F32) + bd_ref[...]
        for s in range(CHUNKS):
            y_ref[s] = y[:, s * LANES:(s + 1) * LANES]


def _experts(block_exp, n_used, xs3, w_gu, b_gu, w_down, b_down):
    NB = block_exp.shape[0]
    used = lambda b, nu: jnp.minimum(b, nu[0] - 1)
    grid_spec = pltpu.PrefetchScalarGridSpec(
        num_scalar_prefetch=2,
        grid=(NB,),
        in_specs=[
            pl.BlockSpec((CHUNKS, MOE_BM, LANES), lambda b, be, nu: (0, used(b, nu), 0)),
            pl.BlockSpec((None, D_MODEL, 2 * D_FF), lambda b, be, nu: (be[b], 0, 0)),
            pl.BlockSpec((None, 1, 2 * D_FF), lambda b, be, nu: (be[b], 0, 0)),
            pl.BlockSpec((None, D_FF, D_MODEL), lambda b, be, nu: (be[b], 0, 0)),
            pl.BlockSpec((None, 1, D_MODEL), lambda b, be, nu: (be[b], 0, 0)),
        ],
        out_specs=pl.BlockSpec((CHUNKS, MOE_BM, LANES), lambda b, be, nu: (0, b, 0)),
        scratch_shapes=[
            pltpu.VMEM((MOE_BM, D_FF), BF16),
            pltpu.VMEM((D_MODEL, 2 * D_FF), BF16),
            pltpu.VMEM((D_FF, D_MODEL), BF16),
        ],
    )
    return pl.pallas_call(
        _experts_kernel,
        grid_spec=grid_spec,
        out_shape=jax.ShapeDtypeStruct(xs3.shape, F32),
        compiler_params=pltpu.CompilerParams(
            dimension_semantics=("arbitrary",), vmem_limit_bytes=VMEM_LIMIT),
        name="experts",
    )(block_exp, n_used, xs3, w_gu, b_gu, w_down, b_down)


def _routing_tables(counts, top_idx, rank):
    M = top_idx.shape[0] * TOP_K
    NB = -(-M // MOE_BM) + N_EXPERTS
    nblk_e = (counts + MOE_BM - 1) // MOE_BM
    blk_end = jnp.cumsum(nblk_e)
    row_start = (blk_end - nblk_e) * MOE_BM
    n_used = blk_end[-1]
    blk = jnp.minimum(jnp.arange(NB, dtype=jnp.int32), n_used - 1)
    bexp = jnp.sum((blk[:, None] >= blk_end[None, :]).astype(jnp.int32), axis=1)
    experts = jnp.arange(N_EXPERTS, dtype=jnp.int32)
    dest = rank + jnp.sum(jnp.where(top_idx[:, :, None] == experts, row_start, 0), axis=-1)
    tail = NB - N_EXPERTS + experts
    fill_start = jnp.concatenate([row_start + counts, tail * MOE_BM])
    fill_rows = jnp.concatenate([nblk_e * MOE_BM - counts,
                                 jnp.where(tail >= n_used, MOE_BM, 0)])
    return (bexp.astype(jnp.int32), n_used.reshape(1).astype(jnp.int32),
            dest.astype(jnp.int32), fill_start.astype(jnp.int32),
            fill_rows.astype(jnp.int32), NB)


def _combine_kernel(dest_ref, dest_next_ref, h_ref, gate_ref, g_ref, ys3_ref, o_ref,
                    buf_ref, sem_ref, *, tm):
    i = pl.program_id(0)
    n = pl.num_programs(0)
    slot = lax.rem(i, 2)

    def start_gather(idx_ref, s):
        def group(g, _):
            m0 = pl.multiple_of(g * DMA_UNROLL, DMA_UNROLL)
            srcs = [idx_ref[0, 0, m0 + u] for u in range(DMA_UNROLL)]
            for u in range(DMA_UNROLL):
                r = g * (DMA_UNROLL // TOP_K) + u // TOP_K
                pltpu.make_async_copy(
                    _sorted_rows(ys3_ref, srcs[u], 1),
                    _tile_rows(buf_ref.at[s, u % TOP_K], r, 1),
                    sem_ref.at[s]).start(priority=u % DMA_PRIORITIES)
            return 0
        lax.fori_loop(0, tm * TOP_K // DMA_UNROLL, group, 0)

    @pl.when(i == 0)
    def _():
        start_gather(dest_ref, 0)

    @pl.when(i + 1 < n)
    def _():
        start_gather(dest_next_ref, 1 - slot)

    for k in range(TOP_K):
        pltpu.make_async_copy(_sorted_rows(ys3_ref, 0, tm), buf_ref.at[slot, k],
                              sem_ref.at[slot]).wait()

    gates = gate_ref[...]
    chunks = []
    ssq = jnp.zeros((tm, 1), F32)
    for s in range(CHUNKS):
        acc = h_ref[:, s * LANES:(s + 1) * LANES]
        for k in range(TOP_K):
            acc = acc + gates[:, k:k + 1] * buf_ref[slot, k, s]
        chunks.append(acc)
        ssq = ssq + jnp.sum(acc * acc, axis=-1, keepdims=True)
    inv = lax.rsqrt(ssq * (1.0 / D_MODEL) + EPS)
    for s in range(CHUNKS):
        o_ref[:, s * LANES:(s + 1) * LANES] = chunks[s] * inv * g_ref[:, s * LANES:(s + 1) * LANES]


def _combine(h, ys3, dest, gates, g_final):
    T = h.shape[0]
    tm = COMBINE_TILE
    n = T // tm
    table = dest.reshape(n, 1, tm * TOP_K)
    idx_spec = lambda f: pl.BlockSpec((1, 1, tm * TOP_K), lambda i: (f(i), 0, 0),
                                      memory_space=pltpu.SMEM)
    return pl.pallas_call(
        functools.partial(_combine_kernel, tm=tm),
        grid=(n,),
        in_specs=[
            idx_spec(lambda i: i), idx_spec(lambda i: jnp.minimum(i + 1, n - 1)),
            pl.BlockSpec((tm, D_MODEL), lambda i: (i, 0)),
            pl.BlockSpec((tm, LANES), lambda i: (i, 0)),
            pl.BlockSpec((1, D_MODEL), lambda i: (0, 0)),
            pl.BlockSpec(memory_space=pl.ANY),
        ],
        out_specs=pl.BlockSpec((tm, D_MODEL), lambda i: (i, 0)),
        out_shape=jax.ShapeDtypeStruct((T, D_MODEL), F32),
        scratch_shapes=[
            pltpu.VMEM((2, TOP_K, CHUNKS, tm, LANES), F32),
            pltpu.SemaphoreType.DMA((2,)),
        ],
        compiler_params=pltpu.CompilerParams(
            dimension_semantics=("arbitrary",), vmem_limit_bytes=VMEM_LIMIT),
        name="combine",
    )(table, table, h, gates, g_final, ys3)


def kernel(x, mem, g_attn_norm, w_in, b_glu, w_dw, b_dw, g_cv_ln, b_cv_ln, w_pw2, b_pw2, g_mem, w_mem_kv, g_sb_out, g_cv_out, g_mx_out, w_out, g_ffn_norm, w_router, b_router, w_gu, b_gu, w_down, b_down, g_final):
    B, S, D = x.shape
    T = B * S
    assert D == D_MODEL and S % ROW_TILE == 0 and g_attn_norm.shape[0] == 1
    l = 0
    row = lambda v: v.reshape(1, -1)
    x2 = x.reshape(T, D)

    proj = _inproj(x2, row(g_attn_norm[l]), w_in[l].astype(BF16))
    proj3 = proj.reshape(B, S, IN_W)
    o_sb = _sb_attention(proj3)
    o_cv = _conformer(proj3, row(b_glu[l]), w_dw[l].reshape(CONV_K, CV_W), row(b_dw[l]),
                      row(g_cv_ln[l]), row(b_cv_ln[l]), w_pw2[l].astype(BF16), row(b_pw2[l]))
    o_mx = _memx(mem, row(g_mem[l]), w_mem_kv[l].astype(BF16), proj3)

    h, hn3, idx_pad, rank_pad, gate_pad, counts = _outproj(
        o_sb.reshape(T, SB_W), o_cv.reshape(T, CV_W), o_mx.reshape(T, MX_W), x2,
        row(g_sb_out[l]), row(g_cv_out[l]), row(g_mx_out[l]), w_out[l].astype(BF16),
        row(g_ffn_norm[l]), w_router[l], row(b_router[l]))

    bexp, n_used, dest, fill_start, fill_rows, n_rows = _routing_tables(
        counts[0].astype(jnp.int32), idx_pad[:, :TOP_K], rank_pad[:, :TOP_K])
    xs3 = _dispatch(fill_start, fill_rows, dest, hn3, n_rows)
    ys3 = _experts(bexp, n_used, xs3, w_gu[l], b_gu[l].reshape(N_EXPERTS, 1, 2 * D_FF),
                   w_down[l], b_down[l].reshape(N_EXPERTS, 1, D_MODEL))
    out = _combine(h, ys3, dest, gate_pad, row(g_final))
    return out.reshape(B, S, D)
```

```python
import functools

import jax
import jax.numpy as jnp
from jax import lax
from jax.experimental import pallas as pl
from jax.experimental.pallas import tpu as pltpu

F32 = jnp.float32
BF16 = jnp.bfloat16

D_MODEL = 1024
HEAD_DIM = 64
SB_W = 512
CV_W = 256
MX_W = 256
IN_W = 3 * SB_W + 2 * CV_W + MX_W
CONV_K = 31
N_MEM = 256
N_EXPERTS = 32
TOP_K = 4
D_FF = 1024
SWIGLU_ALPHA = 1.702
SWIGLU_LIMIT = 7.0
EPS = 1e-6

LANES = 128
SUBLANES = 8
CHUNKS = D_MODEL // LANES
ROW_TILE = 512
SB_TILE = 256
SB_EXP_ZERO_BELOW = -104.0
CONV_PAD = 32
CONV_ROWS = 128
MX_ROWS = 512
MOE_BM = 512
FF_CHUNK = 512
COMBINE_TILE = 256
DMA_UNROLL = 16
DMA_PRIORITIES = 2
VMEM_LIMIT = 56 * 1024 * 1024


def _rms(x, g):
    return x * lax.rsqrt(jnp.mean(x * x, axis=-1, keepdims=True) + EPS) * g


def _inproj_kernel(x_ref, g_ref, w_ref, o_ref):
    xn = _rms(x_ref[...], g_ref[...]).astype(BF16)
    o_ref[...] = jnp.dot(xn, w_ref[...], preferred_element_type=F32).astype(BF16)


def _inproj(x2, g, w_bf):
    T = x2.shape[0]
    return pl.pallas_call(
        _inproj_kernel,
        grid=(T // ROW_TILE,),
        in_specs=[
            pl.BlockSpec((ROW_TILE, D_MODEL), lambda i: (i, 0)),
            pl.BlockSpec((1, D_MODEL), lambda i: (0, 0)),
            pl.BlockSpec((D_MODEL, IN_W), lambda i: (0, 0)),
        ],
        out_specs=pl.BlockSpec((ROW_TILE, IN_W), lambda i: (i, 0)),
        out_shape=jax.ShapeDtypeStruct((T, IN_W), BF16),
        compiler_params=pltpu.CompilerParams(
            dimension_semantics=("arbitrary",), vmem_limit_bytes=VMEM_LIMIT),
        name="inproj",
    )(x2, g, w_bf)


def _sb_kernel(q_ref, k_ref, v_ref, o_ref, *, seq):
    lane = lax.broadcasted_iota(jnp.int32, (SB_TILE, LANES), 1)
    head0 = lane < HEAD_DIM
    row = lax.broadcasted_iota(jnp.int32, (SB_TILE, SB_TILE), 0)
    col = lax.broadcasted_iota(jnp.int32, (SB_TILE, SB_TILE), 1)
    tri = jnp.where(row > col, 1.0, 0.0).astype(BF16)
    dmask = col < row

    def tile(qh, s0, c, diag):
        kb = k_ref[pl.ds(s0, SB_TILE), :]
        z = lax.dot_general(qh, kb, (((1,), (1,)), ((), ())), preferred_element_type=F32)
        lb = jnp.minimum(z, 0.0) - jnp.log(1.0 + jnp.exp(-jnp.abs(z)))
        l1m = lb - z
        if diag:
            l1m = jnp.where(dmask, l1m, 0.0)
        after = jnp.dot(l1m.astype(BF16), tri, preferred_element_type=F32)
        rowsum = jnp.broadcast_to(jnp.sum(l1m, axis=1, keepdims=True), (SB_TILE, LANES))
        arg = lb + after
        if c is not None:
            arg = arg + jnp.concatenate([c] * (SB_TILE // LANES), axis=1)
        a = jnp.exp(arg)
        if diag:
            a = jnp.where(dmask, a, 0.0)
        pv = jnp.dot(a.astype(BF16), v_ref[pl.ds(s0, SB_TILE), :], preferred_element_type=F32)
        return pv, rowsum

    def alive(c0, c1):
        return jnp.max(jnp.maximum(c0, c1)) >= SB_EXP_ZERO_BELOW

    def heads(t0):
        q = q_ref[pl.ds(t0, SB_TILE), :] * (HEAD_DIM ** -0.5)
        return jnp.where(head0, q, jnp.zeros_like(q)), jnp.where(head0, jnp.zeros_like(q), q)

    def store(t0, acc0, acc1):
        o_ref[pl.ds(t0, SB_TILE), :] = jnp.where(head0, acc0, acc1).astype(BF16)

    q0, q1 = heads(0)
    store(0, tile(q0, 0, None, True)[0], tile(q1, 0, None, True)[0])

    def qtile(i, _):
        t0 = pl.multiple_of(i * SB_TILE, SB_TILE)
        s0 = pl.multiple_of((i - 1) * SB_TILE, SB_TILE)
        q0, q1 = heads(t0)
        acc0, c0 = tile(q0, t0, None, True)
        acc1, c1 = tile(q1, t0, None, True)
        pv0, r0 = tile(q0, s0, c0, False)
        pv1, r1 = tile(q1, s0, c1, False)
        acc0, acc1, c0, c1 = acc0 + pv0, acc1 + pv1, c0 + r0, c1 + r1

        def cond(st):
            return (st[0] >= 0) & st[5]

        def body(st):
            j, acc0, acc1, c0, c1, _ = st
            s0 = pl.multiple_of(j * SB_TILE, SB_TILE)
            pv0, r0 = tile(q0, s0, c0, False)
            pv1, r1 = tile(q1, s0, c1, False)
            c0 = c0 + r0
            c1 = c1 + r1
            return j - 1, acc0 + pv0, acc1 + pv1, c0, c1, alive(c0, c1)

        st = lax.while_loop(cond, body, (i - 2, acc0, acc1, c0, c1, alive(c0, c1)))
        store(t0, st[1], st[2])
        return 0

    lax.fori_loop(1, seq // SB_TILE, qtile, 0)


def _sb_attention(proj3):
    B, S, _ = proj3.shape
    pairs = SB_W // LANES
    return pl.pallas_call(
        functools.partial(_sb_kernel, seq=S),
        grid=(B, pairs),
        in_specs=[
            pl.BlockSpec((None, S, LANES), lambda b, p: (b, 0, p)),
            pl.BlockSpec((None, S, LANES), lambda b, p: (b, 0, pairs + p)),
            pl.BlockSpec((None, S, LANES), lambda b, p: (b, 0, 2 * pairs + p)),
        ],
        out_specs=pl.BlockSpec((None, S, LANES), lambda b, p: (b, 0, p)),
        out_shape=jax.ShapeDtypeStruct((B, S, SB_W), BF16),
        compiler_params=pltpu.CompilerParams(
            dimension_semantics=("arbitrary", "arbitrary"), vmem_limit_bytes=VMEM_LIMIT),
        name="sb_attention",
    )(proj3, proj3, proj3)


def _conv_kernel(glu_ref, bglu_ref, wdw_ref, bdw_ref, gln_ref, bln_ref, wpw_ref, bpw_ref,
                 o_ref, upad_ref, shift_ref, *, seq):
    upad_ref[0:CONV_PAD, :] = jnp.zeros((CONV_PAD, CV_W), F32)
    for c in range(seq // CONV_ROWS):
        r0 = c * CONV_ROWS
        g = glu_ref[r0:r0 + CONV_ROWS, :].astype(F32) + bglu_ref[...]
        upad_ref[CONV_PAD + r0:CONV_PAD + r0 + CONV_ROWS, :] = (
            g[:, :CV_W] * jax.nn.sigmoid(g[:, CV_W:]))
    for c in range(seq // CONV_ROWS):
        r0 = c * CONV_ROWS
        acc = jnp.zeros((CONV_ROWS, CV_W), F32) + bdw_ref[...]
        span = CONV_ROWS + CONV_PAD - SUBLANES
        for r in range(1, SUBLANES):
            shift_ref[r, 0:span, :] = upad_ref[r0 + r:r0 + r + span, :]
        for k in range(CONV_K):
            off = CONV_PAD - (CONV_K - 1) + k
            r = off % SUBLANES
            a = off - r
            if r == 0:
                tap = upad_ref[r0 + a:r0 + a + CONV_ROWS, :]
            else:
                tap = shift_ref[r, a:a + CONV_ROWS, :]
            acc = acc + tap * wdw_ref[k:k + 1, :]
        mu = jnp.mean(acc, axis=-1, keepdims=True)
        d = acc - mu
        var = jnp.mean(d * d, axis=-1, keepdims=True)
        y = d * lax.rsqrt(var + EPS) * gln_ref[...] + bln_ref[...]
        y = y * jax.nn.sigmoid(y)
        out = jnp.dot(y.astype(BF16), wpw_ref[...], preferred_element_type=F32) + bpw_ref[...]
        o_ref[r0:r0 + CONV_ROWS, :] = out.astype(BF16)


def _conformer(proj3, b_glu, w_dw, b_dw, g_ln, b_ln, w_pw_bf, b_pw):
    B, S, _ = proj3.shape
    glu_block = (3 * SB_W) // (2 * CV_W)
    vec = lambda n: pl.BlockSpec((1, n), lambda b: (0, 0))
    return pl.pallas_call(
        functools.partial(_conv_kernel, seq=S),
        grid=(B,),
        in_specs=[
            pl.BlockSpec((None, S, 2 * CV_W), lambda b: (b, 0, glu_block)),
            vec(2 * CV_W),
            pl.BlockSpec((CONV_K, CV_W), lambda b: (0, 0)),
            vec(CV_W), vec(CV_W), vec(CV_W),
            pl.BlockSpec((CV_W, CV_W), lambda b: (0, 0)),
            vec(CV_W),
        ],
        out_specs=pl.BlockSpec((None, S, CV_W), lambda b: (b, 0, 0)),
        out_shape=jax.ShapeDtypeStruct((B, S, CV_W), BF16),
        scratch_shapes=[pltpu.VMEM((CONV_PAD + S, CV_W), F32),
                        pltpu.VMEM((SUBLANES, CONV_ROWS + CONV_PAD, CV_W), F32)],
        compiler_params=pltpu.CompilerParams(
            dimension_semantics=("arbitrary",), vmem_limit_bytes=VMEM_LIMIT),
        name="conformer",
    )(proj3, b_glu, w_dw, b_dw, g_ln, b_ln, w_pw_bf, b_pw)


def _memx_kernel(mem_ref, gm_ref, wkv_ref, q_ref, o_ref, *, seq):
    scale = HEAD_DIM ** -0.5
    mn = _rms(mem_ref[...], gm_ref[...]).astype(BF16)
    kv = jnp.dot(mn, wkv_ref[...], preferred_element_type=F32)
    km = kv[:, :MX_W].astype(BF16)
    vm = kv[:, MX_W:].astype(BF16)
    lane = lax.broadcasted_iota(jnp.int32, (MX_ROWS, MX_W), 1)

    def chunk(c, _):
        r0 = pl.multiple_of(c * MX_ROWS, MX_ROWS)
        q = q_ref[pl.ds(r0, MX_ROWS), :]
        out = jnp.zeros((MX_ROWS, MX_W), F32)
        for h in range(MX_W // HEAD_DIM):
            head = (lane >= HEAD_DIM * h) & (lane < HEAD_DIM * (h + 1))
            qh = jnp.where(head, q, jnp.zeros_like(q))
            s = lax.dot_general(qh, km, (((1,), (1,)), ((), ())),
                                preferred_element_type=F32) * scale
            p = jnp.exp(s - jnp.max(s, axis=-1, keepdims=True))
            p = p / jnp.sum(p, axis=-1, keepdims=True)
            oh = jnp.dot(p.astype(BF16), vm, preferred_element_type=F32)
            out = jnp.where(head, oh, out)
        o_ref[pl.ds(r0, MX_ROWS), :] = out.astype(BF16)
        return 0

    lax.fori_loop(0, seq // MX_ROWS, chunk, 0)


def _memx(mem, g_mem, w_kv_bf, proj3):
    B, S, _ = proj3.shape
    q_block = (3 * SB_W + 2 * CV_W) // MX_W
    return pl.pallas_call(
        functools.partial(_memx_kernel, seq=S),
        grid=(B,),
        in_specs=[
            pl.BlockSpec((None, N_MEM, D_MODEL), lambda b: (b, 0, 0)),
            pl.BlockSpec((1, D_MODEL), lambda b: (0, 0)),
            pl.BlockSpec((D_MODEL, 2 * MX_W), lambda b: (0, 0)),
            pl.BlockSpec((None, S, MX_W), lambda b: (b, 0, q_block)),
        ],
        out_specs=pl.BlockSpec((None, S, MX_W), lambda b: (b, 0, 0)),
        out_shape=jax.ShapeDtypeStruct((B, S, MX_W), BF16),
        compiler_params=pltpu.CompilerParams(
            dimension_semantics=("arbitrary",), vmem_limit_bytes=VMEM_LIMIT),
        name="memx",
    )(mem, g_mem, w_kv_bf, proj3)


def _outproj_kernel(sb_ref, cv_ref, mx_ref, x_ref, gsb_ref, gcv_ref, gmx_ref, wo_ref,
                    gffn_ref, wr_ref, br_ref, h_ref, hn3_ref, idx_ref, rank_ref, gate_ref,
                    cnt_ref, run_ref):
    def normed(o_ref, g_ref):
        return _rms(o_ref[...].astype(F32), g_ref[...]).astype(BF16)

    mix = jnp.dot(normed(sb_ref, gsb_ref), wo_ref[0:SB_W, :], preferred_element_type=F32)
    mix += jnp.dot(normed(cv_ref, gcv_ref), wo_ref[SB_W:SB_W + CV_W, :],
                   preferred_element_type=F32)
    mix += jnp.dot(normed(mx_ref, gmx_ref), wo_ref[SB_W + CV_W:, :],
                   preferred_element_type=F32)
    h = x_ref[...] + mix
    h_ref[...] = h
    hn = _rms(h, gffn_ref[...])
    for s in range(CHUNKS):
        hn3_ref[pl.ds(s, ROW_TILE, stride=SUBLANES), :] = hn[:, s * LANES:(s + 1) * LANES]
    wr = wr_ref[...]
    hn_hi = hn.astype(BF16)
    hn_lo = (hn - hn_hi.astype(F32)).astype(BF16)
    wr_hi = wr.astype(BF16)
    wr_lo = (wr - wr_hi.astype(F32)).astype(BF16)
    logits = (jnp.dot(hn_hi, wr_hi, preferred_element_type=F32)
              + jnp.dot(hn_lo, wr_hi, preferred_element_type=F32)
              + jnp.dot(hn_hi, wr_lo, preferred_element_type=F32)) + br_ref[...]
    eid = lax.broadcasted_iota(jnp.int32, logits.shape, 1)
    vals, idxs = [], []
    for _ in range(TOP_K):
        m = jnp.max(logits, axis=-1, keepdims=True)
        i = jnp.min(jnp.where(logits == m, eid, N_EXPERTS), axis=-1, keepdims=True)
        vals.append(m)
        idxs.append(i)
        logits = jnp.where(eid == i, -jnp.inf, logits)
    es = [jnp.exp(v - vals[0]) for v in vals]
    denom = es[0] + es[1] + es[2] + es[3]

    @pl.when(pl.program_id(0) == 0)
    def _():
        run_ref[...] = jnp.zeros_like(run_ref)
    tm = logits.shape[0]
    row = lax.broadcasted_iota(jnp.int32, (tm, tm), 0)
    col = lax.broadcasted_iota(jnp.int32, (tm, tm), 1)
    before = jnp.where(col < row, 1.0, 0.0).astype(BF16)
    base = run_ref[...]
    ranks = []
    for k in range(TOP_K):
        onehot = jnp.where(eid == idxs[k], 1.0, 0.0)
        prefix = jnp.dot(before, onehot.astype(BF16), preferred_element_type=F32)
        ranks.append(jnp.sum(onehot * (prefix + base), axis=-1, keepdims=True))
        base = base + jnp.sum(onehot, axis=0, keepdims=True)
    run_ref[...] = base
    cnt_ref[...] = jnp.broadcast_to(base, cnt_ref.shape)

    lane = lax.broadcasted_iota(jnp.int32, idx_ref.shape, 1)
    idx_out = jnp.zeros(idx_ref.shape, jnp.int32)
    rank_out = jnp.zeros(rank_ref.shape, jnp.int32)
    gate_out = jnp.zeros(gate_ref.shape, F32)
    for k in range(TOP_K):
        idx_out = jnp.where(lane == k, idxs[k], idx_out)
        rank_out = jnp.where(lane == k, ranks[k].astype(jnp.int32), rank_out)
        gate_out = jnp.where(lane == k, es[k] / denom, gate_out)
    idx_ref[...] = idx_out
    rank_ref[...] = rank_out
    gate_ref[...] = gate_out


def _outproj(o_sb, o_cv, o_mx, x2, g_sb, g_cv, g_mx, w_out_bf, g_ffn, w_router, b_router):
    T = x2.shape[0]
    rows = lambda n: pl.BlockSpec((ROW_TILE, n), lambda i: (i, 0))
    full = lambda a, b: pl.BlockSpec((a, b), lambda i: (0, 0))
    return pl.pallas_call(
        _outproj_kernel,
        grid=(T // ROW_TILE,),
        in_specs=[
            rows(SB_W), rows(CV_W), rows(MX_W), rows(D_MODEL),
            full(1, SB_W), full(1, CV_W), full(1, MX_W),
            full(D_MODEL, D_MODEL), full(1, D_MODEL),
            full(D_MODEL, N_EXPERTS), full(1, N_EXPERTS),
        ],
        out_specs=[rows(D_MODEL),
                   pl.BlockSpec((ROW_TILE * SUBLANES, LANES), lambda i: (i, 0)),
                   rows(LANES), rows(LANES), rows(LANES),
                   full(SUBLANES, N_EXPERTS)],
        out_shape=[
            jax.ShapeDtypeStruct((T, D_MODEL), F32),
            jax.ShapeDtypeStruct((T * SUBLANES, LANES), F32),
            jax.ShapeDtypeStruct((T, LANES), jnp.int32),
            jax.ShapeDtypeStruct((T, LANES), jnp.int32),
            jax.ShapeDtypeStruct((T, LANES), F32),
            jax.ShapeDtypeStruct((SUBLANES, N_EXPERTS), F32),
        ],
        scratch_shapes=[pltpu.VMEM((1, N_EXPERTS), F32)],
        compiler_params=pltpu.CompilerParams(
            dimension_semantics=("arbitrary",), vmem_limit_bytes=VMEM_LIMIT),
        name="outproj_router",
    )(o_sb, o_cv, o_mx, x2, g_sb, g_cv, g_mx, w_out_bf, g_ffn, w_router, b_router)


def _rows_copy(src_ref, src_row, dst_ref, dst_row, n, sem):
    src = src_ref.at[pl.ds(pl.multiple_of(src_row * SUBLANES, SUBLANES), n * SUBLANES), :]
    dst = dst_ref.at[pl.ds(pl.multiple_of(dst_row * SUBLANES, SUBLANES), n * SUBLANES), :]
    return pltpu.make_async_copy(src, dst, sem)


def _chunk_rows(ref, row, n):
    return ref.at[:, pl.ds(row, n), :]


def _dispatch_kernel(fill_start_ref, fill_rows_ref, dest_ref, hn3_ref, xs3_ref,
                     tile_ref, zero_ref, sem_ref, load_sem_ref, fill_sem_ref, *, tm):
    i = pl.program_id(0)
    n = pl.num_programs(0)
    slot = lax.rem(i, 3)
    slot_next = lax.rem(i + 1, 3)
    n_copies = tm * TOP_K
    pieces = [1 << p for p in range(MOE_BM.bit_length())]

    def fill(wait):
        for e in range(2 * N_EXPERTS):
            rows = fill_rows_ref[e]
            start = fill_start_ref[e]
            for p in pieces:
                @pl.when((rows & p) != 0)
                def _(p=p, start=start, rows=rows):
                    cp = _rows_copy(zero_ref, 0, xs3_ref, start + (rows & (p - 1)), p,
                                    fill_sem_ref.at[0])
                    if wait:
                        cp.wait()
                    else:
                        cp.start()

    @pl.when(i == 0)
    def _():
        zero_ref[...] = jnp.zeros_like(zero_ref)
        fill(wait=False)

    def tile_load(j, s):
        return _rows_copy(hn3_ref, j * tm, tile_ref.at[s], 0, tm, load_sem_ref.at[s])

    def rows_wait(s):
        for _ in range(TOP_K):
            _rows_copy(tile_ref.at[s], 0, xs3_ref, 0, tm, sem_ref.at[s]).wait()

    @pl.when(i == 0)
    def _():
        tile_load(0, 0).start()

    @pl.when(i >= 2)
    def _():
        rows_wait(slot_next)

    @pl.when(i + 1 < n)
    def _():
        tile_load(i + 1, slot_next).start()

    tile_load(i, slot).wait()

    def group(g, _):
        m0 = pl.multiple_of(g * DMA_UNROLL, DMA_UNROLL)
        dests = [dest_ref[0, 0, m0 + u] for u in range(DMA_UNROLL)]
        for u in range(DMA_UNROLL):
            r = g * (DMA_UNROLL // TOP_K) + u // TOP_K
            _rows_copy(tile_ref.at[slot], r, xs3_ref, dests[u], 1,
                       sem_ref.at[slot]).start(priority=u % DMA_PRIORITIES)
        return 0
    lax.fori_loop(0, n_copies // DMA_UNROLL, group, 0)

    @pl.when(i == 0)
    def _():
        fill(wait=True)

    @pl.when(i == n - 1)
    def _():
        @pl.when(i >= 1)
        def _():
            rows_wait(lax.rem(i + 2, 3))
        rows_wait(slot)


def _dispatch(fill_start, fill_rows, dest, hn3, n_blocks):
    T = hn3.shape[0] // SUBLANES
    tm = ROW_TILE
    grid_spec = pltpu.PrefetchScalarGridSpec(
        num_scalar_prefetch=2,
        grid=(T // tm,),
        in_specs=[
            pl.BlockSpec((1, 1, tm * TOP_K), lambda i, fs, fr: (i, 0, 0),
                         memory_space=pltpu.SMEM),
            pl.BlockSpec(memory_space=pl.ANY),
        ],
        out_specs=pl.BlockSpec(memory_space=pl.ANY),
        scratch_shapes=[
            pltpu.VMEM((3, tm * SUBLANES, LANES), F32),
            pltpu.VMEM((MOE_BM * SUBLANES, LANES), F32),
            pltpu.SemaphoreType.DMA((3,)),
            pltpu.SemaphoreType.DMA((3,)),
            pltpu.SemaphoreType.DMA((1,)),
        ],
    )
    return pl.pallas_call(
        functools.partial(_dispatch_kernel, tm=tm),
        grid_spec=grid_spec,
        out_shape=jax.ShapeDtypeStruct((n_blocks * MOE_BM * SUBLANES, LANES), F32),
        compiler_params=pltpu.CompilerParams(
            dimension_semantics=("arbitrary",), vmem_limit_bytes=VMEM_LIMIT),
        name="dispatch",
    )(fill_start, fill_rows, dest.reshape(T // tm, 1, tm * TOP_K), hn3)


def _experts_kernel(bexp_ref, nused_ref, x_ref, wgu_ref, bgu_ref, wd_ref, bd_ref, y_ref,
                    hm_ref, wgu_bf_ref, wd_bf_ref):
    b = pl.program_id(0)

    @pl.when(b >= nused_ref[0])
    def _():
        y_ref[...] = jnp.zeros_like(y_ref)

    @pl.when(b < nused_ref[0])
    def _():
        prev = jnp.maximum(b - 1, 0)
        @pl.when((b == 0) | (bexp_ref[b] != bexp_ref[prev]))
        def _():
            wgu_bf_ref[...] = wgu_ref[...].astype(BF16)
            wd_bf_ref[...] = wd_ref[...].astype(BF16)

        x = jnp.concatenate(
            [x_ref[pl.ds(s, MOE_BM, stride=SUBLANES), :].astype(BF16) for s in range(CHUNKS)],
            axis=1)

        for c0 in range(0, D_FF, FF_CHUNK):
            g_cols = slice(c0, c0 + FF_CHUNK)
            u_cols = slice(D_FF + c0, D_FF + c0 + FF_CHUNK)
            gate = jnp.dot(x, wgu_bf_ref[:, g_cols], preferred_element_type=F32) + bgu_ref[:, g_cols]
            up = jnp.dot(x, wgu_bf_ref[:, u_cols], preferred_element_type=F32) + bgu_ref[:, u_cols]
            gate = jnp.minimum(gate, SWIGLU_LIMIT)
            up = jnp.clip(up, -SWIGLU_LIMIT, SWIGLU_LIMIT)
            hm_ref[:, g_cols] = ((up + 1.0) * (gate * jax.nn.sigmoid(SWIGLU_ALPHA * gate))
                                 ).astype(BF16)
        y = jnp.dot(hm_ref[...], wd_bf_ref[...], preferred_element_type=F32) + bd_ref[...]
        for s in range(CHUNKS):
            y_ref[s] = y[:, s * LANES:(s + 1) * LANES]


def _experts(block_exp, n_used, xs3, w_gu, b_gu, w_down, b_down):
    NB = block_exp.shape[0]
    used = lambda b, nu: jnp.minimum(b, nu[0] - 1)
    grid_spec = pltpu.PrefetchScalarGridSpec(
        num_scalar_prefetch=2,
        grid=(NB,),
        in_specs=[
            pl.BlockSpec((MOE_BM * SUBLANES, LANES), lambda b, be, nu: (used(b, nu), 0)),
            pl.BlockSpec((None, D_MODEL, 2 * D_FF), lambda b, be, nu: (be[b], 0, 0)),
            pl.BlockSpec((None, 1, 2 * D_FF), lambda b, be, nu: (be[b], 0, 0)),
            pl.BlockSpec((None, D_FF, D_MODEL), lambda b, be, nu: (be[b], 0, 0)),
            pl.BlockSpec((None, 1, D_MODEL), lambda b, be, nu: (be[b], 0, 0)),
        ],
        out_specs=pl.BlockSpec((CHUNKS, MOE_BM, LANES), lambda b, be, nu: (0, b, 0)),
        scratch_shapes=[
            pltpu.VMEM((MOE_BM, D_FF), BF16),
            pltpu.VMEM((D_MODEL, 2 * D_FF), BF16),
            pltpu.VMEM((D_FF, D_MODEL), BF16),
        ],
    )
    return pl.pallas_call(
        _experts_kernel,
        grid_spec=grid_spec,
        out_shape=jax.ShapeDtypeStruct((CHUNKS, NB * MOE_BM, LANES), F32),
        compiler_params=pltpu.CompilerParams(
            dimension_semantics=("arbitrary",), vmem_limit_bytes=VMEM_LIMIT),
        name="experts",
    )(block_exp, n_used, xs3, w_gu, b_gu, w_down, b_down)


def _routing_tables(counts, top_idx, rank):
    M = top_idx.shape[0] * TOP_K
    NB = -(-M // MOE_BM) + N_EXPERTS
    nblk_e = (counts + MOE_BM - 1) // MOE_BM
    blk_end = jnp.cumsum(nblk_e)
    row_start = (blk_end - nblk_e) * MOE_BM
    n_used = blk_end[-1]
    blk = jnp.minimum(jnp.arange(NB, dtype=jnp.int32), n_used - 1)
    bexp = jnp.sum((blk[:, None] >= blk_end[None, :]).astype(jnp.int32), axis=1)
    experts = jnp.arange(N_EXPERTS, dtype=jnp.int32)
    dest = rank + jnp.sum(jnp.where(top_idx[:, :, None] == experts, row_start, 0), axis=-1)
    tail = NB - N_EXPERTS + experts
    fill_start = jnp.concatenate([row_start + counts, tail * MOE_BM])
    fill_rows = jnp.concatenate([nblk_e * MOE_BM - counts,
                                 jnp.where(tail >= n_used, MOE_BM, 0)])
    return (bexp.astype(jnp.int32), n_used.reshape(1).astype(jnp.int32),
            dest.astype(jnp.int32), fill_start.astype(jnp.int32),
            fill_rows.astype(jnp.int32), NB)


def _combine_kernel(dest_ref, dest_next_ref, h_ref, gate_ref, g_ref, ys3_ref, o_ref,
                    buf_ref, sem_ref, *, tm):
    i = pl.program_id(0)
    n = pl.num_programs(0)
    slot = lax.rem(i, 2)

    def start_gather(idx_ref, s):
        def group(g, _):
            m0 = pl.multiple_of(g * DMA_UNROLL, DMA_UNROLL)
            srcs = [idx_ref[0, 0, m0 + u] for u in range(DMA_UNROLL)]
            for u in range(DMA_UNROLL):
                r = g * (DMA_UNROLL // TOP_K) + u // TOP_K
                pltpu.make_async_copy(
                    _chunk_rows(ys3_ref, srcs[u], 1),
                    _chunk_rows(buf_ref.at[s, u % TOP_K], r, 1),
                    sem_ref.at[s]).start(priority=u % DMA_PRIORITIES)
            return 0
        lax.fori_loop(0, tm * TOP_K // DMA_UNROLL, group, 0)

    @pl.when(i == 0)
    def _():
        start_gather(dest_ref, 0)

    @pl.when(i + 1 < n)
    def _():
        start_gather(dest_next_ref, 1 - slot)

    for k in range(TOP_K):
        pltpu.make_async_copy(_chunk_rows(ys3_ref, 0, tm), buf_ref.at[slot, k],
                              sem_ref.at[slot]).wait()

    gates = gate_ref[...]
    chunks = []
    ssq = jnp.zeros((tm, 1), F32)
    for s in range(CHUNKS):
        acc = h_ref[:, s * LANES:(s + 1) * LANES]
        for k in range(TOP_K):
            acc = acc + gates[:, k:k + 1] * buf_ref[slot, k, s]
        chunks.append(acc)
        ssq = ssq + jnp.sum(acc * acc, axis=-1, keepdims=True)
    inv = lax.rsqrt(ssq * (1.0 / D_MODEL) + EPS)
    for s in range(CHUNKS):
        o_ref[:, s * LANES:(s + 1) * LANES] = chunks[s] * inv * g_ref[:, s * LANES:(s + 1) * LANES]


def _combine(h, ys3, dest, gates, g_final):
    T = h.shape[0]
    tm = COMBINE_TILE
    n = T // tm
    table = dest.reshape(n, 1, tm * TOP_K)
    idx_spec = lambda f: pl.BlockSpec((1, 1, tm * TOP_K), lambda i: (f(i), 0, 0),
                                      memory_space=pltpu.SMEM)
    return pl.pallas_call(
        functools.partial(_combine_kernel, tm=tm),
        grid=(n,),
        in_specs=[
            idx_spec(lambda i: i), idx_spec(lambda i: jnp.minimum(i + 1, n - 1)),
            pl.BlockSpec((tm, D_MODEL), lambda i: (i, 0)),
            pl.BlockSpec((tm, LANES), lambda i: (i, 0)),
            pl.BlockSpec((1, D_MODEL), lambda i: (0, 0)),
            pl.BlockSpec(memory_space=pl.ANY),
        ],
        out_specs=pl.BlockSpec((tm, D_MODEL), lambda i: (i, 0)),
        out_shape=jax.ShapeDtypeStruct((T, D_MODEL), F32),
        scratch_shapes=[
            pltpu.VMEM((2, TOP_K, CHUNKS, tm, LANES), F32),
            pltpu.SemaphoreType.DMA((2,)),
        ],
        compiler_params=pltpu.CompilerParams(
            dimension_semantics=("arbitrary",), vmem_limit_bytes=VMEM_LIMIT),
        name="combine",
    )(table, table, h, gates, g_final, ys3)


def kernel(x, mem, g_attn_norm, w_in, b_glu, w_dw, b_dw, g_cv_ln, b_cv_ln, w_pw2, b_pw2, g_mem, w_mem_kv, g_sb_out, g_cv_out, g_mx_out, w_out, g_ffn_norm, w_router, b_router, w_gu, b_gu, w_down, b_down, g_final):
    B, S, D = x.shape
    T = B * S
    assert D == D_MODEL and S % ROW_TILE == 0 and g_attn_norm.shape[0] == 1
    l = 0
    row = lambda v: v.reshape(1, -1)
    x2 = x.reshape(T, D)

    proj = _inproj(x2, row(g_attn_norm[l]), w_in[l].astype(BF16))
    proj3 = proj.reshape(B, S, IN_W)
    o_sb = _sb_attention(proj3)
    o_cv = _conformer(proj3, row(b_glu[l]), w_dw[l].reshape(CONV_K, CV_W), row(b_dw[l]),
                      row(g_cv_ln[l]), row(b_cv_ln[l]), w_pw2[l].astype(BF16), row(b_pw2[l]))
    o_mx = _memx(mem, row(g_mem[l]), w_mem_kv[l].astype(BF16), proj3)

    h, hn3, idx_pad, rank_pad, gate_pad, counts = _outproj(
        o_sb.reshape(T, SB_W), o_cv.reshape(T, CV_W), o_mx.reshape(T, MX_W), x2,
        row(g_sb_out[l]), row(g_cv_out[l]), row(g_mx_out[l]), w_out[l].astype(BF16),
        row(g_ffn_norm[l]), w_router[l], row(b_router[l]))

    bexp, n_used, dest, fill_start, fill_rows, n_rows = _routing_tables(
        counts[0].astype(jnp.int32), idx_pad[:, :TOP_K], rank_pad[:, :TOP_K])
    xs3 = _dispatch(fill_start, fill_rows, dest, hn3, n_rows)
    ys3 = _experts(bexp, n_used, xs3, w_gu[l], b_gu[l].reshape(N_EXPERTS, 1, 2 * D_FF),
                   w_down[l], b_down[l].reshape(N_EXPERTS, 1, D_MODEL))
    out = _combine(h, ys3, dest, gate_pad, row(g_final))
    return out.reshape(B, S, D)
```

```python
import functools

import jax
import jax.numpy as jnp
from jax import lax
from jax.experimental import pallas as pl
from jax.experimental.pallas import tpu as pltpu

F32 = jnp.float32
BF16 = jnp.bfloat16

D_MODEL = 1024
HEAD_DIM = 64
SB_W = 512
CV_W = 256
MX_W = 256
IN_W = 3 * SB_W + 2 * CV_W + MX_W
CONV_K = 31
N_MEM = 256
N_EXPERTS = 32
TOP_K = 4
D_FF = 1024
SWIGLU_ALPHA = 1.702
SWIGLU_LIMIT = 7.0
EPS = 1e-6

LANES = 128
SUBLANES = 8
CHUNKS = D_MODEL // LANES
HALF = D_MODEL // 2
PCHUNKS = HALF // LANES
U32 = jnp.uint32
ROW_TILE = 512
SB_TILE = 256
SB_EXP_ZERO_BELOW = -104.0
CONV_PAD = 32
CONV_ROWS = 128
MX_ROWS = 512
MOE_BM = 512
FF_CHUNK = 512
COMBINE_TILE = 256
DMA_UNROLL = 16
DMA_PRIORITIES = 2
VMEM_LIMIT = 56 * 1024 * 1024


def _rms(x, g):
    return x * lax.rsqrt(jnp.mean(x * x, axis=-1, keepdims=True) + EPS) * g


def _pack_halves(v):
    bits = pltpu.bitcast(v.astype(BF16).astype(F32), U32)
    return lax.shift_right_logical(bits[:, :HALF], U32(16)) | bits[:, HALF:]


def _unpack_halves(w):
    low = pltpu.bitcast(lax.shift_left(w, U32(16)), F32)
    high = pltpu.bitcast(w & U32(0xFFFF0000), F32)
    return low, high


def _inproj_kernel(x_ref, g_ref, w_ref, o_ref):
    xn = _rms(x_ref[...], g_ref[...]).astype(BF16)
    o_ref[...] = jnp.dot(xn, w_ref[...], preferred_element_type=F32).astype(BF16)


def _inproj(x2, g, w_bf):
    T = x2.shape[0]
    return pl.pallas_call(
        _inproj_kernel,
        grid=(T // ROW_TILE,),
        in_specs=[
            pl.BlockSpec((ROW_TILE, D_MODEL), lambda i: (i, 0)),
            pl.BlockSpec((1, D_MODEL), lambda i: (0, 0)),
            pl.BlockSpec((D_MODEL, IN_W), lambda i: (0, 0)),
        ],
        out_specs=pl.BlockSpec((ROW_TILE, IN_W), lambda i: (i, 0)),
        out_shape=jax.ShapeDtypeStruct((T, IN_W), BF16),
        compiler_params=pltpu.CompilerParams(
            dimension_semantics=("arbitrary",), vmem_limit_bytes=VMEM_LIMIT),
        name="inproj",
    )(x2, g, w_bf)


def _sb_kernel(q_ref, k_ref, v_ref, o_ref, *, seq):
    lane = lax.broadcasted_iota(jnp.int32, (SB_TILE, LANES), 1)
    head0 = lane < HEAD_DIM
    row = lax.broadcasted_iota(jnp.int32, (SB_TILE, SB_TILE), 0)
    col = lax.broadcasted_iota(jnp.int32, (SB_TILE, SB_TILE), 1)
    tri = jnp.where(row > col, 1.0, 0.0).astype(BF16)
    dmask = col < row

    def tile(qh, s0, c, diag):
        kb = k_ref[pl.ds(s0, SB_TILE), :]
        z = lax.dot_general(qh, kb, (((1,), (1,)), ((), ())), preferred_element_type=F32)
        lb = jnp.minimum(z, 0.0) - jnp.log(1.0 + jnp.exp(-jnp.abs(z)))
        l1m = lb - z
        if diag:
            l1m = jnp.where(dmask, l1m, 0.0)
        after = jnp.dot(l1m.astype(BF16), tri, preferred_element_type=F32)
        rowsum = jnp.broadcast_to(jnp.sum(l1m, axis=1, keepdims=True), (SB_TILE, LANES))
        arg = lb + after
        if c is not None:
            arg = arg + jnp.concatenate([c] * (SB_TILE // LANES), axis=1)
        a = jnp.exp(arg)
        if diag:
            a = jnp.where(dmask, a, 0.0)
        pv = jnp.dot(a.astype(BF16), v_ref[pl.ds(s0, SB_TILE), :], preferred_element_type=F32)
        return pv, rowsum

    def alive(c0, c1):
        return jnp.max(jnp.maximum(c0, c1)) >= SB_EXP_ZERO_BELOW

    def heads(t0):
        q = q_ref[pl.ds(t0, SB_TILE), :] * (HEAD_DIM ** -0.5)
        return jnp.where(head0, q, jnp.zeros_like(q)), jnp.where(head0, jnp.zeros_like(q), q)

    def store(t0, acc0, acc1):
        o_ref[pl.ds(t0, SB_TILE), :] = jnp.where(head0, acc0, acc1).astype(BF16)

    q0, q1 = heads(0)
    store(0, tile(q0, 0, None, True)[0], tile(q1, 0, None, True)[0])

    def qtile(i, _):
        t0 = pl.multiple_of(i * SB_TILE, SB_TILE)
        s0 = pl.multiple_of((i - 1) * SB_TILE, SB_TILE)
        q0, q1 = heads(t0)
        acc0, c0 = tile(q0, t0, None, True)
        acc1, c1 = tile(q1, t0, None, True)
        pv0, r0 = tile(q0, s0, c0, False)
        pv1, r1 = tile(q1, s0, c1, False)
        acc0, acc1, c0, c1 = acc0 + pv0, acc1 + pv1, c0 + r0, c1 + r1

        def cond(st):
            return (st[0] >= 0) & st[5]

        def body(st):
            j, acc0, acc1, c0, c1, _ = st
            s0 = pl.multiple_of(j * SB_TILE, SB_TILE)
            pv0, r0 = tile(q0, s0, c0, False)
            pv1, r1 = tile(q1, s0, c1, False)
            c0 = c0 + r0
            c1 = c1 + r1
            return j - 1, acc0 + pv0, acc1 + pv1, c0, c1, alive(c0, c1)

        st = lax.while_loop(cond, body, (i - 2, acc0, acc1, c0, c1, alive(c0, c1)))
        store(t0, st[1], st[2])
        return 0

    lax.fori_loop(1, seq // SB_TILE, qtile, 0)


def _sb_attention(proj3):
    B, S, _ = proj3.shape
    pairs = SB_W // LANES
    return pl.pallas_call(
        functools.partial(_sb_kernel, seq=S),
        grid=(B, pairs),
        in_specs=[
            pl.BlockSpec((None, S, LANES), lambda b, p: (b, 0, p)),
            pl.BlockSpec((None, S, LANES), lambda b, p: (b, 0, pairs + p)),
            pl.BlockSpec((None, S, LANES), lambda b, p: (b, 0, 2 * pairs + p)),
        ],
        out_specs=pl.BlockSpec((None, S, LANES), lambda b, p: (b, 0, p)),
        out_shape=jax.ShapeDtypeStruct((B, S, SB_W), BF16),
        compiler_params=pltpu.CompilerParams(
            dimension_semantics=("arbitrary", "arbitrary"), vmem_limit_bytes=VMEM_LIMIT),
        name="sb_attention",
    )(proj3, proj3, proj3)


def _conv_kernel(glu_ref, bglu_ref, wdw_ref, bdw_ref, gln_ref, bln_ref, wpw_ref, bpw_ref,
                 o_ref, upad_ref, shift_ref, *, seq):
    upad_ref[0:CONV_PAD, :] = jnp.zeros((CONV_PAD, CV_W), F32)
    for c in range(seq // CONV_ROWS):
        r0 = c * CONV_ROWS
        g = glu_ref[r0:r0 + CONV_ROWS, :].astype(F32) + bglu_ref[...]
        upad_ref[CONV_PAD + r0:CONV_PAD + r0 + CONV_ROWS, :] = (
            g[:, :CV_W] * jax.nn.sigmoid(g[:, CV_W:]))
    for c in range(seq // CONV_ROWS):
        r0 = c * CONV_ROWS
        acc = jnp.zeros((CONV_ROWS, CV_W), F32) + bdw_ref[...]
        span = CONV_ROWS + CONV_PAD - SUBLANES
        for r in range(1, SUBLANES):
            shift_ref[r, 0:span, :] = upad_ref[r0 + r:r0 + r + span, :]
        for k in range(CONV_K):
            off = CONV_PAD - (CONV_K - 1) + k
            r = off % SUBLANES
            a = off - r
            if r == 0:
                tap = upad_ref[r0 + a:r0 + a + CONV_ROWS, :]
            else:
                tap = shift_ref[r, a:a + CONV_ROWS, :]
            acc = acc + tap * wdw_ref[k:k + 1, :]
        mu = jnp.mean(acc, axis=-1, keepdims=True)
        d = acc - mu
        var = jnp.mean(d * d, axis=-1, keepdims=True)
        y = d * lax.rsqrt(var + EPS) * gln_ref[...] + bln_ref[...]
        y = y * jax.nn.sigmoid(y)
        out = jnp.dot(y.astype(BF16), wpw_ref[...], preferred_element_type=F32) + bpw_ref[...]
        o_ref[r0:r0 + CONV_ROWS, :] = out.astype(BF16)


def _conformer(proj3, b_glu, w_dw, b_dw, g_ln, b_ln, w_pw_bf, b_pw):
    B, S, _ = proj3.shape
    glu_block = (3 * SB_W) // (2 * CV_W)
    vec = lambda n: pl.BlockSpec((1, n), lambda b: (0, 0))
    return pl.pallas_call(
        functools.partial(_conv_kernel, seq=S),
        grid=(B,),
        in_specs=[
            pl.BlockSpec((None, S, 2 * CV_W), lambda b: (b, 0, glu_block)),
            vec(2 * CV_W),
            pl.BlockSpec((CONV_K, CV_W), lambda b: (0, 0)),
            vec(CV_W), vec(CV_W), vec(CV_W),
            pl.BlockSpec((CV_W, CV_W), lambda b: (0, 0)),
            vec(CV_W),
        ],
        out_specs=pl.BlockSpec((None, S, CV_W), lambda b: (b, 0, 0)),
        out_shape=jax.ShapeDtypeStruct((B, S, CV_W), BF16),
        scratch_shapes=[pltpu.VMEM((CONV_PAD + S, CV_W), F32),
                        pltpu.VMEM((SUBLANES, CONV_ROWS + CONV_PAD, CV_W), F32)],
        compiler_params=pltpu.CompilerParams(
            dimension_semantics=("arbitrary",), vmem_limit_bytes=VMEM_LIMIT),
        name="conformer",
    )(proj3, b_glu, w_dw, b_dw, g_ln, b_ln, w_pw_bf, b_pw)


def _memx_kernel(mem_ref, gm_ref, wkv_ref, q_ref, o_ref, *, seq):
    scale = HEAD_DIM ** -0.5
    mn = _rms(mem_ref[...], gm_ref[...]).astype(BF16)
    kv = jnp.dot(mn, wkv_ref[...], preferred_element_type=F32)
    km = kv[:, :MX_W].astype(BF16)
    vm = kv[:, MX_W:].astype(BF16)
    lane = lax.broadcasted_iota(jnp.int32, (MX_ROWS, MX_W), 1)

    def chunk(c, _):
        r0 = pl.multiple_of(c * MX_ROWS, MX_ROWS)
        q = q_ref[pl.ds(r0, MX_ROWS), :]
        out = jnp.zeros((MX_ROWS, MX_W), F32)
        for h in range(MX_W // HEAD_DIM):
            head = (lane >= HEAD_DIM * h) & (lane < HEAD_DIM * (h + 1))
            qh = jnp.where(head, q, jnp.zeros_like(q))
            s = lax.dot_general(qh, km, (((1,), (1,)), ((), ())),
                                preferred_element_type=F32) * scale
            p = jnp.exp(s - jnp.max(s, axis=-1, keepdims=True))
            p = p / jnp.sum(p, axis=-1, keepdims=True)
            oh = jnp.dot(p.astype(BF16), vm, preferred_element_type=F32)
            out = jnp.where(head, oh, out)
        o_ref[pl.ds(r0, MX_ROWS), :] = out.astype(BF16)
        return 0

    lax.fori_loop(0, seq // MX_ROWS, chunk, 0)


def _memx(mem, g_mem, w_kv_bf, proj3):
    B, S, _ = proj3.shape
    q_block = (3 * SB_W + 2 * CV_W) // MX_W
    return pl.pallas_call(
        functools.partial(_memx_kernel, seq=S),
        grid=(B,),
        in_specs=[
            pl.BlockSpec((None, N_MEM, D_MODEL), lambda b: (b, 0, 0)),
            pl.BlockSpec((1, D_MODEL), lambda b: (0, 0)),
            pl.BlockSpec((D_MODEL, 2 * MX_W), lambda b: (0, 0)),
            pl.BlockSpec((None, S, MX_W), lambda b: (b, 0, q_block)),
        ],
        out_specs=pl.BlockSpec((None, S, MX_W), lambda b: (b, 0, 0)),
        out_shape=jax.ShapeDtypeStruct((B, S, MX_W), BF16),
        compiler_params=pltpu.CompilerParams(
            dimension_semantics=("arbitrary",), vmem_limit_bytes=VMEM_LIMIT),
        name="memx",
    )(mem, g_mem, w_kv_bf, proj3)


def _outproj_kernel(sb_ref, cv_ref, mx_ref, x_ref, gsb_ref, gcv_ref, gmx_ref, wo_ref,
                    gffn_ref, wr_ref, br_ref, h_ref, hn3_ref, idx_ref, rank_ref, gate_ref,
                    cnt_ref, run_ref):
    def normed(o_ref, g_ref):
        return _rms(o_ref[...].astype(F32), g_ref[...]).astype(BF16)

    mix = jnp.dot(normed(sb_ref, gsb_ref), wo_ref[0:SB_W, :], preferred_element_type=F32)
    mix += jnp.dot(normed(cv_ref, gcv_ref), wo_ref[SB_W:SB_W + CV_W, :],
                   preferred_element_type=F32)
    mix += jnp.dot(normed(mx_ref, gmx_ref), wo_ref[SB_W + CV_W:, :],
                   preferred_element_type=F32)
    h = x_ref[...] + mix
    h_ref[...] = h
    hn = _rms(h, gffn_ref[...])
    packed = _pack_halves(hn)
    for s in range(PCHUNKS):
        hn3_ref[pl.ds(s, ROW_TILE, stride=PCHUNKS), :] = packed[:, s * LANES:(s + 1) * LANES]
    wr = wr_ref[...]
    hn_hi = hn.astype(BF16)
    hn_lo = (hn - hn_hi.astype(F32)).astype(BF16)
    wr_hi = wr.astype(BF16)
    wr_lo = (wr - wr_hi.astype(F32)).astype(BF16)
    logits = (jnp.dot(hn_hi, wr_hi, preferred_element_type=F32)
              + jnp.dot(hn_lo, wr_hi, preferred_element_type=F32)
              + jnp.dot(hn_hi, wr_lo, preferred_element_type=F32)) + br_ref[...]
    eid = lax.broadcasted_iota(jnp.int32, logits.shape, 1)
    vals, idxs = [], []
    for _ in range(TOP_K):
        m = jnp.max(logits, axis=-1, keepdims=True)
        i = jnp.min(jnp.where(logits == m, eid, N_EXPERTS), axis=-1, keepdims=True)
        vals.append(m)
        idxs.append(i)
        logits = jnp.where(eid == i, -jnp.inf, logits)
    es = [jnp.exp(v - vals[0]) for v in vals]
    denom = es[0] + es[1] + es[2] + es[3]

    @pl.when(pl.program_id(0) == 0)
    def _():
        run_ref[...] = jnp.zeros_like(run_ref)
    tm = logits.shape[0]
    row = lax.broadcasted_iota(jnp.int32, (tm, tm), 0)
    col = lax.broadcasted_iota(jnp.int32, (tm, tm), 1)
    before = jnp.where(col < row, 1.0, 0.0).astype(BF16)
    base = run_ref[...]
    ranks = []
    for k in range(TOP_K):
        onehot = jnp.where(eid == idxs[k], 1.0, 0.0)
        prefix = jnp.dot(before, onehot.astype(BF16), preferred_element_type=F32)
        ranks.append(jnp.sum(onehot * (prefix + base), axis=-1, keepdims=True))
        base = base + jnp.sum(onehot, axis=0, keepdims=True)
    run_ref[...] = base
    cnt_ref[...] = jnp.broadcast_to(base, cnt_ref.shape)

    lane = lax.broadcasted_iota(jnp.int32, idx_ref.shape, 1)
    idx_out = jnp.zeros(idx_ref.shape, jnp.int32)
    rank_out = jnp.zeros(rank_ref.shape, jnp.int32)
    gate_out = jnp.zeros(gate_ref.shape, F32)
    for k in range(TOP_K):
        idx_out = jnp.where(lane == k, idxs[k], idx_out)
        rank_out = jnp.where(lane == k, ranks[k].astype(jnp.int32), rank_out)
        gate_out = jnp.where(lane == k, es[k] / denom, gate_out)
    idx_ref[...] = idx_out
    rank_ref[...] = rank_out
    gate_ref[...] = gate_out


def _outproj(o_sb, o_cv, o_mx, x2, g_sb, g_cv, g_mx, w_out_bf, g_ffn, w_router, b_router):
    T = x2.shape[0]
    rows = lambda n: pl.BlockSpec((ROW_TILE, n), lambda i: (i, 0))
    full = lambda a, b: pl.BlockSpec((a, b), lambda i: (0, 0))
    return pl.pallas_call(
        _outproj_kernel,
        grid=(T // ROW_TILE,),
        in_specs=[
            rows(SB_W), rows(CV_W), rows(MX_W), rows(D_MODEL),
            full(1, SB_W), full(1, CV_W), full(1, MX_W),
            full(D_MODEL, D_MODEL), full(1, D_MODEL),
            full(D_MODEL, N_EXPERTS), full(1, N_EXPERTS),
        ],
        out_specs=[rows(D_MODEL),
                   pl.BlockSpec((ROW_TILE * PCHUNKS, LANES), lambda i: (i, 0)),
                   rows(LANES), rows(LANES), rows(LANES),
                   full(SUBLANES, N_EXPERTS)],
        out_shape=[
            jax.ShapeDtypeStruct((T, D_MODEL), F32),
            jax.ShapeDtypeStruct((T * PCHUNKS, LANES), U32),
            jax.ShapeDtypeStruct((T, LANES), jnp.int32),
            jax.ShapeDtypeStruct((T, LANES), jnp.int32),
            jax.ShapeDtypeStruct((T, LANES), F32),
            jax.ShapeDtypeStruct((SUBLANES, N_EXPERTS), F32),
        ],
        scratch_shapes=[pltpu.VMEM((1, N_EXPERTS), F32)],
        compiler_params=pltpu.CompilerParams(
            dimension_semantics=("arbitrary",), vmem_limit_bytes=VMEM_LIMIT),
        name="outproj_router",
    )(o_sb, o_cv, o_mx, x2, g_sb, g_cv, g_mx, w_out_bf, g_ffn, w_router, b_router)


def _rows_copy(src_ref, src_row, dst_ref, dst_row, n, sem):
    src = src_ref.at[pl.ds(pl.multiple_of(src_row * PCHUNKS, PCHUNKS), n * PCHUNKS), :]
    dst = dst_ref.at[pl.ds(pl.multiple_of(dst_row * PCHUNKS, PCHUNKS), n * PCHUNKS), :]
    return pltpu.make_async_copy(src, dst, sem)


def _chunk_rows(ref, row, n):
    return ref.at[:, pl.ds(row, n), :]


def _dispatch_kernel(fill_start_ref, fill_rows_ref, dest_ref, hn3_ref, xs3_ref,
                     tile_ref, zero_ref, sem_ref, load_sem_ref, fill_sem_ref, *, tm):
    i = pl.program_id(0)
    n = pl.num_programs(0)
    slot = lax.rem(i, 3)
    slot_next = lax.rem(i + 1, 3)
    n_copies = tm * TOP_K
    pieces = [1 << p for p in range(MOE_BM.bit_length())]

    def fill(wait):
        for e in range(2 * N_EXPERTS):
            rows = fill_rows_ref[e]
            start = fill_start_ref[e]
            for p in pieces:
                @pl.when((rows & p) != 0)
                def _(p=p, start=start, rows=rows):
                    cp = _rows_copy(zero_ref, 0, xs3_ref, start + (rows & (p - 1)), p,
                                    fill_sem_ref.at[0])
                    if wait:
                        cp.wait()
                    else:
                        cp.start()

    @pl.when(i == 0)
    def _():
        zero_ref[...] = jnp.zeros_like(zero_ref)
        fill(wait=False)

    def tile_load(j, s):
        return _rows_copy(hn3_ref, j * tm, tile_ref.at[s], 0, tm, load_sem_ref.at[s])

    def rows_wait(s):
        for _ in range(TOP_K):
            _rows_copy(tile_ref.at[s], 0, xs3_ref, 0, tm, sem_ref.at[s]).wait()

    @pl.when(i == 0)
    def _():
        tile_load(0, 0).start()

    @pl.when(i >= 2)
    def _():
        rows_wait(slot_next)

    @pl.when(i + 1 < n)
    def _():
        tile_load(i + 1, slot_next).start()

    tile_load(i, slot).wait()

    def group(g, _):
        m0 = pl.multiple_of(g * DMA_UNROLL, DMA_UNROLL)
        dests = [dest_ref[0, 0, m0 + u] for u in range(DMA_UNROLL)]
        for u in range(DMA_UNROLL):
            r = g * (DMA_UNROLL // TOP_K) + u // TOP_K
            _rows_copy(tile_ref.at[slot], r, xs3_ref, dests[u], 1,
                       sem_ref.at[slot]).start(priority=u % DMA_PRIORITIES)
        return 0
    lax.fori_loop(0, n_copies // DMA_UNROLL, group, 0)

    @pl.when(i == 0)
    def _():
        fill(wait=True)

    @pl.when(i == n - 1)
    def _():
        @pl.when(i >= 1)
        def _():
            rows_wait(lax.rem(i + 2, 3))
        rows_wait(slot)


def _dispatch(fill_start, fill_rows, dest, hn3, n_blocks):
    T = hn3.shape[0] // PCHUNKS
    tm = ROW_TILE
    grid_spec = pltpu.PrefetchScalarGridSpec(
        num_scalar_prefetch=2,
        grid=(T // tm,),
        in_specs=[
            pl.BlockSpec((1, 1, tm * TOP_K), lambda i, fs, fr: (i, 0, 0),
                         memory_space=pltpu.SMEM),
            pl.BlockSpec(memory_space=pl.ANY),
        ],
        out_specs=pl.BlockSpec(memory_space=pl.ANY),
        scratch_shapes=[
            pltpu.VMEM((3, tm * PCHUNKS, LANES), U32),
            pltpu.VMEM((MOE_BM * PCHUNKS, LANES), U32),
            pltpu.SemaphoreType.DMA((3,)),
            pltpu.SemaphoreType.DMA((3,)),
            pltpu.SemaphoreType.DMA((1,)),
        ],
    )
    return pl.pallas_call(
        functools.partial(_dispatch_kernel, tm=tm),
        grid_spec=grid_spec,
        out_shape=jax.ShapeDtypeStruct((n_blocks * MOE_BM * PCHUNKS, LANES), U32),
        compiler_params=pltpu.CompilerParams(
            dimension_semantics=("arbitrary",), vmem_limit_bytes=VMEM_LIMIT),
        name="dispatch",
    )(fill_start, fill_rows, dest.reshape(T // tm, 1, tm * TOP_K), hn3)


def _experts_kernel(bexp_ref, nused_ref, x_ref, wgu_ref, bgu_ref, wd_ref, bd_ref, y_ref,
                    hm_ref, wgu_bf_ref, wd_bf_ref):
    b = pl.program_id(0)

    @pl.when(b >= nused_ref[0])
    def _():
        y_ref[...] = jnp.zeros_like(y_ref)

    @pl.when(b < nused_ref[0])
    def _():
        prev = jnp.maximum(b - 1, 0)
        @pl.when((b == 0) | (bexp_ref[b] != bexp_ref[prev]))
        def _():
            wgu_bf_ref[...] = wgu_ref[...].astype(BF16)
            wd_bf_ref[...] = wd_ref[...].astype(BF16)

        halves = [_unpack_halves(x_ref[pl.ds(s, MOE_BM, stride=PCHUNKS), :])
                  for s in range(PCHUNKS)]
        x = jnp.concatenate([lo for lo, _ in halves] + [hi for _, hi in halves],
                            axis=1).astype(BF16)

        for c0 in range(0, D_FF, FF_CHUNK):
            g_cols = slice(c0, c0 + FF_CHUNK)
            u_cols = slice(D_FF + c0, D_FF + c0 + FF_CHUNK)
            gate = jnp.dot(x, wgu_bf_ref[:, g_cols], preferred_element_type=F32) + bgu_ref[:, g_cols]
            up = jnp.dot(x, wgu_bf_ref[:, u_cols], preferred_element_type=F32) + bgu_ref[:, u_cols]
            gate = jnp.minimum(gate, SWIGLU_LIMIT)
            up = jnp.clip(up, -SWIGLU_LIMIT, SWIGLU_LIMIT)
            hm_ref[:, g_cols] = ((up + 1.0) * (gate * jax.nn.sigmoid(SWIGLU_ALPHA * gate))
                                 ).astype(BF16)
        y = jnp.dot(hm_ref[...], wd_bf_ref[...], preferred_element_type=F32) + bd_ref[...]
        packed = _pack_halves(y)
        for s in range(PCHUNKS):
            y_ref[s] = packed[:, s * LANES:(s + 1) * LANES]


def _experts(block_exp, n_used, xs3, w_gu, b_gu, w_down, b_down):
    NB = block_exp.shape[0]
    used = lambda b, nu: jnp.minimum(b, nu[0] - 1)
    grid_spec = pltpu.PrefetchScalarGridSpec(
        num_scalar_prefetch=2,
        grid=(NB,),
        in_specs=[
            pl.BlockSpec((MOE_BM * PCHUNKS, LANES), lambda b, be, nu: (used(b, nu), 0)),
            pl.BlockSpec((None, D_MODEL, 2 * D_FF), lambda b, be, nu: (be[b], 0, 0)),
            pl.BlockSpec((None, 1, 2 * D_FF), lambda b, be, nu: (be[b], 0, 0)),
            pl.BlockSpec((None, D_FF, D_MODEL), lambda b, be, nu: (be[b], 0, 0)),
            pl.BlockSpec((None, 1, D_MODEL), lambda b, be, nu: (be[b], 0, 0)),
        ],
        out_specs=pl.BlockSpec((PCHUNKS, MOE_BM, LANES), lambda b, be, nu: (0, b, 0)),
        scratch_shapes=[
            pltpu.VMEM((MOE_BM, D_FF), BF16),
            pltpu.VMEM((D_MODEL, 2 * D_FF), BF16),
            pltpu.VMEM((D_FF, D_MODEL), BF16),
        ],
    )
    return pl.pallas_call(
        _experts_kernel,
        grid_spec=grid_spec,
        out_shape=jax.ShapeDtypeStruct((PCHUNKS, NB * MOE_BM, LANES), U32),
        compiler_params=pltpu.CompilerParams(
            dimension_semantics=("arbitrary",), vmem_limit_bytes=VMEM_LIMIT),
        name="experts",
    )(block_exp, n_used, xs3, w_gu, b_gu, w_down, b_down)


def _routing_tables(counts, top_idx, rank):
    M = top_idx.shape[0] * TOP_K
    NB = -(-M // MOE_BM) + N_EXPERTS
    nblk_e = (counts + MOE_BM - 1) // MOE_BM
    blk_end = jnp.cumsum(nblk_e)
    row_start = (blk_end - nblk_e) * MOE_BM
    n_used = blk_end[-1]
    blk = jnp.minimum(jnp.arange(NB, dtype=jnp.int32), n_used - 1)
    bexp = jnp.sum((blk[:, None] >= blk_end[None, :]).astype(jnp.int32), axis=1)
    experts = jnp.arange(N_EXPERTS, dtype=jnp.int32)
    dest = rank + jnp.sum(jnp.where(top_idx[:, :, None] == experts, row_start, 0), axis=-1)
    tail = NB - N_EXPERTS + experts
    fill_start = jnp.concatenate([row_start + counts, tail * MOE_BM])
    fill_rows = jnp.concatenate([nblk_e * MOE_BM - counts,
                                 jnp.where(tail >= n_used, MOE_BM, 0)])
    return (bexp.astype(jnp.int32), n_used.reshape(1).astype(jnp.int32),
            dest.astype(jnp.int32), fill_start.astype(jnp.int32),
            fill_rows.astype(jnp.int32), NB)


def _combine_kernel(dest_ref, dest_next_ref, h_ref, gate_ref, g_ref, ys3_ref, o_ref,
                    buf_ref, sem_ref, *, tm):
    i = pl.program_id(0)
    n = pl.num_programs(0)
    slot = lax.rem(i, 2)

    def start_gather(idx_ref, s):
        def group(g, _):
            m0 = pl.multiple_of(g * DMA_UNROLL, DMA_UNROLL)
            srcs = [idx_ref[0, 0, m0 + u] for u in range(DMA_UNROLL)]
            for u in range(DMA_UNROLL):
                r = g * (DMA_UNROLL // TOP_K) + u // TOP_K
                pltpu.make_async_copy(
                    _chunk_rows(ys3_ref, srcs[u], 1),
                    _chunk_rows(buf_ref.at[s, u % TOP_K], r, 1),
                    sem_ref.at[s]).start(priority=u % DMA_PRIORITIES)
            return 0
        lax.fori_loop(0, tm * TOP_K // DMA_UNROLL, group, 0)

    @pl.when(i == 0)
    def _():
        start_gather(dest_ref, 0)

    @pl.when(i + 1 < n)
    def _():
        start_gather(dest_next_ref, 1 - slot)

    for k in range(TOP_K):
        pltpu.make_async_copy(_chunk_rows(ys3_ref, 0, tm), buf_ref.at[slot, k],
                              sem_ref.at[slot]).wait()

    gates = gate_ref[...]
    chunks = [None] * CHUNKS
    ssq = jnp.zeros((tm, 1), F32)
    for s in range(PCHUNKS):
        acc_lo = h_ref[:, s * LANES:(s + 1) * LANES]
        acc_hi = h_ref[:, HALF + s * LANES:HALF + (s + 1) * LANES]
        for k in range(TOP_K):
            lo, hi = _unpack_halves(buf_ref[slot, k, s])
            acc_lo = acc_lo + gates[:, k:k + 1] * lo
            acc_hi = acc_hi + gates[:, k:k + 1] * hi
        chunks[s], chunks[PCHUNKS + s] = acc_lo, acc_hi
        ssq = ssq + jnp.sum(acc_lo * acc_lo + acc_hi * acc_hi, axis=-1, keepdims=True)
    inv = lax.rsqrt(ssq * (1.0 / D_MODEL) + EPS)
    for s in range(CHUNKS):
        o_ref[:, s * LANES:(s + 1) * LANES] = chunks[s] * inv * g_ref[:, s * LANES:(s + 1) * LANES]


def _combine(h, ys3, dest, gates, g_final):
    T = h.shape[0]
    tm = COMBINE_TILE
    n = T // tm
    table = dest.reshape(n, 1, tm * TOP_K)
    idx_spec = lambda f: pl.BlockSpec((1, 1, tm * TOP_K), lambda i: (f(i), 0, 0),
                                      memory_space=pltpu.SMEM)
    return pl.pallas_call(
        functools.partial(_combine_kernel, tm=tm),
        grid=(n,),
        in_specs=[
            idx_spec(lambda i: i), idx_spec(lambda i: jnp.minimum(i + 1, n - 1)),
            pl.BlockSpec((tm, D_MODEL), lambda i: (i, 0)),
            pl.BlockSpec((tm, LANES), lambda i: (i, 0)),
            pl.BlockSpec((1, D_MODEL), lambda i: (0, 0)),
            pl.BlockSpec(memory_space=pl.ANY),
        ],
        out_specs=pl.BlockSpec((tm, D_MODEL), lambda i: (i, 0)),
        out_shape=jax.ShapeDtypeStruct((T, D_MODEL), F32),
        scratch_shapes=[
            pltpu.VMEM((2, TOP_K, PCHUNKS, tm, LANES), U32),
            pltpu.SemaphoreType.DMA((2,)),
        ],
        compiler_params=pltpu.CompilerParams(
            dimension_semantics=("arbitrary",), vmem_limit_bytes=VMEM_LIMIT),
        name="combine",
    )(table, table, h, gates, g_final, ys3)


def kernel(x, mem, g_attn_norm, w_in, b_glu, w_dw, b_dw, g_cv_ln, b_cv_ln, w_pw2, b_pw2, g_mem, w_mem_kv, g_sb_out, g_cv_out, g_mx_out, w_out, g_ffn_norm, w_router, b_router, w_gu, b_gu, w_down, b_down, g_final):
    B, S, D = x.shape
    T = B * S
    assert D == D_MODEL and S % ROW_TILE == 0 and g_attn_norm.shape[0] == 1
    l = 0
    row = lambda v: v.reshape(1, -1)
    x2 = x.reshape(T, D)

    proj = _inproj(x2, row(g_attn_norm[l]), w_in[l].astype(BF16))
    proj3 = proj.reshape(B, S, IN_W)
    o_sb = _sb_attention(proj3)
    o_cv = _conformer(proj3, row(b_glu[l]), w_dw[l].reshape(CONV_K, CV_W), row(b_dw[l]),
                      row(g_cv_ln[l]), row(b_cv_ln[l]), w_pw2[l].astype(BF16), row(b_pw2[l]))
    o_mx = _memx(mem, row(g_mem[l]), w_mem_kv[l].astype(BF16), proj3)

    h, hn3, idx_pad, rank_pad, gate_pad, counts = _outproj(
        o_sb.reshape(T, SB_W), o_cv.reshape(T, CV_W), o_mx.reshape(T, MX_W), x2,
        row(g_sb_out[l]), row(g_cv_out[l]), row(g_mx_out[l]), w_out[l].astype(BF16),
        row(g_ffn_norm[l]), w_router[l], row(b_router[l]))

    bexp, n_used, dest, fill_start, fill_rows, n_rows = _routing_tables(
        counts[0].astype(jnp.int32), idx_pad[:, :TOP_K], rank_pad[:, :TOP_K])
    xs3 = _dispatch(fill_start, fill_rows, dest, hn3, n_rows)
    ys3 = _experts(bexp, n_used, xs3, w_gu[l], b_gu[l].reshape(N_EXPERTS, 1, 2 * D_FF),
                   w_down[l], b_down[l].reshape(N_EXPERTS, 1, D_MODEL))
    out = _combine(h, ys3, dest, gate_pad, row(g_final))
    return out.reshape(B, S, D)
```

```python
import functools

import jax
import jax.numpy as jnp
from jax import lax
from jax.experimental import pallas as pl
from jax.experimental.pallas import tpu as pltpu

F32 = jnp.float32
BF16 = jnp.bfloat16

D_MODEL = 1024
HEAD_DIM = 64
SB_W = 512
CV_W = 256
MX_W = 256
IN_W = 3 * SB_W + 2 * CV_W + MX_W
CONV_K = 31
N_MEM = 256
N_EXPERTS = 32
TOP_K = 4
D_FF = 1024
SWIGLU_ALPHA = 1.702
SWIGLU_LIMIT = 7.0
EPS = 1e-6

LANES = 128
SUBLANES = 8
CHUNKS = D_MODEL // LANES
HALF = D_MODEL // 2
PCHUNKS = HALF // LANES
U32 = jnp.uint32
ROW_TILE = 512
SB_TILE = 256
SB_PAIRS = 4
SB_EXP_ZERO_BELOW = -104.0
CONV_PAD = 32
CONV_ROWS = 128
MX_ROWS = 512
MOE_BM = 512
FF_CHUNK = 512
COMBINE_TILE = 512
DMA_UNROLL = 16
DMA_PRIORITIES = 2
VMEM_LIMIT = 56 * 1024 * 1024


def _rms(x, g):
    return x * lax.rsqrt(jnp.mean(x * x, axis=-1, keepdims=True) + EPS) * g


def _pack_halves(v):
    bits = pltpu.bitcast(v.astype(BF16).astype(F32), U32)
    return lax.shift_right_logical(bits[:, :HALF], U32(16)) | bits[:, HALF:]


def _unpack_halves(w):
    low = pltpu.bitcast(lax.shift_left(w, U32(16)), F32)
    high = pltpu.bitcast(w & U32(0xFFFF0000), F32)
    return low, high


def _inproj_kernel(x_ref, g_ref, w_ref, o_ref):
    xn = _rms(x_ref[...], g_ref[...]).astype(BF16)
    o_ref[...] = jnp.dot(xn, w_ref[...], preferred_element_type=F32).astype(BF16)


def _inproj(x2, g, w_bf):
    T = x2.shape[0]
    return pl.pallas_call(
        _inproj_kernel,
        grid=(T // ROW_TILE,),
        in_specs=[
            pl.BlockSpec((ROW_TILE, D_MODEL), lambda i: (i, 0)),
            pl.BlockSpec((1, D_MODEL), lambda i: (0, 0)),
            pl.BlockSpec((D_MODEL, IN_W), lambda i: (0, 0)),
        ],
        out_specs=pl.BlockSpec((ROW_TILE, IN_W), lambda i: (i, 0)),
        out_shape=jax.ShapeDtypeStruct((T, IN_W), BF16),
        compiler_params=pltpu.CompilerParams(
            dimension_semantics=("arbitrary",), vmem_limit_bytes=VMEM_LIMIT),
        name="inproj",
    )(x2, g, w_bf)


def _sb_kernel(q_ref, k_ref, v_ref, o_ref, *, seq):
    lane = lax.broadcasted_iota(jnp.int32, (SB_TILE, LANES), 1)
    head0 = lane < HEAD_DIM
    row = lax.broadcasted_iota(jnp.int32, (SB_TILE, SB_TILE), 0)
    col = lax.broadcasted_iota(jnp.int32, (SB_TILE, SB_TILE), 1)
    tri = jnp.where(row > col, 1.0, 0.0).astype(BF16)
    dmask = col < row

    def tile(head, qh, s0, c, diag):
        lanes = slice((head // 2) * LANES, (head // 2 + 1) * LANES)
        kb = k_ref[pl.ds(s0, SB_TILE), lanes]
        z = lax.dot_general(qh, kb, (((1,), (1,)), ((), ())), preferred_element_type=F32)
        lb = jnp.minimum(z, 0.0) - jnp.log(1.0 + jnp.exp(-jnp.abs(z)))
        l1m = lb - z
        if diag:
            l1m = jnp.where(dmask, l1m, 0.0)
        after = jnp.dot(l1m.astype(BF16), tri, preferred_element_type=F32)
        rowsum = jnp.broadcast_to(jnp.sum(l1m, axis=1, keepdims=True), (SB_TILE, LANES))
        arg = lb + after
        if c is not None:
            arg = arg + jnp.concatenate([c] * (SB_TILE // LANES), axis=1)
        a = jnp.exp(arg)
        if diag:
            a = jnp.where(dmask, a, 0.0)
        pv = jnp.dot(a.astype(BF16), v_ref[pl.ds(s0, SB_TILE), lanes],
                     preferred_element_type=F32)
        return pv, rowsum

    n_heads = 2 * SB_PAIRS

    def alive(cs):
        top = cs[0]
        for c in cs[1:]:
            top = jnp.maximum(top, c)
        return jnp.max(top) >= SB_EXP_ZERO_BELOW

    def heads(t0):
        out = []
        for p in range(SB_PAIRS):
            q = q_ref[pl.ds(t0, SB_TILE), p * LANES:(p + 1) * LANES] * (HEAD_DIM ** -0.5)
            out += [jnp.where(head0, q, jnp.zeros_like(q)), jnp.where(head0, jnp.zeros_like(q), q)]
        return out

    def store(t0, accs):
        for p in range(SB_PAIRS):
            o_ref[pl.ds(t0, SB_TILE), p * LANES:(p + 1) * LANES] = jnp.where(
                head0, accs[2 * p], accs[2 * p + 1]).astype(BF16)

    qs = heads(0)
    store(0, [tile(h, qs[h], 0, None, True)[0] for h in range(n_heads)])

    def qtile(i, _):
        t0 = pl.multiple_of(i * SB_TILE, SB_TILE)
        s0 = pl.multiple_of((i - 1) * SB_TILE, SB_TILE)
        qs = heads(t0)
        diag = [tile(h, qs[h], t0, None, True) for h in range(n_heads)]
        prev = [tile(h, qs[h], s0, diag[h][1], False) for h in range(n_heads)]
        accs = tuple(d[0] + p[0] for d, p in zip(diag, prev))
        cs = tuple(d[1] + p[1] for d, p in zip(diag, prev))

        def cond(st):
            return (st[0] >= 0) & st[1]

        def body(st):
            j, _, accs, cs = st
            s0 = pl.multiple_of(j * SB_TILE, SB_TILE)
            new = [tile(h, qs[h], s0, cs[h], False) for h in range(n_heads)]
            accs = tuple(a + t[0] for a, t in zip(accs, new))
            cs = tuple(c + t[1] for c, t in zip(cs, new))
            return j - 1, alive(cs), accs, cs

        st = lax.while_loop(cond, body, (i - 2, alive(cs), accs, cs))
        store(t0, st[2])
        return 0

    lax.fori_loop(1, seq // SB_TILE, qtile, 0)


def _sb_attention(proj3):
    B, S, _ = proj3.shape
    width = SB_PAIRS * LANES
    groups = SB_W // width
    return pl.pallas_call(
        functools.partial(_sb_kernel, seq=S),
        grid=(B, groups),
        in_specs=[
            pl.BlockSpec((None, S, width), lambda b, p: (b, 0, p)),
            pl.BlockSpec((None, S, width), lambda b, p: (b, 0, groups + p)),
            pl.BlockSpec((None, S, width), lambda b, p: (b, 0, 2 * groups + p)),
        ],
        out_specs=pl.BlockSpec((None, S, width), lambda b, p: (b, 0, p)),
        out_shape=jax.ShapeDtypeStruct((B, S, SB_W), BF16),
        compiler_params=pltpu.CompilerParams(
            dimension_semantics=("arbitrary", "arbitrary"), vmem_limit_bytes=VMEM_LIMIT),
        name="sb_attention",
    )(proj3, proj3, proj3)


def _conv_kernel(glu_ref, bglu_ref, wdw_ref, bdw_ref, gln_ref, bln_ref, wpw_ref, bpw_ref,
                 o_ref, upad_ref, shift_ref, *, seq):
    upad_ref[0:CONV_PAD, :] = jnp.zeros((CONV_PAD, CV_W), F32)
    for c in range(seq // CONV_ROWS):
        r0 = c * CONV_ROWS
        g = glu_ref[r0:r0 + CONV_ROWS, :].astype(F32) + bglu_ref[...]
        upad_ref[CONV_PAD + r0:CONV_PAD + r0 + CONV_ROWS, :] = (
            g[:, :CV_W] * jax.nn.sigmoid(g[:, CV_W:]))
    for c in range(seq // CONV_ROWS):
        r0 = c * CONV_ROWS
        acc = jnp.zeros((CONV_ROWS, CV_W), F32) + bdw_ref[...]
        span = CONV_ROWS + CONV_PAD - SUBLANES
        for r in range(1, SUBLANES):
            shift_ref[r, 0:span, :] = upad_ref[r0 + r:r0 + r + span, :]
        for k in range(CONV_K):
            off = CONV_PAD - (CONV_K - 1) + k
            r = off % SUBLANES
            a = off - r
            if r == 0:
                tap = upad_ref[r0 + a:r0 + a + CONV_ROWS, :]
            else:
                tap = shift_ref[r, a:a + CONV_ROWS, :]
            acc = acc + tap * wdw_ref[k:k + 1, :]
        mu = jnp.mean(acc, axis=-1, keepdims=True)
        d = acc - mu
        var = jnp.mean(d * d, axis=-1, keepdims=True)
        y = d * lax.rsqrt(var + EPS) * gln_ref[...] + bln_ref[...]
        y = y * jax.nn.sigmoid(y)
        out = jnp.dot(y.astype(BF16), wpw_ref[...], preferred_element_type=F32) + bpw_ref[...]
        o_ref[r0:r0 + CONV_ROWS, :] = out.astype(BF16)


def _conformer(proj3, b_glu, w_dw, b_dw, g_ln, b_ln, w_pw_bf, b_pw):
    B, S, _ = proj3.shape
    glu_block = (3 * SB_W) // (2 * CV_W)
    vec = lambda n: pl.BlockSpec((1, n), lambda b: (0, 0))
    return pl.pallas_call(
        functools.partial(_conv_kernel, seq=S),
        grid=(B,),
        in_specs=[
            pl.BlockSpec((None, S, 2 * CV_W), lambda b: (b, 0, glu_block)),
            vec(2 * CV_W),
            pl.BlockSpec((CONV_K, CV_W), lambda b: (0, 0)),
            vec(CV_W), vec(CV_W), vec(CV_W),
            pl.BlockSpec((CV_W, CV_W), lambda b: (0, 0)),
            vec(CV_W),
        ],
        out_specs=pl.BlockSpec((None, S, CV_W), lambda b: (b, 0, 0)),
        out_shape=jax.ShapeDtypeStruct((B, S, CV_W), BF16),
        scratch_shapes=[pltpu.VMEM((CONV_PAD + S, CV_W), F32),
                        pltpu.VMEM((SUBLANES, CONV_ROWS + CONV_PAD, CV_W), F32)],
        compiler_params=pltpu.CompilerParams(
            dimension_semantics=("arbitrary",), vmem_limit_bytes=VMEM_LIMIT),
        name="conformer",
    )(proj3, b_glu, w_dw, b_dw, g_ln, b_ln, w_pw_bf, b_pw)


def _memx_kernel(mem_ref, gm_ref, wkv_ref, q_ref, o_ref, *, seq):
    scale = HEAD_DIM ** -0.5
    mn = _rms(mem_ref[...], gm_ref[...]).astype(BF16)
    kv = jnp.dot(mn, wkv_ref[...], preferred_element_type=F32)
    km = kv[:, :MX_W].astype(BF16)
    vm = kv[:, MX_W:].astype(BF16)
    lane = lax.broadcasted_iota(jnp.int32, (MX_ROWS, MX_W), 1)

    def chunk(c, _):
        r0 = pl.multiple_of(c * MX_ROWS, MX_ROWS)
        q = q_ref[pl.ds(r0, MX_ROWS), :]
        out = jnp.zeros((MX_ROWS, MX_W), F32)
        for h in range(MX_W // HEAD_DIM):
            head = (lane >= HEAD_DIM * h) & (lane < HEAD_DIM * (h + 1))
            qh = jnp.where(head, q, jnp.zeros_like(q))
            s = lax.dot_general(qh, km, (((1,), (1,)), ((), ())),
                                preferred_element_type=F32) * scale
            p = jnp.exp(s - jnp.max(s, axis=-1, keepdims=True))
            p = p / jnp.sum(p, axis=-1, keepdims=True)
            oh = jnp.dot(p.astype(BF16), vm, preferred_element_type=F32)
            out = jnp.where(head, oh, out)
        o_ref[pl.ds(r0, MX_ROWS), :] = out.astype(BF16)
        return 0

    lax.fori_loop(0, seq // MX_ROWS, chunk, 0)


def _memx(mem, g_mem, w_kv_bf, proj3):
    B, S, _ = proj3.shape
    q_block = (3 * SB_W + 2 * CV_W) // MX_W
    return pl.pallas_call(
        functools.partial(_memx_kernel, seq=S),
        grid=(B,),
        in_specs=[
            pl.BlockSpec((None, N_MEM, D_MODEL), lambda b: (b, 0, 0)),
            pl.BlockSpec((1, D_MODEL), lambda b: (0, 0)),
            pl.BlockSpec((D_MODEL, 2 * MX_W), lambda b: (0, 0)),
            pl.BlockSpec((None, S, MX_W), lambda b: (b, 0, q_block)),
        ],
        out_specs=pl.BlockSpec((None, S, MX_W), lambda b: (b, 0, 0)),
        out_shape=jax.ShapeDtypeStruct((B, S, MX_W), BF16),
        compiler_params=pltpu.CompilerParams(
            dimension_semantics=("arbitrary",), vmem_limit_bytes=VMEM_LIMIT),
        name="memx",
    )(mem, g_mem, w_kv_bf, proj3)


def _outproj_kernel(sb_ref, cv_ref, mx_ref, x_ref, gsb_ref, gcv_ref, gmx_ref, wo_ref,
                    gffn_ref, wr_ref, br_ref, h_ref, hn3_ref, idx_ref, rank_ref, gate_ref,
                    cnt_ref, run_ref):
    def normed(o_ref, g_ref):
        return _rms(o_ref[...].astype(F32), g_ref[...]).astype(BF16)

    mix = jnp.dot(normed(sb_ref, gsb_ref), wo_ref[0:SB_W, :], preferred_element_type=F32)
    mix += jnp.dot(normed(cv_ref, gcv_ref), wo_ref[SB_W:SB_W + CV_W, :],
                   preferred_element_type=F32)
    mix += jnp.dot(normed(mx_ref, gmx_ref), wo_ref[SB_W + CV_W:, :],
                   preferred_element_type=F32)
    h = x_ref[...] + mix
    h_ref[...] = h
    hn = _rms(h, gffn_ref[...])
    packed = _pack_halves(hn)
    for s in range(PCHUNKS):
        hn3_ref[pl.ds(s, ROW_TILE, stride=PCHUNKS), :] = packed[:, s * LANES:(s + 1) * LANES]
    wr = wr_ref[...]
    hn_hi = hn.astype(BF16)
    hn_lo = (hn - hn_hi.astype(F32)).astype(BF16)
    wr_hi = wr.astype(BF16)
    wr_lo = (wr - wr_hi.astype(F32)).astype(BF16)
    logits = (jnp.dot(hn_hi, wr_hi, preferred_element_type=F32)
              + jnp.dot(hn_lo, wr_hi, preferred_element_type=F32)
              + jnp.dot(hn_hi, wr_lo, preferred_element_type=F32)) + br_ref[...]
    eid = lax.broadcasted_iota(jnp.int32, logits.shape, 1)
    vals, idxs = [], []
    for _ in range(TOP_K):
        m = jnp.max(logits, axis=-1, keepdims=True)
        i = jnp.min(jnp.where(logits == m, eid, N_EXPERTS), axis=-1, keepdims=True)
        vals.append(m)
        idxs.append(i)
        logits = jnp.where(eid == i, -jnp.inf, logits)
    es = [jnp.exp(v - vals[0]) for v in vals]
    denom = es[0] + es[1] + es[2] + es[3]

    @pl.when(pl.program_id(0) == 0)
    def _():
        run_ref[...] = jnp.zeros_like(run_ref)
    tm = logits.shape[0]
    row = lax.broadcasted_iota(jnp.int32, (tm, tm), 0)
    col = lax.broadcasted_iota(jnp.int32, (tm, tm), 1)
    before = jnp.where(col < row, 1.0, 0.0).astype(BF16)
    base = run_ref[...]
    ranks = []
    for k in range(TOP_K):
        onehot = jnp.where(eid == idxs[k], 1.0, 0.0)
        prefix = jnp.dot(before, onehot.astype(BF16), preferred_element_type=F32)
        ranks.append(jnp.sum(onehot * (prefix + base), axis=-1, keepdims=True))
        base = base + jnp.sum(onehot, axis=0, keepdims=True)
    run_ref[...] = base
    cnt_ref[...] = jnp.broadcast_to(base, cnt_ref.shape)

    lane = lax.broadcasted_iota(jnp.int32, idx_ref.shape, 1)
    idx_out = jnp.zeros(idx_ref.shape, jnp.int32)
    rank_out = jnp.zeros(rank_ref.shape, jnp.int32)
    gate_out = jnp.zeros(gate_ref.shape, F32)
    for k in range(TOP_K):
        idx_out = jnp.where(lane == k, idxs[k], idx_out)
        rank_out = jnp.where(lane == k, ranks[k].astype(jnp.int32), rank_out)
        gate_out = jnp.where(lane == k, es[k] / denom, gate_out)
    idx_ref[...] = idx_out
    rank_ref[...] = rank_out
    gate_ref[...] = gate_out


def _outproj(o_sb, o_cv, o_mx, x2, g_sb, g_cv, g_mx, w_out_bf, g_ffn, w_router, b_router):
    T = x2.shape[0]
    rows = lambda n: pl.BlockSpec((ROW_TILE, n), lambda i: (i, 0))
    full = lambda a, b: pl.BlockSpec((a, b), lambda i: (0, 0))
    return pl.pallas_call(
        _outproj_kernel,
        grid=(T // ROW_TILE,),
        in_specs=[
            rows(SB_W), rows(CV_W), rows(MX_W), rows(D_MODEL),
            full(1, SB_W), full(1, CV_W), full(1, MX_W),
            full(D_MODEL, D_MODEL), full(1, D_MODEL),
            full(D_MODEL, N_EXPERTS), full(1, N_EXPERTS),
        ],
        out_specs=[rows(D_MODEL),
                   pl.BlockSpec((ROW_TILE * PCHUNKS, LANES), lambda i: (i, 0)),
                   rows(LANES), rows(LANES), rows(LANES),
                   full(SUBLANES, N_EXPERTS)],
        out_shape=[
            jax.ShapeDtypeStruct((T, D_MODEL), F32),
            jax.ShapeDtypeStruct((T * PCHUNKS, LANES), U32),
            jax.ShapeDtypeStruct((T, LANES), jnp.int32),
            jax.ShapeDtypeStruct((T, LANES), jnp.int32),
            jax.ShapeDtypeStruct((T, LANES), F32),
            jax.ShapeDtypeStruct((SUBLANES, N_EXPERTS), F32),
        ],
        scratch_shapes=[pltpu.VMEM((1, N_EXPERTS), F32)],
        compiler_params=pltpu.CompilerParams(
            dimension_semantics=("arbitrary",), vmem_limit_bytes=VMEM_LIMIT),
        name="outproj_router",
    )(o_sb, o_cv, o_mx, x2, g_sb, g_cv, g_mx, w_out_bf, g_ffn, w_router, b_router)


def _rows_copy(src_ref, src_row, dst_ref, dst_row, n, sem):
    src = src_ref.at[pl.ds(pl.multiple_of(src_row * PCHUNKS, PCHUNKS), n * PCHUNKS), :]
    dst = dst_ref.at[pl.ds(pl.multiple_of(dst_row * PCHUNKS, PCHUNKS), n * PCHUNKS), :]
    return pltpu.make_async_copy(src, dst, sem)


def _chunk_rows(ref, row, n):
    return ref.at[:, pl.ds(row, n), :]


def _dispatch_kernel(fill_start_ref, fill_rows_ref, dest_ref, hn3_ref, xs3_ref,
                     tile_ref, zero_ref, sem_ref, load_sem_ref, fill_sem_ref, *, tm):
    i = pl.program_id(0)
    n = pl.num_programs(0)
    slot = lax.rem(i, 3)
    slot_next = lax.rem(i + 1, 3)
    n_copies = tm * TOP_K
    pieces = [1 << p for p in range(MOE_BM.bit_length())]

    def fill(wait):
        for e in range(2 * N_EXPERTS):
            rows = fill_rows_ref[e]
            start = fill_start_ref[e]
            for p in pieces:
                @pl.when((rows & p) != 0)
                def _(p=p, start=start, rows=rows):
                    cp = _rows_copy(zero_ref, 0, xs3_ref, start + (rows & (p - 1)), p,
                                    fill_sem_ref.at[0])
                    if wait:
                        cp.wait()
                    else:
                        cp.start()

    @pl.when(i == 0)
    def _():
        zero_ref[...] = jnp.zeros_like(zero_ref)
        fill(wait=False)

    def tile_load(j, s):
        return _rows_copy(hn3_ref, j * tm, tile_ref.at[s], 0, tm, load_sem_ref.at[s])

    def rows_wait(s):
        for _ in range(TOP_K):
            _rows_copy(tile_ref.at[s], 0, xs3_ref, 0, tm, sem_ref.at[s]).wait()

    @pl.when(i == 0)
    def _():
        tile_load(0, 0).start()

    @pl.when(i >= 2)
    def _():
        rows_wait(slot_next)

    @pl.when(i + 1 < n)
    def _():
        tile_load(i + 1, slot_next).start()

    tile_load(i, slot).wait()

    def group(g, _):
        m0 = pl.multiple_of(g * DMA_UNROLL, DMA_UNROLL)
        dests = [dest_ref[0, 0, m0 + u] for u in range(DMA_UNROLL)]
        for u in range(DMA_UNROLL):
            r = g * (DMA_UNROLL // TOP_K) + u // TOP_K
            _rows_copy(tile_ref.at[slot], r, xs3_ref, dests[u], 1,
                       sem_ref.at[slot]).start(priority=u % DMA_PRIORITIES)
        return 0
    lax.fori_loop(0, n_copies // DMA_UNROLL, group, 0)

    @pl.when(i == 0)
    def _():
        fill(wait=True)

    @pl.when(i == n - 1)
    def _():
        @pl.when(i >= 1)
        def _():
            rows_wait(lax.rem(i + 2, 3))
        rows_wait(slot)


def _dispatch(fill_start, fill_rows, dest, hn3, n_blocks):
    T = hn3.shape[0] // PCHUNKS
    tm = ROW_TILE
    grid_spec = pltpu.PrefetchScalarGridSpec(
        num_scalar_prefetch=2,
        grid=(T // tm,),
        in_specs=[
            pl.BlockSpec((1, 1, tm * TOP_K), lambda i, fs, fr: (i, 0, 0),
                         memory_space=pltpu.SMEM),
            pl.BlockSpec(memory_space=pl.ANY),
        ],
        out_specs=pl.BlockSpec(memory_space=pl.ANY),
        scratch_shapes=[
            pltpu.VMEM((3, tm * PCHUNKS, LANES), U32),
            pltpu.VMEM((MOE_BM * PCHUNKS, LANES), U32),
            pltpu.SemaphoreType.DMA((3,)),
            pltpu.SemaphoreType.DMA((3,)),
            pltpu.SemaphoreType.DMA((1,)),
        ],
    )
    return pl.pallas_call(
        functools.partial(_dispatch_kernel, tm=tm),
        grid_spec=grid_spec,
        out_shape=jax.ShapeDtypeStruct((n_blocks * MOE_BM * PCHUNKS, LANES), U32),
        compiler_params=pltpu.CompilerParams(
            dimension_semantics=("arbitrary",), vmem_limit_bytes=VMEM_LIMIT),
        name="dispatch",
    )(fill_start, fill_rows, dest.reshape(T // tm, 1, tm * TOP_K), hn3)


def _experts_kernel(bexp_ref, nused_ref, x_ref, wgu_ref, bgu_ref, wd_ref, bd_ref, y_ref,
                    hm_ref, wgu_bf_ref, wd_bf_ref):
    b = pl.program_id(0)

    @pl.when(b >= nused_ref[0])
    def _():
        y_ref[...] = jnp.zeros_like(y_ref)

    @pl.when(b < nused_ref[0])
    def _():
        prev = jnp.maximum(b - 1, 0)
        @pl.when((b == 0) | (bexp_ref[b] != bexp_ref[prev]))
        def _():
            wgu_bf_ref[...] = wgu_ref[...].astype(BF16)
            wd_bf_ref[...] = wd_ref[...].astype(BF16)

        halves = [_unpack_halves(x_ref[pl.ds(s, MOE_BM, stride=PCHUNKS), :])
                  for s in range(PCHUNKS)]
        x = jnp.concatenate([lo for lo, _ in halves] + [hi for _, hi in halves],
                            axis=1).astype(BF16)

        for c0 in range(0, D_FF, FF_CHUNK):
            g_cols = slice(c0, c0 + FF_CHUNK)
            u_cols = slice(D_FF + c0, D_FF + c0 + FF_CHUNK)
            gate = jnp.dot(x, wgu_bf_ref[:, g_cols], preferred_element_type=F32) + bgu_ref[:, g_cols]
            up = jnp.dot(x, wgu_bf_ref[:, u_cols], preferred_element_type=F32) + bgu_ref[:, u_cols]
            gate = jnp.minimum(gate, SWIGLU_LIMIT)
            up = jnp.clip(up, -SWIGLU_LIMIT, SWIGLU_LIMIT)
            hm_ref[:, g_cols] = ((up + 1.0) * (gate * jax.nn.sigmoid(SWIGLU_ALPHA * gate))
                                 ).astype(BF16)
        y = jnp.dot(hm_ref[...], wd_bf_ref[...], preferred_element_type=F32) + bd_ref[...]
        packed = _pack_halves(y)
        for s in range(PCHUNKS):
            y_ref[s] = packed[:, s * LANES:(s + 1) * LANES]


def _experts(block_exp, n_used, xs3, w_gu, b_gu, w_down, b_down):
    NB = block_exp.shape[0]
    used = lambda b, nu: jnp.minimum(b, nu[0] - 1)
    grid_spec = pltpu.PrefetchScalarGridSpec(
        num_scalar_prefetch=2,
        grid=(NB,),
        in_specs=[
            pl.BlockSpec((MOE_BM * PCHUNKS, LANES), lambda b, be, nu: (used(b, nu), 0)),
            pl.BlockSpec((None, D_MODEL, 2 * D_FF), lambda b, be, nu: (be[b], 0, 0)),
            pl.BlockSpec((None, 1, 2 * D_FF), lambda b, be, nu: (be[b], 0, 0)),
            pl.BlockSpec((None, D_FF, D_MODEL), lambda b, be, nu: (be[b], 0, 0)),
            pl.BlockSpec((None, 1, D_MODEL), lambda b, be, nu: (be[b], 0, 0)),
        ],
        out_specs=pl.BlockSpec((PCHUNKS, MOE_BM, LANES), lambda b, be, nu: (0, b, 0)),
        scratch_shapes=[
            pltpu.VMEM((MOE_BM, D_FF), BF16),
            pltpu.VMEM((D_MODEL, 2 * D_FF), BF16),
            pltpu.VMEM((D_FF, D_MODEL), BF16),
        ],
    )
    return pl.pallas_call(
        _experts_kernel,
        grid_spec=grid_spec,
        out_shape=jax.ShapeDtypeStruct((PCHUNKS, NB * MOE_BM, LANES), U32),
        compiler_params=pltpu.CompilerParams(
            dimension_semantics=("arbitrary",), vmem_limit_bytes=VMEM_LIMIT),
        name="experts",
    )(block_exp, n_used, xs3, w_gu, b_gu, w_down, b_down)


def _routing_tables(counts, top_idx, rank):
    M = top_idx.shape[0] * TOP_K
    NB = -(-M // MOE_BM) + N_EXPERTS
    nblk_e = (counts + MOE_BM - 1) // MOE_BM
    blk_end = jnp.cumsum(nblk_e)
    row_start = (blk_end - nblk_e) * MOE_BM
    n_used = blk_end[-1]
    blk = jnp.minimum(jnp.arange(NB, dtype=jnp.int32), n_used - 1)
    bexp = jnp.sum((blk[:, None] >= blk_end[None, :]).astype(jnp.int32), axis=1)
    experts = jnp.arange(N_EXPERTS, dtype=jnp.int32)
    dest = rank + jnp.sum(jnp.where(top_idx[:, :, None] == experts, row_start, 0), axis=-1)
    tail = NB - N_EXPERTS + experts
    fill_start = jnp.concatenate([row_start + counts, tail * MOE_BM])
    fill_rows = jnp.concatenate([nblk_e * MOE_BM - counts,
                                 jnp.where(tail >= n_used, MOE_BM, 0)])
    return (bexp.astype(jnp.int32), n_used.reshape(1).astype(jnp.int32),
            dest.astype(jnp.int32), fill_start.astype(jnp.int32),
            fill_rows.astype(jnp.int32), NB)


def _combine_kernel(dest_ref, dest_next_ref, h_ref, gate_ref, g_ref, ys3_ref, o_ref,
                    buf_ref, sem_ref, *, tm):
    i = pl.program_id(0)
    n = pl.num_programs(0)
    slot = lax.rem(i, 2)

    def start_gather(idx_ref, s):
        def group(g, _):
            m0 = pl.multiple_of(g * DMA_UNROLL, DMA_UNROLL)
            srcs = [idx_ref[0, 0, m0 + u] for u in range(DMA_UNROLL)]
            for u in range(DMA_UNROLL):
                r = g * (DMA_UNROLL // TOP_K) + u // TOP_K
                pltpu.make_async_copy(
                    _chunk_rows(ys3_ref, srcs[u], 1),
                    _chunk_rows(buf_ref.at[s, u % TOP_K], r, 1),
                    sem_ref.at[s]).start(priority=u % DMA_PRIORITIES)
            return 0
        lax.fori_loop(0, tm * TOP_K // DMA_UNROLL, group, 0)

    @pl.when(i == 0)
    def _():
        start_gather(dest_ref, 0)

    @pl.when(i + 1 < n)
    def _():
        start_gather(dest_next_ref, 1 - slot)

    for k in range(TOP_K):
        pltpu.make_async_copy(_chunk_rows(ys3_ref, 0, tm), buf_ref.at[slot, k],
                              sem_ref.at[slot]).wait()

    gates = gate_ref[...]
    chunks = [None] * CHUNKS
    ssq = jnp.zeros((tm, 1), F32)
    for s in range(PCHUNKS):
        acc_lo = h_ref[:, s * LANES:(s + 1) * LANES]
        acc_hi = h_ref[:, HALF + s * LANES:HALF + (s + 1) * LANES]
        for k in range(TOP_K):
            lo, hi = _unpack_halves(buf_ref[slot, k, s])
            acc_lo = acc_lo + gates[:, k:k + 1] * lo
            acc_hi = acc_hi + gates[:, k:k + 1] * hi
        chunks[s], chunks[PCHUNKS + s] = acc_lo, acc_hi
        ssq = ssq + jnp.sum(acc_lo * acc_lo + acc_hi * acc_hi, axis=-1, keepdims=True)
    inv = lax.rsqrt(ssq * (1.0 / D_MODEL) + EPS)
    for s in range(CHUNKS):
        o_ref[:, s * LANES:(s + 1) * LANES] = chunks[s] * inv * g_ref[:, s * LANES:(s + 1) * LANES]


def _combine(h, ys3, dest, gates, g_final):
    T = h.shape[0]
    tm = COMBINE_TILE
    n = T // tm
    table = dest.reshape(n, 1, tm * TOP_K)
    idx_spec = lambda f: pl.BlockSpec((1, 1, tm * TOP_K), lambda i: (f(i), 0, 0),
                                      memory_space=pltpu.SMEM)
    return pl.pallas_call(
        functools.partial(_combine_kernel, tm=tm),
        grid=(n,),
        in_specs=[
            idx_spec(lambda i: i), idx_spec(lambda i: jnp.minimum(i + 1, n - 1)),
            pl.BlockSpec((tm, D_MODEL), lambda i: (i, 0)),
            pl.BlockSpec((tm, LANES), lambda i: (i, 0)),
            pl.BlockSpec((1, D_MODEL), lambda i: (0, 0)),
            pl.BlockSpec(memory_space=pl.ANY),
        ],
        out_specs=pl.BlockSpec((tm, D_MODEL), lambda i: (i, 0)),
        out_shape=jax.ShapeDtypeStruct((T, D_MODEL), F32),
        scratch_shapes=[
            pltpu.VMEM((2, TOP_K, PCHUNKS, tm, LANES), U32),
            pltpu.SemaphoreType.DMA((2,)),
        ],
        compiler_params=pltpu.CompilerParams(
            dimension_semantics=("arbitrary",), vmem_limit_bytes=VMEM_LIMIT),
        name="combine",
    )(table, table, h, gates, g_final, ys3)


def kernel(x, mem, g_attn_norm, w_in, b_glu, w_dw, b_dw, g_cv_ln, b_cv_ln, w_pw2, b_pw2, g_mem, w_mem_kv, g_sb_out, g_cv_out, g_mx_out, w_out, g_ffn_norm, w_router, b_router, w_gu, b_gu, w_down, b_down, g_final):
    B, S, D = x.shape
    T = B * S
    assert D == D_MODEL and S % ROW_TILE == 0 and g_attn_norm.shape[0] == 1
    l = 0
    row = lambda v: v.reshape(1, -1)
    x2 = x.reshape(T, D)

    proj = _inproj(x2, row(g_attn_norm[l]), w_in[l].astype(BF16))
    proj3 = proj.reshape(B, S, IN_W)
    o_sb = _sb_attention(proj3)
    o_cv = _conformer(proj3, row(b_glu[l]), w_dw[l].reshape(CONV_K, CV_W), row(b_dw[l]),
                      row(g_cv_ln[l]), row(b_cv_ln[l]), w_pw2[l].astype(BF16), row(b_pw2[l]))
    o_mx = _memx(mem, row(g_mem[l]), w_mem_kv[l].astype(BF16), proj3)

    h, hn3, idx_pad, rank_pad, gate_pad, counts = _outproj(
        o_sb.reshape(T, SB_W), o_cv.reshape(T, CV_W), o_mx.reshape(T, MX_W), x2,
        row(g_sb_out[l]), row(g_cv_out[l]), row(g_mx_out[l]), w_out[l].astype(BF16),
        row(g_ffn_norm[l]), w_router[l], row(b_router[l]))

    bexp, n_used, dest, fill_start, fill_rows, n_rows = _routing_tables(
        counts[0].astype(jnp.int32), idx_pad[:, :TOP_K], rank_pad[:, :TOP_K])
    xs3 = _dispatch(fill_start, fill_rows, dest, hn3, n_rows)
    ys3 = _experts(bexp, n_used, xs3, w_gu[l], b_gu[l].reshape(N_EXPERTS, 1, 2 * D_FF),
                   w_down[l], b_down[l].reshape(N_EXPERTS, 1, D_MODEL))
    out = _combine(h, ys3, dest, gate_pad, row(g_final))
    return out.reshape(B, S, D)
```

```python
import functools

import jax
import jax.numpy as jnp
from jax import lax
from jax.experimental import pallas as pl
from jax.experimental.pallas import tpu as pltpu

F32 = jnp.float32
BF16 = jnp.bfloat16

D_MODEL = 1024
HEAD_DIM = 64
SB_W = 512
CV_W = 256
MX_W = 256
IN_W = 3 * SB_W + 2 * CV_W + MX_W
CONV_K = 31
N_MEM = 256
N_EXPERTS = 32
TOP_K = 4
D_FF = 1024
SWIGLU_ALPHA = 1.702
SWIGLU_LIMIT = 7.0
EPS = 1e-6

LANES = 128
SUBLANES = 8
CHUNKS = D_MODEL // LANES
HALF = D_MODEL // 2
PCHUNKS = HALF // LANES
U32 = jnp.uint32
ROW_TILE = 512
SB_TILE = 256
SB_PAIRS = 4
SB_EXP_ZERO_BELOW = -104.0
CONV_PAD = 32
CONV_ROWS = 128
MX_ROWS = 512
MOE_BM = 512
FF_CHUNK = 512
COMBINE_TILE = 512
DMA_UNROLL = 16
DMA_PRIORITIES = 2
VMEM_LIMIT = 56 * 1024 * 1024


def _rms(x, g):
    return x * lax.rsqrt(jnp.mean(x * x, axis=-1, keepdims=True) + EPS) * g


def _pack_halves(v):
    bits = pltpu.bitcast(v.astype(BF16).astype(F32), U32)
    return lax.shift_right_logical(bits[:, :HALF], U32(16)) | bits[:, HALF:]


def _unpack_halves(w):
    low = pltpu.bitcast(lax.shift_left(w, U32(16)), F32)
    high = pltpu.bitcast(w & U32(0xFFFF0000), F32)
    return low, high


def _inproj_kernel(x_ref, g_ref, w_ref, o_ref):
    xn = _rms(x_ref[...], g_ref[...]).astype(BF16)
    o_ref[...] = jnp.dot(xn, w_ref[...], preferred_element_type=F32).astype(BF16)


def _inproj(x2, g, w_bf):
    T = x2.shape[0]
    return pl.pallas_call(
        _inproj_kernel,
        grid=(T // ROW_TILE,),
        in_specs=[
            pl.BlockSpec((ROW_TILE, D_MODEL), lambda i: (i, 0)),
            pl.BlockSpec((1, D_MODEL), lambda i: (0, 0)),
            pl.BlockSpec((D_MODEL, IN_W), lambda i: (0, 0)),
        ],
        out_specs=pl.BlockSpec((ROW_TILE, IN_W), lambda i: (i, 0)),
        out_shape=jax.ShapeDtypeStruct((T, IN_W), BF16),
        compiler_params=pltpu.CompilerParams(
            dimension_semantics=("arbitrary",), vmem_limit_bytes=VMEM_LIMIT),
        name="inproj",
    )(x2, g, w_bf)


def _sb_kernel(q_ref, k_ref, v_ref, o_ref, *, seq):
    lane = lax.broadcasted_iota(jnp.int32, (SB_TILE, LANES), 1)
    head0 = lane < HEAD_DIM
    row = lax.broadcasted_iota(jnp.int32, (SB_TILE, SB_TILE), 0)
    col = lax.broadcasted_iota(jnp.int32, (SB_TILE, SB_TILE), 1)
    tri = jnp.where(row > col, 1.0, 0.0).astype(BF16)
    dmask = col < row

    def tile(head, qh, s0, c, diag):
        lanes = slice((head // 2) * LANES, (head // 2 + 1) * LANES)
        kb = k_ref[pl.ds(s0, SB_TILE), lanes]
        z = lax.dot_general(qh, kb, (((1,), (1,)), ((), ())), preferred_element_type=F32)
        lb = jnp.minimum(z, 0.0) - jnp.log(1.0 + jnp.exp(-jnp.abs(z)))
        l1m = lb - z
        if diag:
            l1m = jnp.where(dmask, l1m, 0.0)
        after = jnp.dot(l1m.astype(BF16), tri, preferred_element_type=F32)
        rowsum = jnp.broadcast_to(jnp.sum(l1m, axis=1, keepdims=True), (SB_TILE, LANES))
        arg = lb + after
        if c is not None:
            arg = arg + jnp.concatenate([c] * (SB_TILE // LANES), axis=1)
        a = jnp.exp(arg)
        if diag:
            a = jnp.where(dmask, a, 0.0)
        pv = jnp.dot(a.astype(BF16), v_ref[pl.ds(s0, SB_TILE), lanes],
                     preferred_element_type=F32)
        return pv, rowsum

    n_heads = 2 * SB_PAIRS

    def alive(cs):
        top = cs[0]
        for c in cs[1:]:
            top = jnp.maximum(top, c)
        return jnp.max(top) >= SB_EXP_ZERO_BELOW

    def heads(t0):
        out = []
        for p in range(SB_PAIRS):
            q = q_ref[pl.ds(t0, SB_TILE), p * LANES:(p + 1) * LANES] * (HEAD_DIM ** -0.5)
            out += [jnp.where(head0, q, jnp.zeros_like(q)), jnp.where(head0, jnp.zeros_like(q), q)]
        return out

    def store(t0, accs):
        for p in range(SB_PAIRS):
            o_ref[pl.ds(t0, SB_TILE), p * LANES:(p + 1) * LANES] = jnp.where(
                head0, accs[2 * p], accs[2 * p + 1]).astype(BF16)

    qs = heads(0)
    store(0, [tile(h, qs[h], 0, None, True)[0] for h in range(n_heads)])

    def qtile(i, _):
        t0 = pl.multiple_of(i * SB_TILE, SB_TILE)
        s0 = pl.multiple_of((i - 1) * SB_TILE, SB_TILE)
        qs = heads(t0)
        diag = [tile(h, qs[h], t0, None, True) for h in range(n_heads)]
        prev = [tile(h, qs[h], s0, diag[h][1], False) for h in range(n_heads)]
        accs = tuple(d[0] + p[0] for d, p in zip(diag, prev))
        cs = tuple(d[1] + p[1] for d, p in zip(diag, prev))

        def cond(st):
            return (st[0] >= 0) & st[1]

        def body(st):
            j, _, accs, cs = st
            s0 = pl.multiple_of(j * SB_TILE, SB_TILE)
            new = [tile(h, qs[h], s0, cs[h], False) for h in range(n_heads)]
            accs = tuple(a + t[0] for a, t in zip(accs, new))
            cs = tuple(c + t[1] for c, t in zip(cs, new))
            return j - 1, alive(cs), accs, cs

        st = lax.while_loop(cond, body, (i - 2, alive(cs), accs, cs))
        store(t0, st[2])
        return 0

    lax.fori_loop(1, seq // SB_TILE, qtile, 0)


def _sb_attention(proj3):
    B, S, _ = proj3.shape
    width = SB_PAIRS * LANES
    groups = SB_W // width
    return pl.pallas_call(
        functools.partial(_sb_kernel, seq=S),
        grid=(B, groups),
        in_specs=[
            pl.BlockSpec((None, S, width), lambda b, p: (b, 0, p)),
            pl.BlockSpec((None, S, width), lambda b, p: (b, 0, groups + p)),
            pl.BlockSpec((None, S, width), lambda b, p: (b, 0, 2 * groups + p)),
        ],
        out_specs=pl.BlockSpec((None, S, width), lambda b, p: (b, 0, p)),
        out_shape=jax.ShapeDtypeStruct((B, S, SB_W), BF16),
        compiler_params=pltpu.CompilerParams(
            dimension_semantics=("arbitrary", "arbitrary"), vmem_limit_bytes=VMEM_LIMIT),
        name="sb_attention",
    )(proj3, proj3, proj3)


def _conv_kernel(glu_ref, bglu_ref, wdw_ref, bdw_ref, gln_ref, bln_ref, wpw_ref, bpw_ref,
                 o_ref, upad_ref, shift_ref, *, seq):
    upad_ref[0:CONV_PAD, :] = jnp.zeros((CONV_PAD, CV_W), F32)
    for c in range(seq // CONV_ROWS):
        r0 = c * CONV_ROWS
        g = glu_ref[r0:r0 + CONV_ROWS, :].astype(F32) + bglu_ref[...]
        upad_ref[CONV_PAD + r0:CONV_PAD + r0 + CONV_ROWS, :] = (
            g[:, :CV_W] * jax.nn.sigmoid(g[:, CV_W:]))
    for c in range(seq // CONV_ROWS):
        r0 = c * CONV_ROWS
        acc = jnp.zeros((CONV_ROWS, CV_W), F32) + bdw_ref[...]
        span = CONV_ROWS + CONV_PAD - SUBLANES
        for r in range(1, SUBLANES):
            shift_ref[r, 0:span, :] = upad_ref[r0 + r:r0 + r + span, :]
        for k in range(CONV_K):
            off = CONV_PAD - (CONV_K - 1) + k
            r = off % SUBLANES
            a = off - r
            if r == 0:
                tap = upad_ref[r0 + a:r0 + a + CONV_ROWS, :]
            else:
                tap = shift_ref[r, a:a + CONV_ROWS, :]
            acc = acc + tap * wdw_ref[k:k + 1, :]
        mu = jnp.mean(acc, axis=-1, keepdims=True)
        d = acc - mu
        var = jnp.mean(d * d, axis=-1, keepdims=True)
        y = d * lax.rsqrt(var + EPS) * gln_ref[...] + bln_ref[...]
        y = y * jax.nn.sigmoid(y)
        out = jnp.dot(y.astype(BF16), wpw_ref[...], preferred_element_type=F32) + bpw_ref[...]
        o_ref[r0:r0 + CONV_ROWS, :] = out.astype(BF16)


def _conformer(proj3, b_glu, w_dw, b_dw, g_ln, b_ln, w_pw_bf, b_pw):
    B, S, _ = proj3.shape
    glu_block = (3 * SB_W) // (2 * CV_W)
    vec = lambda n: pl.BlockSpec((1, n), lambda b: (0, 0))
    return pl.pallas_call(
        functools.partial(_conv_kernel, seq=S),
        grid=(B,),
        in_specs=[
            pl.BlockSpec((None, S, 2 * CV_W), lambda b: (b, 0, glu_block)),
            vec(2 * CV_W),
            pl.BlockSpec((CONV_K, CV_W), lambda b: (0, 0)),
            vec(CV_W), vec(CV_W), vec(CV_W),
            pl.BlockSpec((CV_W, CV_W), lambda b: (0, 0)),
            vec(CV_W),
        ],
        out_specs=pl.BlockSpec((None, S, CV_W), lambda b: (b, 0, 0)),
        out_shape=jax.ShapeDtypeStruct((B, S, CV_W), BF16),
        scratch_shapes=[pltpu.VMEM((CONV_PAD + S, CV_W), F32),
                        pltpu.VMEM((SUBLANES, CONV_ROWS + CONV_PAD, CV_W), F32)],
        compiler_params=pltpu.CompilerParams(
            dimension_semantics=("arbitrary",), vmem_limit_bytes=VMEM_LIMIT),
        name="conformer",
    )(proj3, b_glu, w_dw, b_dw, g_ln, b_ln, w_pw_bf, b_pw)


def _memx_kernel(mem_ref, gm_ref, wkv_ref, q_ref, o_ref, *, seq):
    scale = HEAD_DIM ** -0.5
    mn = _rms(mem_ref[...], gm_ref[...]).astype(BF16)
    kv = jnp.dot(mn, wkv_ref[...], preferred_element_type=F32)
    km = kv[:, :MX_W].astype(BF16)
    vm = kv[:, MX_W:].astype(BF16)
    lane = lax.broadcasted_iota(jnp.int32, (MX_ROWS, MX_W), 1)

    def chunk(c, _):
        r0 = pl.multiple_of(c * MX_ROWS, MX_ROWS)
        q = q_ref[pl.ds(r0, MX_ROWS), :]
        out = jnp.zeros((MX_ROWS, MX_W), F32)
        for h in range(MX_W // HEAD_DIM):
            head = (lane >= HEAD_DIM * h) & (lane < HEAD_DIM * (h + 1))
            qh = jnp.where(head, q, jnp.zeros_like(q))
            s = lax.dot_general(qh, km, (((1,), (1,)), ((), ())),
                                preferred_element_type=F32) * scale
            p = jnp.exp(s - jnp.max(s, axis=-1, keepdims=True))
            p = p / jnp.sum(p, axis=-1, keepdims=True)
            oh = jnp.dot(p.astype(BF16), vm, preferred_element_type=F32)
            out = jnp.where(head, oh, out)
        o_ref[pl.ds(r0, MX_ROWS), :] = out.astype(BF16)
        return 0

    lax.fori_loop(0, seq // MX_ROWS, chunk, 0)


def _memx(mem, g_mem, w_kv_bf, proj3):
    B, S, _ = proj3.shape
    q_block = (3 * SB_W + 2 * CV_W) // MX_W
    return pl.pallas_call(
        functools.partial(_memx_kernel, seq=S),
        grid=(B,),
        in_specs=[
            pl.BlockSpec((None, N_MEM, D_MODEL), lambda b: (b, 0, 0)),
            pl.BlockSpec((1, D_MODEL), lambda b: (0, 0)),
            pl.BlockSpec((D_MODEL, 2 * MX_W), lambda b: (0, 0)),
            pl.BlockSpec((None, S, MX_W), lambda b: (b, 0, q_block)),
        ],
        out_specs=pl.BlockSpec((None, S, MX_W), lambda b: (b, 0, 0)),
        out_shape=jax.ShapeDtypeStruct((B, S, MX_W), BF16),
        compiler_params=pltpu.CompilerParams(
            dimension_semantics=("arbitrary",), vmem_limit_bytes=VMEM_LIMIT),
        name="memx",
    )(mem, g_mem, w_kv_bf, proj3)


def _outproj_kernel(sb_ref, cv_ref, mx_ref, x_ref, gsb_ref, gcv_ref, gmx_ref, wo_ref,
                    gffn_ref, wr_ref, br_ref, h_ref, hn3_ref, idx_ref, rank_ref, gate_ref,
                    cnt_ref, run_ref):
    def normed(o_ref, g_ref):
        return _rms(o_ref[...].astype(F32), g_ref[...]).astype(BF16)

    mix = jnp.dot(normed(sb_ref, gsb_ref), wo_ref[0:SB_W, :], preferred_element_type=F32)
    mix += jnp.dot(normed(cv_ref, gcv_ref), wo_ref[SB_W:SB_W + CV_W, :],
                   preferred_element_type=F32)
    mix += jnp.dot(normed(mx_ref, gmx_ref), wo_ref[SB_W + CV_W:, :],
                   preferred_element_type=F32)
    h = x_ref[...] + mix
    h_ref[...] = h
    hn = _rms(h, gffn_ref[...])
    packed = _pack_halves(hn)
    for s in range(PCHUNKS):
        hn3_ref[pl.ds(s, ROW_TILE, stride=PCHUNKS), :] = packed[:, s * LANES:(s + 1) * LANES]
    wr = wr_ref[...]
    hn_hi = hn.astype(BF16)
    hn_lo = (hn - hn_hi.astype(F32)).astype(BF16)
    wr_hi = wr.astype(BF16)
    wr_lo = (wr - wr_hi.astype(F32)).astype(BF16)
    logits = (jnp.dot(hn_hi, wr_hi, preferred_element_type=F32)
              + jnp.dot(hn_lo, wr_hi, preferred_element_type=F32)
              + jnp.dot(hn_hi, wr_lo, preferred_element_type=F32)) + br_ref[...]
    eid = lax.broadcasted_iota(jnp.int32, logits.shape, 1)
    vals, idxs = [], []
    for _ in range(TOP_K):
        m = jnp.max(logits, axis=-1, keepdims=True)
        i = jnp.min(jnp.where(logits == m, eid, N_EXPERTS), axis=-1, keepdims=True)
        vals.append(m)
        idxs.append(i)
        logits = jnp.where(eid == i, -jnp.inf, logits)
    es = [jnp.exp(v - vals[0]) for v in vals]
    denom = es[0] + es[1] + es[2] + es[3]

    @pl.when(pl.program_id(0) == 0)
    def _():
        run_ref[...] = jnp.zeros_like(run_ref)
    tm = logits.shape[0]
    row = lax.broadcasted_iota(jnp.int32, (tm, tm), 0)
    col = lax.broadcasted_iota(jnp.int32, (tm, tm), 1)
    before = jnp.where(col < row, 1.0, 0.0).astype(BF16)
    base = run_ref[...]
    ranks = []
    for k in range(TOP_K):
        onehot = jnp.where(eid == idxs[k], 1.0, 0.0)
        prefix = jnp.dot(before, onehot.astype(BF16), preferred_element_type=F32)
        ranks.append(jnp.sum(onehot * (prefix + base), axis=-1, keepdims=True))
        base = base + jnp.sum(onehot, axis=0, keepdims=True)
    run_ref[...] = base
    cnt_ref[...] = jnp.broadcast_to(base, cnt_ref.shape)

    lane = lax.broadcasted_iota(jnp.int32, idx_ref.shape, 1)
    idx_out = jnp.zeros(idx_ref.shape, jnp.int32)
    rank_out = jnp.zeros(rank_ref.shape, jnp.int32)
    gate_out = jnp.zeros(gate_ref.shape, F32)
    for k in range(TOP_K):
        idx_out = jnp.where(lane == k, idxs[k], idx_out)
        rank_out = jnp.where(lane == k, ranks[k].astype(jnp.int32), rank_out)
        gate_out = jnp.where(lane == k, es[k] / denom, gate_out)
    idx_ref[...] = idx_out
    rank_ref[...] = rank_out
    gate_ref[...] = gate_out


def _outproj(o_sb, o_cv, o_mx, x2, g_sb, g_cv, g_mx, w_out_bf, g_ffn, w_router, b_router):
    T = x2.shape[0]
    rows = lambda n: pl.BlockSpec((ROW_TILE, n), lambda i: (i, 0))
    full = lambda a, b: pl.BlockSpec((a, b), lambda i: (0, 0))
    return pl.pallas_call(
        _outproj_kernel,
        grid=(T // ROW_TILE,),
        in_specs=[
            rows(SB_W), rows(CV_W), rows(MX_W), rows(D_MODEL),
            full(1, SB_W), full(1, CV_W), full(1, MX_W),
            full(D_MODEL, D_MODEL), full(1, D_MODEL),
            full(D_MODEL, N_EXPERTS), full(1, N_EXPERTS),
        ],
        out_specs=[rows(D_MODEL),
                   pl.BlockSpec((ROW_TILE * PCHUNKS, LANES), lambda i: (i, 0)),
                   rows(LANES), rows(LANES), rows(LANES),
                   full(SUBLANES, N_EXPERTS)],
        out_shape=[
            jax.ShapeDtypeStruct((T, D_MODEL), F32),
            jax.ShapeDtypeStruct((T * PCHUNKS, LANES), U32),
            jax.ShapeDtypeStruct((T, LANES), jnp.int32),
            jax.ShapeDtypeStruct((T, LANES), jnp.int32),
            jax.ShapeDtypeStruct((T, LANES), F32),
            jax.ShapeDtypeStruct((SUBLANES, N_EXPERTS), F32),
        ],
        scratch_shapes=[pltpu.VMEM((1, N_EXPERTS), F32)],
        compiler_params=pltpu.CompilerParams(
            dimension_semantics=("arbitrary",), vmem_limit_bytes=VMEM_LIMIT),
        name="outproj_router",
    )(o_sb, o_cv, o_mx, x2, g_sb, g_cv, g_mx, w_out_bf, g_ffn, w_router, b_router)


def _rows_copy(src_ref, src_row, dst_ref, dst_row, n, sem):
    src = src_ref.at[pl.ds(pl.multiple_of(src_row * PCHUNKS, PCHUNKS), n * PCHUNKS), :]
    dst = dst_ref.at[pl.ds(pl.multiple_of(dst_row * PCHUNKS, PCHUNKS), n * PCHUNKS), :]
    return pltpu.make_async_copy(src, dst, sem)


def _dispatch_kernel(fill_start_ref, fill_rows_ref, dest_ref, hn3_ref, xs3_ref,
                     tile_ref, zero_ref, sem_ref, load_sem_ref, fill_sem_ref, *, tm):
    i = pl.program_id(0)
    n = pl.num_programs(0)
    slot = lax.rem(i, 3)
    slot_next = lax.rem(i + 1, 3)
    n_copies = tm * TOP_K
    pieces = [1 << p for p in range(MOE_BM.bit_length())]

    def fill(wait):
        for e in range(2 * N_EXPERTS):
            rows = fill_rows_ref[e]
            start = fill_start_ref[e]
            for p in pieces:
                @pl.when((rows & p) != 0)
                def _(p=p, start=start, rows=rows):
                    cp = _rows_copy(zero_ref, 0, xs3_ref, start + (rows & (p - 1)), p,
                                    fill_sem_ref.at[0])
                    if wait:
                        cp.wait()
                    else:
                        cp.start()

    @pl.when(i == 0)
    def _():
        zero_ref[...] = jnp.zeros_like(zero_ref)
        fill(wait=False)

    def tile_load(j, s):
        return _rows_copy(hn3_ref, j * tm, tile_ref.at[s], 0, tm, load_sem_ref.at[s])

    def rows_wait(s):
        for _ in range(TOP_K):
            _rows_copy(tile_ref.at[s], 0, xs3_ref, 0, tm, sem_ref.at[s]).wait()

    @pl.when(i == 0)
    def _():
        tile_load(0, 0).start()

    @pl.when(i >= 2)
    def _():
        rows_wait(slot_next)

    @pl.when(i + 1 < n)
    def _():
        tile_load(i + 1, slot_next).start()

    tile_load(i, slot).wait()

    def group(g, _):
        m0 = pl.multiple_of(g * DMA_UNROLL, DMA_UNROLL)
        dests = [dest_ref[0, 0, m0 + u] for u in range(DMA_UNROLL)]
        for u in range(DMA_UNROLL):
            r = g * (DMA_UNROLL // TOP_K) + u // TOP_K
            _rows_copy(tile_ref.at[slot], r, xs3_ref, dests[u], 1,
                       sem_ref.at[slot]).start(priority=u % DMA_PRIORITIES)
        return 0
    lax.fori_loop(0, n_copies // DMA_UNROLL, group, 0)

    @pl.when(i == 0)
    def _():
        fill(wait=True)

    @pl.when(i == n - 1)
    def _():
        @pl.when(i >= 1)
        def _():
            rows_wait(lax.rem(i + 2, 3))
        rows_wait(slot)


def _dispatch(fill_start, fill_rows, dest, hn3, n_blocks):
    T = hn3.shape[0] // PCHUNKS
    tm = ROW_TILE
    grid_spec = pltpu.PrefetchScalarGridSpec(
        num_scalar_prefetch=2,
        grid=(T // tm,),
        in_specs=[
            pl.BlockSpec((1, 1, tm * TOP_K), lambda i, fs, fr: (i, 0, 0),
                         memory_space=pltpu.SMEM),
            pl.BlockSpec(memory_space=pl.ANY),
        ],
        out_specs=pl.BlockSpec(memory_space=pl.ANY),
        scratch_shapes=[
            pltpu.VMEM((3, tm * PCHUNKS, LANES), U32),
            pltpu.VMEM((MOE_BM * PCHUNKS, LANES), U32),
            pltpu.SemaphoreType.DMA((3,)),
            pltpu.SemaphoreType.DMA((3,)),
            pltpu.SemaphoreType.DMA((1,)),
        ],
    )
    return pl.pallas_call(
        functools.partial(_dispatch_kernel, tm=tm),
        grid_spec=grid_spec,
        out_shape=jax.ShapeDtypeStruct((n_blocks * MOE_BM * PCHUNKS, LANES), U32),
        compiler_params=pltpu.CompilerParams(
            dimension_semantics=("arbitrary",), vmem_limit_bytes=VMEM_LIMIT),
        name="dispatch",
    )(fill_start, fill_rows, dest.reshape(T // tm, 1, tm * TOP_K), hn3)


def _experts_kernel(bexp_ref, nused_ref, x_ref, wgu_ref, bgu_ref, wd_ref, bd_ref, y_ref,
                    hm_ref, wgu_bf_ref, wd_bf_ref):
    b = pl.program_id(0)

    @pl.when(b >= nused_ref[0])
    def _():
        y_ref[...] = jnp.zeros_like(y_ref)

    @pl.when(b < nused_ref[0])
    def _():
        prev = jnp.maximum(b - 1, 0)
        @pl.when((b == 0) | (bexp_ref[b] != bexp_ref[prev]))
        def _():
            wgu_bf_ref[...] = wgu_ref[...].astype(BF16)
            wd_bf_ref[...] = wd_ref[...].astype(BF16)

        halves = [_unpack_halves(x_ref[pl.ds(s, MOE_BM, stride=PCHUNKS), :])
                  for s in range(PCHUNKS)]
        x = jnp.concatenate([lo for lo, _ in halves] + [hi for _, hi in halves],
                            axis=1).astype(BF16)

        for c0 in range(0, D_FF, FF_CHUNK):
            g_cols = slice(c0, c0 + FF_CHUNK)
            u_cols = slice(D_FF + c0, D_FF + c0 + FF_CHUNK)
            gate = jnp.dot(x, wgu_bf_ref[:, g_cols], preferred_element_type=F32) + bgu_ref[:, g_cols]
            up = jnp.dot(x, wgu_bf_ref[:, u_cols], preferred_element_type=F32) + bgu_ref[:, u_cols]
            gate = jnp.minimum(gate, SWIGLU_LIMIT)
            up = jnp.clip(up, -SWIGLU_LIMIT, SWIGLU_LIMIT)
            hm_ref[:, g_cols] = ((up + 1.0) * (gate * jax.nn.sigmoid(SWIGLU_ALPHA * gate))
                                 ).astype(BF16)
        y = jnp.dot(hm_ref[...], wd_bf_ref[...], preferred_element_type=F32) + bd_ref[...]
        packed = _pack_halves(y)
        for s in range(PCHUNKS):
            y_ref[pl.ds(s, MOE_BM, stride=PCHUNKS), :] = packed[:, s * LANES:(s + 1) * LANES]


def _experts(block_exp, n_used, xs3, w_gu, b_gu, w_down, b_down):
    NB = block_exp.shape[0]
    used = lambda b, nu: jnp.minimum(b, nu[0] - 1)
    grid_spec = pltpu.PrefetchScalarGridSpec(
        num_scalar_prefetch=2,
        grid=(NB,),
        in_specs=[
            pl.BlockSpec((MOE_BM * PCHUNKS, LANES), lambda b, be, nu: (used(b, nu), 0)),
            pl.BlockSpec((None, D_MODEL, 2 * D_FF), lambda b, be, nu: (be[b], 0, 0)),
            pl.BlockSpec((None, 1, 2 * D_FF), lambda b, be, nu: (be[b], 0, 0)),
            pl.BlockSpec((None, D_FF, D_MODEL), lambda b, be, nu: (be[b], 0, 0)),
            pl.BlockSpec((None, 1, D_MODEL), lambda b, be, nu: (be[b], 0, 0)),
        ],
        out_specs=pl.BlockSpec((MOE_BM * PCHUNKS, LANES), lambda b, be, nu: (b, 0)),
        scratch_shapes=[
            pltpu.VMEM((MOE_BM, D_FF), BF16),
            pltpu.VMEM((D_MODEL, 2 * D_FF), BF16),
            pltpu.VMEM((D_FF, D_MODEL), BF16),
        ],
    )
    return pl.pallas_call(
        _experts_kernel,
        grid_spec=grid_spec,
        out_shape=jax.ShapeDtypeStruct((NB * MOE_BM * PCHUNKS, LANES), U32),
        compiler_params=pltpu.CompilerParams(
            dimension_semantics=("arbitrary",), vmem_limit_bytes=VMEM_LIMIT),
        name="experts",
    )(block_exp, n_used, xs3, w_gu, b_gu, w_down, b_down)


def _routing_tables(counts, top_idx, rank):
    M = top_idx.shape[0] * TOP_K
    NB = -(-M // MOE_BM) + N_EXPERTS
    nblk_e = (counts + MOE_BM - 1) // MOE_BM
    blk_end = jnp.cumsum(nblk_e)
    row_start = (blk_end - nblk_e) * MOE_BM
    n_used = blk_end[-1]
    blk = jnp.minimum(jnp.arange(NB, dtype=jnp.int32), n_used - 1)
    bexp = jnp.sum((blk[:, None] >= blk_end[None, :]).astype(jnp.int32), axis=1)
    experts = jnp.arange(N_EXPERTS, dtype=jnp.int32)
    dest = rank + jnp.sum(jnp.where(top_idx[:, :, None] == experts, row_start, 0), axis=-1)
    tail = NB - N_EXPERTS + experts
    fill_start = jnp.concatenate([row_start + counts, tail * MOE_BM])
    fill_rows = jnp.concatenate([nblk_e * MOE_BM - counts,
                                 jnp.where(tail >= n_used, MOE_BM, 0)])
    return (bexp.astype(jnp.int32), n_used.reshape(1).astype(jnp.int32),
            dest.astype(jnp.int32), fill_start.astype(jnp.int32),
            fill_rows.astype(jnp.int32), NB)


def _combine_kernel(dest_ref, dest_next_ref, h_ref, gate_ref, g_ref, ys3_ref, o_ref,
                    buf_ref, sem_ref, *, tm):
    i = pl.program_id(0)
    n = pl.num_programs(0)
    slot = lax.rem(i, 2)

    def start_gather(idx_ref, s):
        def group(g, _):
            m0 = pl.multiple_of(g * DMA_UNROLL, DMA_UNROLL)
            srcs = [idx_ref[0, 0, m0 + u] for u in range(DMA_UNROLL)]
            for u in range(DMA_UNROLL):
                r = g * (DMA_UNROLL // TOP_K) + u // TOP_K
                _rows_copy(ys3_ref, srcs[u], buf_ref.at[s], (u % TOP_K) * tm + r, 1,
                           sem_ref.at[s]).start(priority=u % DMA_PRIORITIES)
            return 0
        lax.fori_loop(0, tm * TOP_K // DMA_UNROLL, group, 0)

    @pl.when(i == 0)
    def _():
        start_gather(dest_ref, 0)

    @pl.when(i + 1 < n)
    def _():
        start_gather(dest_next_ref, 1 - slot)

    _rows_copy(ys3_ref, 0, buf_ref.at[slot], 0, tm * TOP_K, sem_ref.at[slot]).wait()

    gates = gate_ref[...]
    chunks = [None] * CHUNKS
    ssq = jnp.zeros((tm, 1), F32)
    for s in range(PCHUNKS):
        acc_lo = h_ref[:, s * LANES:(s + 1) * LANES]
        acc_hi = h_ref[:, HALF + s * LANES:HALF + (s + 1) * LANES]
        for k in range(TOP_K):
            lo, hi = _unpack_halves(
                buf_ref[slot, pl.ds(k * tm * PCHUNKS + s, tm, stride=PCHUNKS), :])
            acc_lo = acc_lo + gates[:, k:k + 1] * lo
            acc_hi = acc_hi + gates[:, k:k + 1] * hi
        chunks[s], chunks[PCHUNKS + s] = acc_lo, acc_hi
        ssq = ssq + jnp.sum(acc_lo * acc_lo + acc_hi * acc_hi, axis=-1, keepdims=True)
    inv = lax.rsqrt(ssq * (1.0 / D_MODEL) + EPS)
    for s in range(CHUNKS):
        o_ref[:, s * LANES:(s + 1) * LANES] = chunks[s] * inv * g_ref[:, s * LANES:(s + 1) * LANES]


def _combine(h, ys3, dest, gates, g_final):
    T = h.shape[0]
    tm = COMBINE_TILE
    n = T // tm
    table = dest.reshape(n, 1, tm * TOP_K)
    idx_spec = lambda f: pl.BlockSpec((1, 1, tm * TOP_K), lambda i: (f(i), 0, 0),
                                      memory_space=pltpu.SMEM)
    return pl.pallas_call(
        functools.partial(_combine_kernel, tm=tm),
        grid=(n,),
        in_specs=[
            idx_spec(lambda i: i), idx_spec(lambda i: jnp.minimum(i + 1, n - 1)),
            pl.BlockSpec((tm, D_MODEL), lambda i: (i, 0)),
            pl.BlockSpec((tm, LANES), lambda i: (i, 0)),
            pl.BlockSpec((1, D_MODEL), lambda i: (0, 0)),
            pl.BlockSpec(memory_space=pl.ANY),
        ],
        out_specs=pl.BlockSpec((tm, D_MODEL), lambda i: (i, 0)),
        out_shape=jax.ShapeDtypeStruct((T, D_MODEL), F32),
        scratch_shapes=[
            pltpu.VMEM((2, TOP_K * tm * PCHUNKS, LANES), U32),
            pltpu.SemaphoreType.DMA((2,)),
        ],
        compiler_params=pltpu.CompilerParams(
            dimension_semantics=("arbitrary",), vmem_limit_bytes=VMEM_LIMIT),
        name="combine",
    )(table, table, h, gates, g_final, ys3)


def kernel(x, mem, g_attn_norm, w_in, b_glu, w_dw, b_dw, g_cv_ln, b_cv_ln, w_pw2, b_pw2, g_mem, w_mem_kv, g_sb_out, g_cv_out, g_mx_out, w_out, g_ffn_norm, w_router, b_router, w_gu, b_gu, w_down, b_down, g_final):
    B, S, D = x.shape
    T = B * S
    assert D == D_MODEL and S % ROW_TILE == 0 and g_attn_norm.shape[0] == 1
    l = 0
    row = lambda v: v.reshape(1, -1)
    x2 = x.reshape(T, D)

    proj = _inproj(x2, row(g_attn_norm[l]), w_in[l].astype(BF16))
    proj3 = proj.reshape(B, S, IN_W)
    o_sb = _sb_attention(proj3)
    o_cv = _conformer(proj3, row(b_glu[l]), w_dw[l].reshape(CONV_K, CV_W), row(b_dw[l]),
                      row(g_cv_ln[l]), row(b_cv_ln[l]), w_pw2[l].astype(BF16), row(b_pw2[l]))
    o_mx = _memx(mem, row(g_mem[l]), w_mem_kv[l].astype(BF16), proj3)

    h, hn3, idx_pad, rank_pad, gate_pad, counts = _outproj(
        o_sb.reshape(T, SB_W), o_cv.reshape(T, CV_W), o_mx.reshape(T, MX_W), x2,
        row(g_sb_out[l]), row(g_cv_out[l]), row(g_mx_out[l]), w_out[l].astype(BF16),
        row(g_ffn_norm[l]), w_router[l], row(b_router[l]))

    bexp, n_used, dest, fill_start, fill_rows, n_rows = _routing_tables(
        counts[0].astype(jnp.int32), idx_pad[:, :TOP_K], rank_pad[:, :TOP_K])
    xs3 = _dispatch(fill_start, fill_rows, dest, hn3, n_rows)
    ys3 = _experts(bexp, n_used, xs3, w_gu[l], b_gu[l].reshape(N_EXPERTS, 1, 2 * D_FF),
                   w_down[l], b_down[l].reshape(N_EXPERTS, 1, D_MODEL))
    out = _combine(h, ys3, dest, gate_pad, row(g_final))
    return out.reshape(B, S, D)
```

```python
import functools

import jax
import jax.numpy as jnp
from jax import lax
from jax.experimental import pallas as pl
from jax.experimental.pallas import tpu as pltpu

F32 = jnp.float32
BF16 = jnp.bfloat16

D_MODEL = 1024
HEAD_DIM = 64
SB_W = 512
CV_W = 256
MX_W = 256
IN_W = 3 * SB_W + 2 * CV_W + MX_W
CONV_K = 31
N_MEM = 256
N_EXPERTS = 32
TOP_K = 4
D_FF = 1024
SWIGLU_ALPHA = 1.702
SWIGLU_LIMIT = 7.0
EPS = 1e-6

LANES = 128
SUBLANES = 8
CHUNKS = D_MODEL // LANES
HALF = D_MODEL // 2
PCHUNKS = HALF // LANES
U32 = jnp.uint32
ROW_TILE = 512
SB_TILE = 256
SB_PAIRS = 4
SB_EXP_ZERO_BELOW = -104.0
CONV_PAD = 32
CONV_ROWS = 128
MX_ROWS = 512
MOE_BM = 512
FF_CHUNK = 512
COMBINE_TILE = 512
COMBINE_ROWS = 64
DMA_UNROLL = 16
DMA_PRIORITIES = 2
VMEM_LIMIT = 56 * 1024 * 1024


def _rms(x, g):
    return x * lax.rsqrt(jnp.mean(x * x, axis=-1, keepdims=True) + EPS) * g


def _pack_halves(v):
    bits = pltpu.bitcast(v.astype(BF16).astype(F32), U32)
    return lax.shift_right_logical(bits[:, :HALF], U32(16)) | bits[:, HALF:]


def _unpack_halves(w):
    low = pltpu.bitcast(lax.shift_left(w, U32(16)), F32)
    high = pltpu.bitcast(w & U32(0xFFFF0000), F32)
    return low, high


def _inproj_kernel(x_ref, g_ref, w_ref, o_ref):
    xn = _rms(x_ref[...], g_ref[...]).astype(BF16)
    o_ref[...] = jnp.dot(xn, w_ref[...], preferred_element_type=F32).astype(BF16)


def _inproj(x2, g, w_bf):
    T = x2.shape[0]
    return pl.pallas_call(
        _inproj_kernel,
        grid=(T // ROW_TILE,),
        in_specs=[
            pl.BlockSpec((ROW_TILE, D_MODEL), lambda i: (i, 0)),
            pl.BlockSpec((1, D_MODEL), lambda i: (0, 0)),
            pl.BlockSpec((D_MODEL, IN_W), lambda i: (0, 0)),
        ],
        out_specs=pl.BlockSpec((ROW_TILE, IN_W), lambda i: (i, 0)),
        out_shape=jax.ShapeDtypeStruct((T, IN_W), BF16),
        compiler_params=pltpu.CompilerParams(
            dimension_semantics=("arbitrary",), vmem_limit_bytes=VMEM_LIMIT),
        name="inproj",
    )(x2, g, w_bf)


def _sb_kernel(q_ref, k_ref, v_ref, o_ref, *, seq):
    lane = lax.broadcasted_iota(jnp.int32, (SB_TILE, LANES), 1)
    head0 = lane < HEAD_DIM
    row = lax.broadcasted_iota(jnp.int32, (SB_TILE, SB_TILE), 0)
    col = lax.broadcasted_iota(jnp.int32, (SB_TILE, SB_TILE), 1)
    tri = jnp.where(row > col, 1.0, 0.0).astype(BF16)
    dmask = col < row

    def tile(head, qh, s0, c, diag):
        lanes = slice((head // 2) * LANES, (head // 2 + 1) * LANES)
        kb = k_ref[pl.ds(s0, SB_TILE), lanes]
        z = lax.dot_general(qh, kb, (((1,), (1,)), ((), ())), preferred_element_type=F32)
        lb = jnp.minimum(z, 0.0) - jnp.log(1.0 + jnp.exp(-jnp.abs(z)))
        l1m = lb - z
        if diag:
            l1m = jnp.where(dmask, l1m, 0.0)
        after = jnp.dot(l1m.astype(BF16), tri, preferred_element_type=F32)
        rowsum = jnp.broadcast_to(jnp.sum(l1m, axis=1, keepdims=True), (SB_TILE, LANES))
        arg = lb + after
        if c is not None:
            arg = arg + jnp.concatenate([c] * (SB_TILE // LANES), axis=1)
        a = jnp.exp(arg)
        if diag:
            a = jnp.where(dmask, a, 0.0)
        pv = jnp.dot(a.astype(BF16), v_ref[pl.ds(s0, SB_TILE), lanes],
                     preferred_element_type=F32)
        return pv, rowsum

    n_heads = 2 * SB_PAIRS

    def alive(cs):
        top = cs[0]
        for c in cs[1:]:
            top = jnp.maximum(top, c)
        return jnp.max(top) >= SB_EXP_ZERO_BELOW

    def heads(t0):
        out = []
        for p in range(SB_PAIRS):
            q = q_ref[pl.ds(t0, SB_TILE), p * LANES:(p + 1) * LANES] * (HEAD_DIM ** -0.5)
            out += [jnp.where(head0, q, jnp.zeros_like(q)), jnp.where(head0, jnp.zeros_like(q), q)]
        return out

    def store(t0, accs):
        for p in range(SB_PAIRS):
            o_ref[pl.ds(t0, SB_TILE), p * LANES:(p + 1) * LANES] = jnp.where(
                head0, accs[2 * p], accs[2 * p + 1]).astype(BF16)

    qs = heads(0)
    store(0, [tile(h, qs[h], 0, None, True)[0] for h in range(n_heads)])

    def qtile(i, _):
        t0 = pl.multiple_of(i * SB_TILE, SB_TILE)
        s0 = pl.multiple_of((i - 1) * SB_TILE, SB_TILE)
        qs = heads(t0)
        diag = [tile(h, qs[h], t0, None, True) for h in range(n_heads)]
        prev = [tile(h, qs[h], s0, diag[h][1], False) for h in range(n_heads)]
        accs = tuple(d[0] + p[0] for d, p in zip(diag, prev))
        cs = tuple(d[1] + p[1] for d, p in zip(diag, prev))

        def cond(st):
            return (st[0] >= 0) & st[1]

        def body(st):
            j, _, accs, cs = st
            s0 = pl.multiple_of(j * SB_TILE, SB_TILE)
            new = [tile(h, qs[h], s0, cs[h], False) for h in range(n_heads)]
            accs = tuple(a + t[0] for a, t in zip(accs, new))
            cs = tuple(c + t[1] for c, t in zip(cs, new))
            return j - 1, alive(cs), accs, cs

        st = lax.while_loop(cond, body, (i - 2, alive(cs), accs, cs))
        store(t0, st[2])
        return 0

    lax.fori_loop(1, seq // SB_TILE, qtile, 0)


def _sb_attention(proj3):
    B, S, _ = proj3.shape
    width = SB_PAIRS * LANES
    groups = SB_W // width
    return pl.pallas_call(
        functools.partial(_sb_kernel, seq=S),
        grid=(B, groups),
        in_specs=[
            pl.BlockSpec((None, S, width), lambda b, p: (b, 0, p)),
            pl.BlockSpec((None, S, width), lambda b, p: (b, 0, groups + p)),
            pl.BlockSpec((None, S, width), lambda b, p: (b, 0, 2 * groups + p)),
        ],
        out_specs=pl.BlockSpec((None, S, width), lambda b, p: (b, 0, p)),
        out_shape=jax.ShapeDtypeStruct((B, S, SB_W), BF16),
        compiler_params=pltpu.CompilerParams(
            dimension_semantics=("arbitrary", "arbitrary"), vmem_limit_bytes=VMEM_LIMIT),
        name="sb_attention",
    )(proj3, proj3, proj3)


def _conv_kernel(glu_ref, bglu_ref, wdw_ref, bdw_ref, gln_ref, bln_ref, wpw_ref, bpw_ref,
                 o_ref, upad_ref, shift_ref, *, seq):
    upad_ref[0:CONV_PAD, :] = jnp.zeros((CONV_PAD, CV_W), F32)
    for c in range(seq // CONV_ROWS):
        r0 = c * CONV_ROWS
        g = glu_ref[r0:r0 + CONV_ROWS, :].astype(F32) + bglu_ref[...]
        upad_ref[CONV_PAD + r0:CONV_PAD + r0 + CONV_ROWS, :] = (
            g[:, :CV_W] * jax.nn.sigmoid(g[:, CV_W:]))
    for c in range(seq // CONV_ROWS):
        r0 = c * CONV_ROWS
        acc = jnp.zeros((CONV_ROWS, CV_W), F32) + bdw_ref[...]
        span = CONV_ROWS + CONV_PAD - SUBLANES
        for r in range(1, SUBLANES):
            shift_ref[r, 0:span, :] = upad_ref[r0 + r:r0 + r + span, :]
        for k in range(CONV_K):
            off = CONV_PAD - (CONV_K - 1) + k
            r = off % SUBLANES
            a = off - r
            if r == 0:
                tap = upad_ref[r0 + a:r0 + a + CONV_ROWS, :]
            else:
                tap = shift_ref[r, a:a + CONV_ROWS, :]
            acc = acc + tap * wdw_ref[k:k + 1, :]
        mu = jnp.mean(acc, axis=-1, keepdims=True)
        d = acc - mu
        var = jnp.mean(d * d, axis=-1, keepdims=True)
        y = d * lax.rsqrt(var + EPS) * gln_ref[...] + bln_ref[...]
        y = y * jax.nn.sigmoid(y)
        out = jnp.dot(y.astype(BF16), wpw_ref[...], preferred_element_type=F32) + bpw_ref[...]
        o_ref[r0:r0 + CONV_ROWS, :] = out.astype(BF16)


def _conformer(proj3, b_glu, w_dw, b_dw, g_ln, b_ln, w_pw_bf, b_pw):
    B, S, _ = proj3.shape
    glu_block = (3 * SB_W) // (2 * CV_W)
    vec = lambda n: pl.BlockSpec((1, n), lambda b: (0, 0))
    return pl.pallas_call(
        functools.partial(_conv_kernel, seq=S),
        grid=(B,),
        in_specs=[
            pl.BlockSpec((None, S, 2 * CV_W), lambda b: (b, 0, glu_block)),
            vec(2 * CV_W),
            pl.BlockSpec((CONV_K, CV_W), lambda b: (0, 0)),
            vec(CV_W), vec(CV_W), vec(CV_W),
            pl.BlockSpec((CV_W, CV_W), lambda b: (0, 0)),
            vec(CV_W),
        ],
        out_specs=pl.BlockSpec((None, S, CV_W), lambda b: (b, 0, 0)),
        out_shape=jax.ShapeDtypeStruct((B, S, CV_W), BF16),
        scratch_shapes=[pltpu.VMEM((CONV_PAD + S, CV_W), F32),
                        pltpu.VMEM((SUBLANES, CONV_ROWS + CONV_PAD, CV_W), F32)],
        compiler_params=pltpu.CompilerParams(
            dimension_semantics=("arbitrary",), vmem_limit_bytes=VMEM_LIMIT),
        name="conformer",
    )(proj3, b_glu, w_dw, b_dw, g_ln, b_ln, w_pw_bf, b_pw)


def _memx_kernel(mem_ref, gm_ref, wkv_ref, q_ref, o_ref, *, seq):
    scale = HEAD_DIM ** -0.5
    mn = _rms(mem_ref[...], gm_ref[...]).astype(BF16)
    kv = jnp.dot(mn, wkv_ref[...], preferred_element_type=F32)
    km = kv[:, :MX_W].astype(BF16)
    vm = kv[:, MX_W:].astype(BF16)
    lane = lax.broadcasted_iota(jnp.int32, (MX_ROWS, MX_W), 1)

    def chunk(c, _):
        r0 = pl.multiple_of(c * MX_ROWS, MX_ROWS)
        q = q_ref[pl.ds(r0, MX_ROWS), :]
        out = jnp.zeros((MX_ROWS, MX_W), F32)
        for h in range(MX_W // HEAD_DIM):
            head = (lane >= HEAD_DIM * h) & (lane < HEAD_DIM * (h + 1))
            qh = jnp.where(head, q, jnp.zeros_like(q))
            s = lax.dot_general(qh, km, (((1,), (1,)), ((), ())),
                                preferred_element_type=F32) * scale
            p = jnp.exp(s - jnp.max(s, axis=-1, keepdims=True))
            p = p / jnp.sum(p, axis=-1, keepdims=True)
            oh = jnp.dot(p.astype(BF16), vm, preferred_element_type=F32)
            out = jnp.where(head, oh, out)
        o_ref[pl.ds(r0, MX_ROWS), :] = out.astype(BF16)
        return 0

    lax.fori_loop(0, seq // MX_ROWS, chunk, 0)


def _memx(mem, g_mem, w_kv_bf, proj3):
    B, S, _ = proj3.shape
    q_block = (3 * SB_W + 2 * CV_W) // MX_W
    return pl.pallas_call(
        functools.partial(_memx_kernel, seq=S),
        grid=(B,),
        in_specs=[
            pl.BlockSpec((None, N_MEM, D_MODEL), lambda b: (b, 0, 0)),
            pl.BlockSpec((1, D_MODEL), lambda b: (0, 0)),
            pl.BlockSpec((D_MODEL, 2 * MX_W), lambda b: (0, 0)),
            pl.BlockSpec((None, S, MX_W), lambda b: (b, 0, q_block)),
        ],
        out_specs=pl.BlockSpec((None, S, MX_W), lambda b: (b, 0, 0)),
        out_shape=jax.ShapeDtypeStruct((B, S, MX_W), BF16),
        compiler_params=pltpu.CompilerParams(
            dimension_semantics=("arbitrary",), vmem_limit_bytes=VMEM_LIMIT),
        name="memx",
    )(mem, g_mem, w_kv_bf, proj3)


def _outproj_kernel(sb_ref, cv_ref, mx_ref, x_ref, gsb_ref, gcv_ref, gmx_ref, wo_ref,
                    gffn_ref, wr_ref, br_ref, h_ref, hn3_ref, idx_ref, rank_ref, gate_ref,
                    cnt_ref, run_ref):
    def normed(o_ref, g_ref):
        return _rms(o_ref[...].astype(F32), g_ref[...]).astype(BF16)

    mix = jnp.dot(normed(sb_ref, gsb_ref), wo_ref[0:SB_W, :], preferred_element_type=F32)
    mix += jnp.dot(normed(cv_ref, gcv_ref), wo_ref[SB_W:SB_W + CV_W, :],
                   preferred_element_type=F32)
    mix += jnp.dot(normed(mx_ref, gmx_ref), wo_ref[SB_W + CV_W:, :],
                   preferred_element_type=F32)
    h = x_ref[...] + mix
    h_ref[...] = h
    hn = _rms(h, gffn_ref[...])
    packed = _pack_halves(hn)
    for s in range(PCHUNKS):
        hn3_ref[pl.ds(s, ROW_TILE, stride=PCHUNKS), :] = packed[:, s * LANES:(s + 1) * LANES]
    wr = wr_ref[...]
    hn_hi = hn.astype(BF16)
    hn_lo = (hn - hn_hi.astype(F32)).astype(BF16)
    wr_hi = wr.astype(BF16)
    wr_lo = (wr - wr_hi.astype(F32)).astype(BF16)
    logits = (jnp.dot(hn_hi, wr_hi, preferred_element_type=F32)
              + jnp.dot(hn_lo, wr_hi, preferred_element_type=F32)
              + jnp.dot(hn_hi, wr_lo, preferred_element_type=F32)) + br_ref[...]
    eid = lax.broadcasted_iota(jnp.int32, logits.shape, 1)
    vals, idxs = [], []
    for _ in range(TOP_K):
        m = jnp.max(logits, axis=-1, keepdims=True)
        i = jnp.min(jnp.where(logits == m, eid, N_EXPERTS), axis=-1, keepdims=True)
        vals.append(m)
        idxs.append(i)
        logits = jnp.where(eid == i, -jnp.inf, logits)
    es = [jnp.exp(v - vals[0]) for v in vals]
    denom = es[0] + es[1] + es[2] + es[3]

    @pl.when(pl.program_id(0) == 0)
    def _():
        run_ref[...] = jnp.zeros_like(run_ref)
    tm = logits.shape[0]
    row = lax.broadcasted_iota(jnp.int32, (tm, tm), 0)
    col = lax.broadcasted_iota(jnp.int32, (tm, tm), 1)
    before = jnp.where(col < row, 1.0, 0.0).astype(BF16)
    base = run_ref[...]
    ranks = []
    for k in range(TOP_K):
        onehot = jnp.where(eid == idxs[k], 1.0, 0.0)
        prefix = jnp.dot(before, onehot.astype(BF16), preferred_element_type=F32)
        ranks.append(jnp.sum(onehot * (prefix + base), axis=-1, keepdims=True))
        base = base + jnp.sum(onehot, axis=0, keepdims=True)
    run_ref[...] = base
    cnt_ref[...] = jnp.broadcast_to(base, cnt_ref.shape)

    lane = lax.broadcasted_iota(jnp.int32, idx_ref.shape, 1)
    idx_out = jnp.zeros(idx_ref.shape, jnp.int32)
    rank_out = jnp.zeros(rank_ref.shape, jnp.int32)
    gate_out = jnp.zeros(gate_ref.shape, F32)
    for k in range(TOP_K):
        idx_out = jnp.where(lane == k, idxs[k], idx_out)
        rank_out = jnp.where(lane == k, ranks[k].astype(jnp.int32), rank_out)
        gate_out = jnp.where(lane == k, es[k] / denom, gate_out)
    idx_ref[...] = idx_out
    rank_ref[...] = rank_out
    gate_ref[...] = gate_out


def _outproj(o_sb, o_cv, o_mx, x2, g_sb, g_cv, g_mx, w_out_bf, g_ffn, w_router, b_router):
    T = x2.shape[0]
    rows = lambda n: pl.BlockSpec((ROW_TILE, n), lambda i: (i, 0))
    full = lambda a, b: pl.BlockSpec((a, b), lambda i: (0, 0))
    return pl.pallas_call(
        _outproj_kernel,
        grid=(T // ROW_TILE,),
        in_specs=[
            rows(SB_W), rows(CV_W), rows(MX_W), rows(D_MODEL),
            full(1, SB_W), full(1, CV_W), full(1, MX_W),
            full(D_MODEL, D_MODEL), full(1, D_MODEL),
            full(D_MODEL, N_EXPERTS), full(1, N_EXPERTS),
        ],
        out_specs=[rows(D_MODEL),
                   pl.BlockSpec((ROW_TILE * PCHUNKS, LANES), lambda i: (i, 0)),
                   rows(LANES), rows(LANES), rows(LANES),
                   full(SUBLANES, N_EXPERTS)],
        out_shape=[
            jax.ShapeDtypeStruct((T, D_MODEL), F32),
            jax.ShapeDtypeStruct((T * PCHUNKS, LANES), U32),
            jax.ShapeDtypeStruct((T, LANES), jnp.int32),
            jax.ShapeDtypeStruct((T, LANES), jnp.int32),
            jax.ShapeDtypeStruct((T, LANES), F32),
            jax.ShapeDtypeStruct((SUBLANES, N_EXPERTS), F32),
        ],
        scratch_shapes=[pltpu.VMEM((1, N_EXPERTS), F32)],
        compiler_params=pltpu.CompilerParams(
            dimension_semantics=("arbitrary",), vmem_limit_bytes=VMEM_LIMIT),
        name="outproj_router",
    )(o_sb, o_cv, o_mx, x2, g_sb, g_cv, g_mx, w_out_bf, g_ffn, w_router, b_router)


def _rows_copy(src_ref, src_row, dst_ref, dst_row, n, sem):
    src = src_ref.at[pl.ds(pl.multiple_of(src_row * PCHUNKS, PCHUNKS), n * PCHUNKS), :]
    dst = dst_ref.at[pl.ds(pl.multiple_of(dst_row * PCHUNKS, PCHUNKS), n * PCHUNKS), :]
    return pltpu.make_async_copy(src, dst, sem)


def _dispatch_kernel(fill_start_ref, fill_rows_ref, dest_ref, hn3_ref, xs3_ref,
                     tile_ref, zero_ref, sem_ref, load_sem_ref, fill_sem_ref, *, tm):
    i = pl.program_id(0)
    n = pl.num_programs(0)
    slot = lax.rem(i, 3)
    slot_next = lax.rem(i + 1, 3)
    n_copies = tm * TOP_K
    pieces = [1 << p for p in range(MOE_BM.bit_length())]

    def fill(wait):
        for e in range(2 * N_EXPERTS):
            rows = fill_rows_ref[e]
            start = fill_start_ref[e]
            for p in pieces:
                @pl.when((rows & p) != 0)
                def _(p=p, start=start, rows=rows):
                    cp = _rows_copy(zero_ref, 0, xs3_ref, start + (rows & (p - 1)), p,
                                    fill_sem_ref.at[0])
                    if wait:
                        cp.wait()
                    else:
                        cp.start()

    @pl.when(i == 0)
    def _():
        zero_ref[...] = jnp.zeros_like(zero_ref)
        fill(wait=False)

    def tile_load(j, s):
        return _rows_copy(hn3_ref, j * tm, tile_ref.at[s], 0, tm, load_sem_ref.at[s])

    def rows_wait(s):
        for _ in range(TOP_K):
            _rows_copy(tile_ref.at[s], 0, xs3_ref, 0, tm, sem_ref.at[s]).wait()

    @pl.when(i == 0)
    def _():
        tile_load(0, 0).start()

    @pl.when(i >= 2)
    def _():
        rows_wait(slot_next)

    @pl.when(i + 1 < n)
    def _():
        tile_load(i + 1, slot_next).start()

    tile_load(i, slot).wait()

    def group(g, _):
        m0 = pl.multiple_of(g * DMA_UNROLL, DMA_UNROLL)
        dests = [dest_ref[0, 0, m0 + u] for u in range(DMA_UNROLL)]
        for u in range(DMA_UNROLL):
            r = g * (DMA_UNROLL // TOP_K) + u // TOP_K
            _rows_copy(tile_ref.at[slot], r, xs3_ref, dests[u], 1,
                       sem_ref.at[slot]).start(priority=u % DMA_PRIORITIES)
        return 0
    lax.fori_loop(0, n_copies // DMA_UNROLL, group, 0)

    @pl.when(i == 0)
    def _():
        fill(wait=True)

    @pl.when(i == n - 1)
    def _():
        @pl.when(i >= 1)
        def _():
            rows_wait(lax.rem(i + 2, 3))
        rows_wait(slot)


def _dispatch(fill_start, fill_rows, dest, hn3, n_blocks):
    T = hn3.shape[0] // PCHUNKS
    tm = ROW_TILE
    grid_spec = pltpu.PrefetchScalarGridSpec(
        num_scalar_prefetch=2,
        grid=(T // tm,),
        in_specs=[
            pl.BlockSpec((1, 1, tm * TOP_K), lambda i, fs, fr: (i, 0, 0),
                         memory_space=pltpu.SMEM),
            pl.BlockSpec(memory_space=pl.ANY),
        ],
        out_specs=pl.BlockSpec(memory_space=pl.ANY),
        scratch_shapes=[
            pltpu.VMEM((3, tm * PCHUNKS, LANES), U32),
            pltpu.VMEM((MOE_BM * PCHUNKS, LANES), U32),
            pltpu.SemaphoreType.DMA((3,)),
            pltpu.SemaphoreType.DMA((3,)),
            pltpu.SemaphoreType.DMA((1,)),
        ],
    )
    return pl.pallas_call(
        functools.partial(_dispatch_kernel, tm=tm),
        grid_spec=grid_spec,
        out_shape=jax.ShapeDtypeStruct((n_blocks * MOE_BM * PCHUNKS, LANES), U32),
        compiler_params=pltpu.CompilerParams(
            dimension_semantics=("arbitrary",), vmem_limit_bytes=VMEM_LIMIT),
        name="dispatch",
    )(fill_start, fill_rows, dest.reshape(T // tm, 1, tm * TOP_K), hn3)


def _experts_kernel(bexp_ref, nused_ref, x_ref, wgu_ref, bgu_ref, wd_ref, bd_ref, y_ref,
                    hm_ref, wgu_bf_ref, wd_bf_ref):
    b = pl.program_id(0)

    @pl.when(b >= nused_ref[0])
    def _():
        y_ref[...] = jnp.zeros_like(y_ref)

    @pl.when(b < nused_ref[0])
    def _():
        prev = jnp.maximum(b - 1, 0)
        @pl.when((b == 0) | (bexp_ref[b] != bexp_ref[prev]))
        def _():
            wgu_bf_ref[...] = wgu_ref[...].astype(BF16)
            wd_bf_ref[...] = wd_ref[...].astype(BF16)

        halves = [_unpack_halves(x_ref[pl.ds(s, MOE_BM, stride=PCHUNKS), :])
                  for s in range(PCHUNKS)]
        x = jnp.concatenate([lo for lo, _ in halves] + [hi for _, hi in halves],
                            axis=1).astype(BF16)

        for c0 in range(0, D_FF, FF_CHUNK):
            g_cols = slice(c0, c0 + FF_CHUNK)
            u_cols = slice(D_FF + c0, D_FF + c0 + FF_CHUNK)
            gate = jnp.dot(x, wgu_bf_ref[:, g_cols], preferred_element_type=F32) + bgu_ref[:, g_cols]
            up = jnp.dot(x, wgu_bf_ref[:, u_cols], preferred_element_type=F32) + bgu_ref[:, u_cols]
            gate = jnp.minimum(gate, SWIGLU_LIMIT)
            up = jnp.clip(up, -SWIGLU_LIMIT, SWIGLU_LIMIT)
            hm_ref[:, g_cols] = ((up + 1.0) * (gate * jax.nn.sigmoid(SWIGLU_ALPHA * gate))
                                 ).astype(BF16)
        y = jnp.dot(hm_ref[...], wd_bf_ref[...], preferred_element_type=F32) + bd_ref[...]
        packed = _pack_halves(y)
        for s in range(PCHUNKS):
            y_ref[pl.ds(s, MOE_BM, stride=PCHUNKS), :] = packed[:, s * LANES:(s + 1) * LANES]


def _experts(block_exp, n_used, xs3, w_gu, b_gu, w_down, b_down):
    NB = block_exp.shape[0]
    used = lambda b, nu: jnp.minimum(b, nu[0] - 1)
    grid_spec = pltpu.PrefetchScalarGridSpec(
        num_scalar_prefetch=2,
        grid=(NB,),
        in_specs=[
            pl.BlockSpec((MOE_BM * PCHUNKS, LANES), lambda b, be, nu: (used(b, nu), 0)),
            pl.BlockSpec((None, D_MODEL, 2 * D_FF), lambda b, be, nu: (be[b], 0, 0)),
            pl.BlockSpec((None, 1, 2 * D_FF), lambda b, be, nu: (be[b], 0, 0)),
            pl.BlockSpec((None, D_FF, D_MODEL), lambda b, be, nu: (be[b], 0, 0)),
            pl.BlockSpec((None, 1, D_MODEL), lambda b, be, nu: (be[b], 0, 0)),
        ],
        out_specs=pl.BlockSpec((MOE_BM * PCHUNKS, LANES), lambda b, be, nu: (b, 0)),
        scratch_shapes=[
            pltpu.VMEM((MOE_BM, D_FF), BF16),
            pltpu.VMEM((D_MODEL, 2 * D_FF), BF16),
            pltpu.VMEM((D_FF, D_MODEL), BF16),
        ],
    )
    return pl.pallas_call(
        _experts_kernel,
        grid_spec=grid_spec,
        out_shape=jax.ShapeDtypeStruct((NB * MOE_BM * PCHUNKS, LANES), U32),
        compiler_params=pltpu.CompilerParams(
            dimension_semantics=("arbitrary",), vmem_limit_bytes=VMEM_LIMIT),
        name="experts",
    )(block_exp, n_used, xs3, w_gu, b_gu, w_down, b_down)


def _routing_tables(counts, top_idx, rank):
    M = top_idx.shape[0] * TOP_K
    NB = -(-M // MOE_BM) + N_EXPERTS
    nblk_e = (counts + MOE_BM - 1) // MOE_BM
    blk_end = jnp.cumsum(nblk_e)
    row_start = (blk_end - nblk_e) * MOE_BM
    n_used = blk_end[-1]
    blk = jnp.minimum(jnp.arange(NB, dtype=jnp.int32), n_used - 1)
    bexp = jnp.sum((blk[:, None] >= blk_end[None, :]).astype(jnp.int32), axis=1)
    experts = jnp.arange(N_EXPERTS, dtype=jnp.int32)
    dest = rank + jnp.sum(jnp.where(top_idx[:, :, None] == experts, row_start, 0), axis=-1)
    tail = NB - N_EXPERTS + experts
    fill_start = jnp.concatenate([row_start + counts, tail * MOE_BM])
    fill_rows = jnp.concatenate([nblk_e * MOE_BM - counts,
                                 jnp.where(tail >= n_used, MOE_BM, 0)])
    return (bexp.astype(jnp.int32), n_used.reshape(1).astype(jnp.int32),
            dest.astype(jnp.int32), fill_start.astype(jnp.int32),
            fill_rows.astype(jnp.int32), NB)


def _combine_kernel(dest_ref, dest_next_ref, dest_next2_ref, h_ref, gate_ref, g_ref, ys3_ref,
                    o_ref, buf_ref, sem_ref, *, tm):
    i = pl.program_id(0)
    n = pl.num_programs(0)
    slot = lax.rem(i, 3)
    slot_ahead = lax.rem(i + 2, 3)
    n_groups = tm * TOP_K // DMA_UNROLL
    groups_per_trip = n_groups * COMBINE_ROWS // tm

    def issue_group(idx_ref, s, g):
        m0 = pl.multiple_of(g * DMA_UNROLL, DMA_UNROLL)
        srcs = [idx_ref[0, 0, m0 + u] for u in range(DMA_UNROLL)]
        for u in range(DMA_UNROLL):
            r = g * (DMA_UNROLL // TOP_K) + u // TOP_K
            _rows_copy(ys3_ref, srcs[u], buf_ref.at[s], (u % TOP_K) * tm + r, 1,
                       sem_ref.at[s]).start(priority=u % DMA_PRIORITIES)

    def gather_wait(s):
        _rows_copy(ys3_ref, 0, buf_ref.at[s], 0, tm * TOP_K, sem_ref.at[s]).wait()

    @pl.when(i == 0)
    def _():
        def group(g, _):
            issue_group(dest_ref, 0, g)
            issue_group(dest_next_ref, 1, g)
            return 0
        lax.fori_loop(0, n_groups, group, 0)

    gather_wait(slot)

    def row_block(rb, _):
        r0 = pl.multiple_of(rb * COMBINE_ROWS, COMBINE_ROWS)
        rows = pl.ds(r0, COMBINE_ROWS)
        gates = gate_ref[rows, :]
        chunks = [None] * CHUNKS
        ssq = jnp.zeros((COMBINE_ROWS, 1), F32)
        for s in range(PCHUNKS):
            acc_lo = h_ref[rows, s * LANES:(s + 1) * LANES]
            acc_hi = h_ref[rows, HALF + s * LANES:HALF + (s + 1) * LANES]
            for k in range(TOP_K):
                lo, hi = _unpack_halves(buf_ref[
                    slot, pl.ds((k * tm + r0) * PCHUNKS + s, COMBINE_ROWS, stride=PCHUNKS), :])
                acc_lo = acc_lo + gates[:, k:k + 1] * lo
                acc_hi = acc_hi + gates[:, k:k + 1] * hi
            chunks[s], chunks[PCHUNKS + s] = acc_lo, acc_hi
            ssq = ssq + jnp.sum(acc_lo * acc_lo + acc_hi * acc_hi, axis=-1, keepdims=True)
        for gg in range(groups_per_trip):
            issue_group(dest_next2_ref, slot_ahead, rb * groups_per_trip + gg)
        inv = lax.rsqrt(ssq * (1.0 / D_MODEL) + EPS)
        for s in range(CHUNKS):
            cols = slice(s * LANES, (s + 1) * LANES)
            o_ref[rows, cols] = chunks[s] * inv * g_ref[:, cols]
        return 0
    lax.fori_loop(0, tm // COMBINE_ROWS, row_block, 0)

    @pl.when(i == n - 1)
    def _():
        gather_wait(lax.rem(i + 1, 3))
        gather_wait(slot_ahead)


def _combine(h, ys3, dest, gates, g_final):
    T = h.shape[0]
    tm = COMBINE_TILE
    n = T // tm
    table = dest.reshape(n, 1, tm * TOP_K)
    idx_spec = lambda f: pl.BlockSpec((1, 1, tm * TOP_K), lambda i: (f(i), 0, 0),
                                      memory_space=pltpu.SMEM)
    return pl.pallas_call(
        functools.partial(_combine_kernel, tm=tm),
        grid=(n,),
        in_specs=[
            idx_spec(lambda i: i), idx_spec(lambda i: jnp.minimum(i + 1, n - 1)),
            idx_spec(lambda i: jnp.minimum(i + 2, n - 1)),
            pl.BlockSpec((tm, D_MODEL), lambda i: (i, 0)),
            pl.BlockSpec((tm, LANES), lambda i: (i, 0)),
            pl.BlockSpec((1, D_MODEL), lambda i: (0, 0)),
            pl.BlockSpec(memory_space=pl.ANY),
        ],
        out_specs=pl.BlockSpec((tm, D_MODEL), lambda i: (i, 0)),
        out_shape=jax.ShapeDtypeStruct((T, D_MODEL), F32),
        scratch_shapes=[
            pltpu.VMEM((3, TOP_K * tm * PCHUNKS, LANES), U32),
            pltpu.SemaphoreType.DMA((3,)),
        ],
        compiler_params=pltpu.CompilerParams(
            dimension_semantics=("arbitrary",), vmem_limit_bytes=VMEM_LIMIT),
        name="combine",
    )(table, table, table, h, gates, g_final, ys3)


def kernel(x, mem, g_attn_norm, w_in, b_glu, w_dw, b_dw, g_cv_ln, b_cv_ln, w_pw2, b_pw2, g_mem, w_mem_kv, g_sb_out, g_cv_out, g_mx_out, w_out, g_ffn_norm, w_router, b_router, w_gu, b_gu, w_down, b_down, g_final):
    B, S, D = x.shape
    T = B * S
    assert D == D_MODEL and S % ROW_TILE == 0 and g_attn_norm.shape[0] == 1
    l = 0
    row = lambda v: v.reshape(1, -1)
    x2 = x.reshape(T, D)

    proj = _inproj(x2, row(g_attn_norm[l]), w_in[l].astype(BF16))
    proj3 = proj.reshape(B, S, IN_W)
    o_sb = _sb_attention(proj3)
    o_cv = _conformer(proj3, row(b_glu[l]), w_dw[l].reshape(CONV_K, CV_W), row(b_dw[l]),
                      row(g_cv_ln[l]), row(b_cv_ln[l]), w_pw2[l].astype(BF16), row(b_pw2[l]))
    o_mx = _memx(mem, row(g_mem[l]), w_mem_kv[l].astype(BF16), proj3)

    h, hn3, idx_pad, rank_pad, gate_pad, counts = _outproj(
        o_sb.reshape(T, SB_W), o_cv.reshape(T, CV_W), o_mx.reshape(T, MX_W), x2,
        row(g_sb_out[l]), row(g_cv_out[l]), row(g_mx_out[l]), w_out[l].astype(BF16),
        row(g_ffn_norm[l]), w_router[l], row(b_router[l]))

    bexp, n_used, dest, fill_start, fill_rows, n_rows = _routing_tables(
        counts[0].astype(jnp.int32), idx_pad[:, :TOP_K], rank_pad[:, :TOP_K])
    xs3 = _dispatch(fill_start, fill_rows, dest, hn3, n_rows)
    ys3 = _experts(bexp, n_used, xs3, w_gu[l], b_gu[l].reshape(N_EXPERTS, 1, 2 * D_FF),
                   w_down[l], b_down[l].reshape(N_EXPERTS, 1, D_MODEL))
    out = _combine(h, ys3, dest, gate_pad, row(g_final))
    return out.reshape(B, S, D)
```

```python
import functools

import jax
import jax.numpy as jnp
from jax import lax
from jax.experimental import pallas as pl
from jax.experimental.pallas import tpu as pltpu

F32 = jnp.float32
BF16 = jnp.bfloat16

D_MODEL = 1024
HEAD_DIM = 64
SB_W = 512
CV_W = 256
MX_W = 256
IN_W = 3 * SB_W + 2 * CV_W + MX_W
CONV_K = 31
N_MEM = 256
N_EXPERTS = 32
TOP_K = 4
D_FF = 1024
SWIGLU_ALPHA = 1.702
SWIGLU_LIMIT = 7.0
EPS = 1e-6

LANES = 128
SUBLANES = 8
CHUNKS = D_MODEL // LANES
HALF = D_MODEL // 2
PCHUNKS = HALF // LANES
U32 = jnp.uint32
ROW_TILE = 512
SB_TILE = 256
SB_PAIRS = 4
SB_EXP_ZERO_BELOW = -104.0
CONV_PAD = 32
CONV_ROWS = 128
MX_ROWS = 1024
MOE_BM = 512
FF_CHUNK = 256
COMBINE_TILE = 512
COMBINE_ROWS = 64
DMA_UNROLL = 16
DMA_PRIORITIES = 2
VMEM_LIMIT = 56 * 1024 * 1024


def _rms(x, g):
    return x * lax.rsqrt(jnp.mean(x * x, axis=-1, keepdims=True) + EPS) * g


def _pack_halves(v):
    bits = pltpu.bitcast(v.astype(BF16).astype(F32), U32)
    return lax.shift_right_logical(bits[:, :HALF], U32(16)) | bits[:, HALF:]


def _unpack_halves(w):
    low = pltpu.bitcast(lax.shift_left(w, U32(16)), F32)
    high = pltpu.bitcast(w & U32(0xFFFF0000), F32)
    return low, high


def _inproj_kernel(x_ref, g_ref, w_ref, o_ref):
    xn = _rms(x_ref[...], g_ref[...]).astype(BF16)
    o_ref[...] = jnp.dot(xn, w_ref[...], preferred_element_type=F32).astype(BF16)


def _inproj(x2, g, w_bf):
    T = x2.shape[0]
    return pl.pallas_call(
        _inproj_kernel,
        grid=(T // ROW_TILE,),
        in_specs=[
            pl.BlockSpec((ROW_TILE, D_MODEL), lambda i: (i, 0)),
            pl.BlockSpec((1, D_MODEL), lambda i: (0, 0)),
            pl.BlockSpec((D_MODEL, IN_W), lambda i: (0, 0)),
        ],
        out_specs=pl.BlockSpec((ROW_TILE, IN_W), lambda i: (i, 0)),
        out_shape=jax.ShapeDtypeStruct((T, IN_W), BF16),
        compiler_params=pltpu.CompilerParams(
            dimension_semantics=("arbitrary",), vmem_limit_bytes=VMEM_LIMIT),
        name="inproj",
    )(x2, g, w_bf)


def _sb_kernel(q_ref, k_ref, v_ref, o_ref, *, seq):
    lane = lax.broadcasted_iota(jnp.int32, (SB_TILE, LANES), 1)
    head0 = lane < HEAD_DIM
    row = lax.broadcasted_iota(jnp.int32, (SB_TILE, SB_TILE), 0)
    col = lax.broadcasted_iota(jnp.int32, (SB_TILE, SB_TILE), 1)
    tri = jnp.where(row > col, 1.0, 0.0).astype(BF16)
    dmask = col < row

    def tile(head, qh, s0, c, diag):
        lanes = slice((head // 2) * LANES, (head // 2 + 1) * LANES)
        kb = k_ref[pl.ds(s0, SB_TILE), lanes]
        z = lax.dot_general(qh, kb, (((1,), (1,)), ((), ())), preferred_element_type=F32)
        lb = jnp.minimum(z, 0.0) - jnp.log(1.0 + jnp.exp(-jnp.abs(z)))
        l1m = lb - z
        if diag:
            l1m = jnp.where(dmask, l1m, 0.0)
        after = jnp.dot(l1m.astype(BF16), tri, preferred_element_type=F32)
        rowsum = jnp.broadcast_to(jnp.sum(l1m, axis=1, keepdims=True), (SB_TILE, LANES))
        arg = lb + after
        if c is not None:
            arg = arg + jnp.concatenate([c] * (SB_TILE // LANES), axis=1)
        a = jnp.exp(arg)
        if diag:
            a = jnp.where(dmask, a, 0.0)
        pv = jnp.dot(a.astype(BF16), v_ref[pl.ds(s0, SB_TILE), lanes],
                     preferred_element_type=F32)
        return pv, rowsum

    n_heads = 2 * SB_PAIRS

    def alive(cs):
        top = cs[0]
        for c in cs[1:]:
            top = jnp.maximum(top, c)
        return jnp.max(top) >= SB_EXP_ZERO_BELOW

    def heads(t0):
        out = []
        for p in range(SB_PAIRS):
            q = q_ref[pl.ds(t0, SB_TILE), p * LANES:(p + 1) * LANES] * (HEAD_DIM ** -0.5)
            out += [jnp.where(head0, q, jnp.zeros_like(q)), jnp.where(head0, jnp.zeros_like(q), q)]
        return out

    def store(t0, accs):
        for p in range(SB_PAIRS):
            o_ref[pl.ds(t0, SB_TILE), p * LANES:(p + 1) * LANES] = jnp.where(
                head0, accs[2 * p], accs[2 * p + 1]).astype(BF16)

    qs = heads(0)
    store(0, [tile(h, qs[h], 0, None, True)[0] for h in range(n_heads)])

    def qtile(i, _):
        t0 = pl.multiple_of(i * SB_TILE, SB_TILE)
        s0 = pl.multiple_of((i - 1) * SB_TILE, SB_TILE)
        qs = heads(t0)
        diag = [tile(h, qs[h], t0, None, True) for h in range(n_heads)]
        prev = [tile(h, qs[h], s0, diag[h][1], False) for h in range(n_heads)]
        accs = tuple(d[0] + p[0] for d, p in zip(diag, prev))
        cs = tuple(d[1] + p[1] for d, p in zip(diag, prev))

        def cond(st):
            return (st[0] >= 0) & st[1]

        def body(st):
            j, _, accs, cs = st
            s0 = pl.multiple_of(j * SB_TILE, SB_TILE)
            new = [tile(h, qs[h], s0, cs[h], False) for h in range(n_heads)]
            accs = tuple(a + t[0] for a, t in zip(accs, new))
            cs = tuple(c + t[1] for c, t in zip(cs, new))
            return j - 1, alive(cs), accs, cs

        st = lax.while_loop(cond, body, (i - 2, alive(cs), accs, cs))
        store(t0, st[2])
        return 0

    lax.fori_loop(1, seq // SB_TILE, qtile, 0)


def _sb_attention(proj3):
    B, S, _ = proj3.shape
    width = SB_PAIRS * LANES
    groups = SB_W // width
    return pl.pallas_call(
        functools.partial(_sb_kernel, seq=S),
        grid=(B, groups),
        in_specs=[
            pl.BlockSpec((None, S, width), lambda b, p: (b, 0, p)),
            pl.BlockSpec((None, S, width), lambda b, p: (b, 0, groups + p)),
            pl.BlockSpec((None, S, width), lambda b, p: (b, 0, 2 * groups + p)),
        ],
        out_specs=pl.BlockSpec((None, S, width), lambda b, p: (b, 0, p)),
        out_shape=jax.ShapeDtypeStruct((B, S, SB_W), BF16),
        compiler_params=pltpu.CompilerParams(
            dimension_semantics=("arbitrary", "arbitrary"), vmem_limit_bytes=VMEM_LIMIT),
        name="sb_attention",
    )(proj3, proj3, proj3)


def _conv_kernel(glu_ref, bglu_ref, wdw_ref, bdw_ref, gln_ref, bln_ref, wpw_ref, bpw_ref,
                 o_ref, upad_ref, shift_ref, *, seq):
    upad_ref[0:CONV_PAD, :] = jnp.zeros((CONV_PAD, CV_W), F32)
    for c in range(seq // CONV_ROWS):
        r0 = c * CONV_ROWS
        g = glu_ref[r0:r0 + CONV_ROWS, :].astype(F32) + bglu_ref[...]
        upad_ref[CONV_PAD + r0:CONV_PAD + r0 + CONV_ROWS, :] = (
            g[:, :CV_W] * jax.nn.sigmoid(g[:, CV_W:]))
    for c in range(seq // CONV_ROWS):
        r0 = c * CONV_ROWS
        acc = jnp.zeros((CONV_ROWS, CV_W), F32) + bdw_ref[...]
        span = CONV_ROWS + CONV_PAD - SUBLANES
        for r in range(1, SUBLANES):
            shift_ref[r, 0:span, :] = upad_ref[r0 + r:r0 + r + span, :]
        for k in range(CONV_K):
            off = CONV_PAD - (CONV_K - 1) + k
            r = off % SUBLANES
            a = off - r
            if r == 0:
                tap = upad_ref[r0 + a:r0 + a + CONV_ROWS, :]
            else:
                tap = shift_ref[r, a:a + CONV_ROWS, :]
            acc = acc + tap * wdw_ref[k:k + 1, :]
        mu = jnp.mean(acc, axis=-1, keepdims=True)
        d = acc - mu
        var = jnp.mean(d * d, axis=-1, keepdims=True)
        y = d * lax.rsqrt(var + EPS) * gln_ref[...] + bln_ref[...]
        y = y * jax.nn.sigmoid(y)
        out = jnp.dot(y.astype(BF16), wpw_ref[...], preferred_element_type=F32) + bpw_ref[...]
        o_ref[r0:r0 + CONV_ROWS, :] = out.astype(BF16)


def _conformer(proj3, b_glu, w_dw, b_dw, g_ln, b_ln, w_pw_bf, b_pw):
    B, S, _ = proj3.shape
    glu_block = (3 * SB_W) // (2 * CV_W)
    vec = lambda n: pl.BlockSpec((1, n), lambda b: (0, 0))
    return pl.pallas_call(
        functools.partial(_conv_kernel, seq=S),
        grid=(B,),
        in_specs=[
            pl.BlockSpec((None, S, 2 * CV_W), lambda b: (b, 0, glu_block)),
            vec(2 * CV_W),
            pl.BlockSpec((CONV_K, CV_W), lambda b: (0, 0)),
            vec(CV_W), vec(CV_W), vec(CV_W),
            pl.BlockSpec((CV_W, CV_W), lambda b: (0, 0)),
            vec(CV_W),
        ],
        out_specs=pl.BlockSpec((None, S, CV_W), lambda b: (b, 0, 0)),
        out_shape=jax.ShapeDtypeStruct((B, S, CV_W), BF16),
        scratch_shapes=[pltpu.VMEM((CONV_PAD + S, CV_W), F32),
                        pltpu.VMEM((SUBLANES, CONV_ROWS + CONV_PAD, CV_W), F32)],
        compiler_params=pltpu.CompilerParams(
            dimension_semantics=("arbitrary",), vmem_limit_bytes=VMEM_LIMIT),
        name="conformer",
    )(proj3, b_glu, w_dw, b_dw, g_ln, b_ln, w_pw_bf, b_pw)


def _memx_kernel(mem_ref, gm_ref, wkv_ref, q_ref, o_ref, *, seq):
    scale = HEAD_DIM ** -0.5
    mn = _rms(mem_ref[...], gm_ref[...]).astype(BF16)
    kv = jnp.dot(mn, wkv_ref[...], preferred_element_type=F32)
    km = kv[:, :MX_W].astype(BF16)
    vm = kv[:, MX_W:].astype(BF16)
    lane = lax.broadcasted_iota(jnp.int32, (MX_ROWS, MX_W), 1)

    def chunk(c, _):
        r0 = pl.multiple_of(c * MX_ROWS, MX_ROWS)
        q = q_ref[pl.ds(r0, MX_ROWS), :]
        out = jnp.zeros((MX_ROWS, MX_W), F32)
        for h in range(MX_W // HEAD_DIM):
            head = (lane >= HEAD_DIM * h) & (lane < HEAD_DIM * (h + 1))
            qh = jnp.where(head, q, jnp.zeros_like(q))
            s = lax.dot_general(qh, km, (((1,), (1,)), ((), ())),
                                preferred_element_type=F32) * scale
            p = jnp.exp(s - jnp.max(s, axis=-1, keepdims=True))
            p = p / jnp.sum(p, axis=-1, keepdims=True)
            oh = jnp.dot(p.astype(BF16), vm, preferred_element_type=F32)
            out = jnp.where(head, oh, out)
        o_ref[pl.ds(r0, MX_ROWS), :] = out.astype(BF16)
        return 0

    lax.fori_loop(0, seq // MX_ROWS, chunk, 0)


def _memx(mem, g_mem, w_kv_bf, proj3):
    B, S, _ = proj3.shape
    q_block = (3 * SB_W + 2 * CV_W) // MX_W
    return pl.pallas_call(
        functools.partial(_memx_kernel, seq=S),
        grid=(B,),
        in_specs=[
            pl.BlockSpec((None, N_MEM, D_MODEL), lambda b: (b, 0, 0)),
            pl.BlockSpec((1, D_MODEL), lambda b: (0, 0)),
            pl.BlockSpec((D_MODEL, 2 * MX_W), lambda b: (0, 0)),
            pl.BlockSpec((None, S, MX_W), lambda b: (b, 0, q_block)),
        ],
        out_specs=pl.BlockSpec((None, S, MX_W), lambda b: (b, 0, 0)),
        out_shape=jax.ShapeDtypeStruct((B, S, MX_W), BF16),
        compiler_params=pltpu.CompilerParams(
            dimension_semantics=("arbitrary",), vmem_limit_bytes=VMEM_LIMIT),
        name="memx",
    )(mem, g_mem, w_kv_bf, proj3)


def _outproj_kernel(sb_ref, cv_ref, mx_ref, x_ref, gsb_ref, gcv_ref, gmx_ref, wo_ref,
                    gffn_ref, wr_ref, br_ref, h_ref, hn3_ref, idx_ref, rank_ref, gate_ref,
                    cnt_ref, run_ref):
    def normed(o_ref, g_ref):
        return _rms(o_ref[...].astype(F32), g_ref[...]).astype(BF16)

    mix = jnp.dot(normed(sb_ref, gsb_ref), wo_ref[0:SB_W, :], preferred_element_type=F32)
    mix += jnp.dot(normed(cv_ref, gcv_ref), wo_ref[SB_W:SB_W + CV_W, :],
                   preferred_element_type=F32)
    mix += jnp.dot(normed(mx_ref, gmx_ref), wo_ref[SB_W + CV_W:, :],
                   preferred_element_type=F32)
    h = x_ref[...] + mix
    h_ref[...] = h
    hn = _rms(h, gffn_ref[...])
    packed = _pack_halves(hn)
    for s in range(PCHUNKS):
        hn3_ref[pl.ds(s, ROW_TILE, stride=PCHUNKS), :] = packed[:, s * LANES:(s + 1) * LANES]
    wr = wr_ref[...]
    hn_hi = hn.astype(BF16)
    hn_lo = (hn - hn_hi.astype(F32)).astype(BF16)
    wr_hi = wr.astype(BF16)
    wr_lo = (wr - wr_hi.astype(F32)).astype(BF16)
    logits = (jnp.dot(hn_hi, wr_hi, preferred_element_type=F32)
              + jnp.dot(hn_lo, wr_hi, preferred_element_type=F32)
              + jnp.dot(hn_hi, wr_lo, preferred_element_type=F32)) + br_ref[...]
    eid = lax.broadcasted_iota(jnp.int32, logits.shape, 1)
    vals, idxs = [], []
    for _ in range(TOP_K):
        m = jnp.max(logits, axis=-1, keepdims=True)
        i = jnp.min(jnp.where(logits == m, eid, N_EXPERTS), axis=-1, keepdims=True)
        vals.append(m)
        idxs.append(i)
        logits = jnp.where(eid == i, -jnp.inf, logits)
    es = [jnp.exp(v - vals[0]) for v in vals]
    denom = es[0] + es[1] + es[2] + es[3]

    @pl.when(pl.program_id(0) == 0)
    def _():
        run_ref[...] = jnp.zeros_like(run_ref)
    tm = logits.shape[0]
    row = lax.broadcasted_iota(jnp.int32, (tm, tm), 0)
    col = lax.broadcasted_iota(jnp.int32, (tm, tm), 1)
    before = jnp.where(col < row, 1.0, 0.0).astype(BF16)
    base = run_ref[...]
    ranks = []
    for k in range(TOP_K):
        onehot = jnp.where(eid == idxs[k], 1.0, 0.0)
        prefix = jnp.dot(before, onehot.astype(BF16), preferred_element_type=F32)
        ranks.append(jnp.sum(onehot * (prefix + base), axis=-1, keepdims=True))
        base = base + jnp.sum(onehot, axis=0, keepdims=True)
    run_ref[...] = base
    cnt_ref[...] = jnp.broadcast_to(base, cnt_ref.shape)

    lane = lax.broadcasted_iota(jnp.int32, idx_ref.shape, 1)
    idx_out = jnp.zeros(idx_ref.shape, jnp.int32)
    rank_out = jnp.zeros(rank_ref.shape, jnp.int32)
    gate_out = jnp.zeros(gate_ref.shape, F32)
    for k in range(TOP_K):
        idx_out = jnp.where(lane == k, idxs[k], idx_out)
        rank_out = jnp.where(lane == k, ranks[k].astype(jnp.int32), rank_out)
        gate_out = jnp.where(lane == k, es[k] / denom, gate_out)
    idx_ref[...] = idx_out
    rank_ref[...] = rank_out
    gate_ref[...] = gate_out


def _outproj(o_sb, o_cv, o_mx, x2, g_sb, g_cv, g_mx, w_out_bf, g_ffn, w_router, b_router):
    T = x2.shape[0]
    rows = lambda n: pl.BlockSpec((ROW_TILE, n), lambda i: (i, 0))
    full = lambda a, b: pl.BlockSpec((a, b), lambda i: (0, 0))
    return pl.pallas_call(
        _outproj_kernel,
        grid=(T // ROW_TILE,),
        in_specs=[
            rows(SB_W), rows(CV_W), rows(MX_W), rows(D_MODEL),
            full(1, SB_W), full(1, CV_W), full(1, MX_W),
            full(D_MODEL, D_MODEL), full(1, D_MODEL),
            full(D_MODEL, N_EXPERTS), full(1, N_EXPERTS),
        ],
        out_specs=[rows(D_MODEL),
                   pl.BlockSpec((ROW_TILE * PCHUNKS, LANES), lambda i: (i, 0)),
                   rows(LANES), rows(LANES), rows(LANES),
                   full(SUBLANES, N_EXPERTS)],
        out_shape=[
            jax.ShapeDtypeStruct((T, D_MODEL), F32),
            jax.ShapeDtypeStruct((T * PCHUNKS, LANES), U32),
            jax.ShapeDtypeStruct((T, LANES), jnp.int32),
            jax.ShapeDtypeStruct((T, LANES), jnp.int32),
            jax.ShapeDtypeStruct((T, LANES), F32),
            jax.ShapeDtypeStruct((SUBLANES, N_EXPERTS), F32),
        ],
        scratch_shapes=[pltpu.VMEM((1, N_EXPERTS), F32)],
        compiler_params=pltpu.CompilerParams(
            dimension_semantics=("arbitrary",), vmem_limit_bytes=VMEM_LIMIT),
        name="outproj_router",
    )(o_sb, o_cv, o_mx, x2, g_sb, g_cv, g_mx, w_out_bf, g_ffn, w_router, b_router)


def _rows_copy(src_ref, src_row, dst_ref, dst_row, n, sem):
    src = src_ref.at[pl.ds(pl.multiple_of(src_row * PCHUNKS, PCHUNKS), n * PCHUNKS), :]
    dst = dst_ref.at[pl.ds(pl.multiple_of(dst_row * PCHUNKS, PCHUNKS), n * PCHUNKS), :]
    return pltpu.make_async_copy(src, dst, sem)


def _dispatch_kernel(fill_start_ref, fill_rows_ref, dest_ref, hn3_ref, xs3_ref,
                     tile_ref, zero_ref, sem_ref, load_sem_ref, fill_sem_ref, *, tm):
    i = pl.program_id(0)
    n = pl.num_programs(0)
    slot = lax.rem(i, 3)
    slot_next = lax.rem(i + 1, 3)
    n_copies = tm * TOP_K
    pieces = [1 << p for p in range(MOE_BM.bit_length())]

    def fill(wait):
        for e in range(2 * N_EXPERTS):
            rows = fill_rows_ref[e]
            start = fill_start_ref[e]
            for p in pieces:
                @pl.when((rows & p) != 0)
                def _(p=p, start=start, rows=rows):
                    cp = _rows_copy(zero_ref, 0, xs3_ref, start + (rows & (p - 1)), p,
                                    fill_sem_ref.at[0])
                    if wait:
                        cp.wait()
                    else:
                        cp.start()

    @pl.when(i == 0)
    def _():
        zero_ref[...] = jnp.zeros_like(zero_ref)
        fill(wait=False)

    def tile_load(j, s):
        return _rows_copy(hn3_ref, j * tm, tile_ref.at[s], 0, tm, load_sem_ref.at[s])

    def rows_wait(s):
        for _ in range(TOP_K):
            _rows_copy(tile_ref.at[s], 0, xs3_ref, 0, tm, sem_ref.at[s]).wait()

    @pl.when(i == 0)
    def _():
        tile_load(0, 0).start()

    @pl.when(i >= 2)
    def _():
        rows_wait(slot_next)

    @pl.when(i + 1 < n)
    def _():
        tile_load(i + 1, slot_next).start()

    tile_load(i, slot).wait()

    def group(g, _):
        m0 = pl.multiple_of(g * DMA_UNROLL, DMA_UNROLL)
        dests = [dest_ref[0, 0, m0 + u] for u in range(DMA_UNROLL)]
        for u in range(DMA_UNROLL):
            r = g * (DMA_UNROLL // TOP_K) + u // TOP_K
            _rows_copy(tile_ref.at[slot], r, xs3_ref, dests[u], 1,
                       sem_ref.at[slot]).start(priority=u % DMA_PRIORITIES)
        return 0
    lax.fori_loop(0, n_copies // DMA_UNROLL, group, 0)

    @pl.when(i == 0)
    def _():
        fill(wait=True)

    @pl.when(i == n - 1)
    def _():
        @pl.when(i >= 1)
        def _():
            rows_wait(lax.rem(i + 2, 3))
        rows_wait(slot)


def _dispatch(fill_start, fill_rows, dest, hn3, n_blocks):
    T = hn3.shape[0] // PCHUNKS
    tm = ROW_TILE
    grid_spec = pltpu.PrefetchScalarGridSpec(
        num_scalar_prefetch=2,
        grid=(T // tm,),
        in_specs=[
            pl.BlockSpec((1, 1, tm * TOP_K), lambda i, fs, fr: (i, 0, 0),
                         memory_space=pltpu.SMEM),
            pl.BlockSpec(memory_space=pl.ANY),
        ],
        out_specs=pl.BlockSpec(memory_space=pl.ANY),
        scratch_shapes=[
            pltpu.VMEM((3, tm * PCHUNKS, LANES), U32),
            pltpu.VMEM((MOE_BM * PCHUNKS, LANES), U32),
            pltpu.SemaphoreType.DMA((3,)),
            pltpu.SemaphoreType.DMA((3,)),
            pltpu.SemaphoreType.DMA((1,)),
        ],
    )
    return pl.pallas_call(
        functools.partial(_dispatch_kernel, tm=tm),
        grid_spec=grid_spec,
        out_shape=jax.ShapeDtypeStruct((n_blocks * MOE_BM * PCHUNKS, LANES), U32),
        compiler_params=pltpu.CompilerParams(
            dimension_semantics=("arbitrary",), vmem_limit_bytes=VMEM_LIMIT),
        name="dispatch",
    )(fill_start, fill_rows, dest.reshape(T // tm, 1, tm * TOP_K), hn3)


def _experts_kernel(bexp_ref, nused_ref, x_ref, wgu_ref, bgu_ref, wd_ref, bd_ref, y_ref,
                    hm_ref, wgu_bf_ref, wd_bf_ref):
    b = pl.program_id(0)

    @pl.when(b >= nused_ref[0])
    def _():
        y_ref[...] = jnp.zeros_like(y_ref)

    @pl.when(b < nused_ref[0])
    def _():
        prev = jnp.maximum(b - 1, 0)
        @pl.when((b == 0) | (bexp_ref[b] != bexp_ref[prev]))
        def _():
            wgu_bf_ref[...] = wgu_ref[...].astype(BF16)
            wd_bf_ref[...] = wd_ref[...].astype(BF16)

        halves = [_unpack_halves(x_ref[pl.ds(s, MOE_BM, stride=PCHUNKS), :])
                  for s in range(PCHUNKS)]
        x = jnp.concatenate([lo for lo, _ in halves] + [hi for _, hi in halves],
                            axis=1).astype(BF16)

        for c0 in range(0, D_FF, FF_CHUNK):
            g_cols = slice(c0, c0 + FF_CHUNK)
            u_cols = slice(D_FF + c0, D_FF + c0 + FF_CHUNK)
            gate = jnp.dot(x, wgu_bf_ref[:, g_cols], preferred_element_type=F32) + bgu_ref[:, g_cols]
            up = jnp.dot(x, wgu_bf_ref[:, u_cols], preferred_element_type=F32) + bgu_ref[:, u_cols]
            gate = jnp.minimum(gate, SWIGLU_LIMIT)
            up = jnp.clip(up, -SWIGLU_LIMIT, SWIGLU_LIMIT)
            hm_ref[:, g_cols] = ((up + 1.0) * (gate * jax.nn.sigmoid(SWIGLU_ALPHA * gate))
                                 ).astype(BF16)
        y = jnp.dot(hm_ref[...], wd_bf_ref[...], preferred_element_type=F32) + bd_ref[...]
        packed = _pack_halves(y)
        for s in range(PCHUNKS):
            y_ref[pl.ds(s, MOE_BM, stride=PCHUNKS), :] = packed[:, s * LANES:(s + 1) * LANES]


def _experts(block_exp, n_used, xs3, w_gu, b_gu, w_down, b_down):
    NB = block_exp.shape[0]
    used = lambda b, nu: jnp.minimum(b, nu[0] - 1)
    grid_spec = pltpu.PrefetchScalarGridSpec(
        num_scalar_prefetch=2,
        grid=(NB,),
        in_specs=[
            pl.BlockSpec((MOE_BM * PCHUNKS, LANES), lambda b, be, nu: (used(b, nu), 0)),
            pl.BlockSpec((None, D_MODEL, 2 * D_FF), lambda b, be, nu: (be[b], 0, 0)),
            pl.BlockSpec((None, 1, 2 * D_FF), lambda b, be, nu: (be[b], 0, 0)),
            pl.BlockSpec((None, D_FF, D_MODEL), lambda b, be, nu: (be[b], 0, 0)),
            pl.BlockSpec((None, 1, D_MODEL), lambda b, be, nu: (be[b], 0, 0)),
        ],
        out_specs=pl.BlockSpec((MOE_BM * PCHUNKS, LANES), lambda b, be, nu: (b, 0)),
        scratch_shapes=[
            pltpu.VMEM((MOE_BM, D_FF), BF16),
            pltpu.VMEM((D_MODEL, 2 * D_FF), BF16),
            pltpu.VMEM((D_FF, D_MODEL), BF16),
        ],
    )
    return pl.pallas_call(
        _experts_kernel,
        grid_spec=grid_spec,
        out_shape=jax.ShapeDtypeStruct((NB * MOE_BM * PCHUNKS, LANES), U32),
        compiler_params=pltpu.CompilerParams(
            dimension_semantics=("arbitrary",), vmem_limit_bytes=VMEM_LIMIT),
        name="experts",
    )(block_exp, n_used, xs3, w_gu, b_gu, w_down, b_down)


def _routing_tables(counts, top_idx, rank):
    M = top_idx.shape[0] * TOP_K
    NB = -(-M // MOE_BM) + N_EXPERTS
    nblk_e = (counts + MOE_BM - 1) // MOE_BM
    blk_end = jnp.cumsum(nblk_e)
    row_start = (blk_end - nblk_e) * MOE_BM
    n_used = blk_end[-1]
    blk = jnp.minimum(jnp.arange(NB, dtype=jnp.int32), n_used - 1)
    bexp = jnp.sum((blk[:, None] >= blk_end[None, :]).astype(jnp.int32), axis=1)
    experts = jnp.arange(N_EXPERTS, dtype=jnp.int32)
    dest = rank + jnp.sum(jnp.where(top_idx[:, :, None] == experts, row_start, 0), axis=-1)
    tail = NB - N_EXPERTS + experts
    fill_start = jnp.concatenate([row_start + counts, tail * MOE_BM])
    fill_rows = jnp.concatenate([nblk_e * MOE_BM - counts,
                                 jnp.where(tail >= n_used, MOE_BM, 0)])
    return (bexp.astype(jnp.int32), n_used.reshape(1).astype(jnp.int32),
            dest.astype(jnp.int32), fill_start.astype(jnp.int32),
            fill_rows.astype(jnp.int32), NB)


def _combine_kernel(dest_ref, dest_next_ref, dest_next2_ref, h_ref, gate_ref, g_ref, ys3_ref,
                    o_ref, buf_ref, sem_ref, *, tm):
    i = pl.program_id(0)
    n = pl.num_programs(0)
    slot = lax.rem(i, 3)
    slot_ahead = lax.rem(i + 2, 3)
    n_groups = tm * TOP_K // DMA_UNROLL
    groups_per_trip = n_groups * COMBINE_ROWS // tm

    def issue_group(idx_ref, s, g):
        m0 = pl.multiple_of(g * DMA_UNROLL, DMA_UNROLL)
        srcs = [idx_ref[0, 0, m0 + u] for u in range(DMA_UNROLL)]
        for u in range(DMA_UNROLL):
            r = g * (DMA_UNROLL // TOP_K) + u // TOP_K
            _rows_copy(ys3_ref, srcs[u], buf_ref.at[s], (u % TOP_K) * tm + r, 1,
                       sem_ref.at[s]).start(priority=u % DMA_PRIORITIES)

    def gather_wait(s):
        _rows_copy(ys3_ref, 0, buf_ref.at[s], 0, tm * TOP_K, sem_ref.at[s]).wait()

    @pl.when(i == 0)
    def _():
        def group(g, _):
            issue_group(dest_ref, 0, g)
            issue_group(dest_next_ref, 1, g)
            return 0
        lax.fori_loop(0, n_groups, group, 0)

    gather_wait(slot)

    def row_block(rb, _):
        r0 = pl.multiple_of(rb * COMBINE_ROWS, COMBINE_ROWS)
        rows = pl.ds(r0, COMBINE_ROWS)
        gates = gate_ref[rows, :]
        chunks = [None] * CHUNKS
        ssq = jnp.zeros((COMBINE_ROWS, 1), F32)
        for s in range(PCHUNKS):
            acc_lo = h_ref[rows, s * LANES:(s + 1) * LANES]
            acc_hi = h_ref[rows, HALF + s * LANES:HALF + (s + 1) * LANES]
            for k in range(TOP_K):
                lo, hi = _unpack_halves(buf_ref[
                    slot, pl.ds((k * tm + r0) * PCHUNKS + s, COMBINE_ROWS, stride=PCHUNKS), :])
                acc_lo = acc_lo + gates[:, k:k + 1] * lo
                acc_hi = acc_hi + gates[:, k:k + 1] * hi
            chunks[s], chunks[PCHUNKS + s] = acc_lo, acc_hi
            ssq = ssq + jnp.sum(acc_lo * acc_lo + acc_hi * acc_hi, axis=-1, keepdims=True)
        for gg in range(groups_per_trip):
            issue_group(dest_next2_ref, slot_ahead, rb * groups_per_trip + gg)
        inv = lax.rsqrt(ssq * (1.0 / D_MODEL) + EPS)
        for s in range(CHUNKS):
            cols = slice(s * LANES, (s + 1) * LANES)
            o_ref[rows, cols] = chunks[s] * inv * g_ref[:, cols]
        return 0
    lax.fori_loop(0, tm // COMBINE_ROWS, row_block, 0)

    @pl.when(i == n - 1)
    def _():
        gather_wait(lax.rem(i + 1, 3))
        gather_wait(slot_ahead)


def _combine(h, ys3, dest, gates, g_final):
    T = h.shape[0]
    tm = COMBINE_TILE
    n = T // tm
    table = dest.reshape(n, 1, tm * TOP_K)
    idx_spec = lambda f: pl.BlockSpec((1, 1, tm * TOP_K), lambda i: (f(i), 0, 0),
                                      memory_space=pltpu.SMEM)
    return pl.pallas_call(
        functools.partial(_combine_kernel, tm=tm),
        grid=(n,),
        in_specs=[
            idx_spec(lambda i: i), idx_spec(lambda i: jnp.minimum(i + 1, n - 1)),
            idx_spec(lambda i: jnp.minimum(i + 2, n - 1)),
            pl.BlockSpec((tm, D_MODEL), lambda i: (i, 0)),
            pl.BlockSpec((tm, LANES), lambda i: (i, 0)),
            pl.BlockSpec((1, D_MODEL), lambda i: (0, 0)),
            pl.BlockSpec(memory_space=pl.ANY),
        ],
        out_specs=pl.BlockSpec((tm, D_MODEL), lambda i: (i, 0)),
        out_shape=jax.ShapeDtypeStruct((T, D_MODEL), F32),
        scratch_shapes=[
            pltpu.VMEM((3, TOP_K * tm * PCHUNKS, LANES), U32),
            pltpu.SemaphoreType.DMA((3,)),
        ],
        compiler_params=pltpu.CompilerParams(
            dimension_semantics=("arbitrary",), vmem_limit_bytes=VMEM_LIMIT),
        name="combine",
    )(table, table, table, h, gates, g_final, ys3)


def kernel(x, mem, g_attn_norm, w_in, b_glu, w_dw, b_dw, g_cv_ln, b_cv_ln, w_pw2, b_pw2, g_mem, w_mem_kv, g_sb_out, g_cv_out, g_mx_out, w_out, g_ffn_norm, w_router, b_router, w_gu, b_gu, w_down, b_down, g_final):
    B, S, D = x.shape
    T = B * S
    assert D == D_MODEL and S % ROW_TILE == 0 and g_attn_norm.shape[0] == 1
    l = 0
    row = lambda v: v.reshape(1, -1)
    x2 = x.reshape(T, D)

    proj = _inproj(x2, row(g_attn_norm[l]), w_in[l].astype(BF16))
    proj3 = proj.reshape(B, S, IN_W)
    o_sb = _sb_attention(proj3)
    o_cv = _conformer(proj3, row(b_glu[l]), w_dw[l].reshape(CONV_K, CV_W), row(b_dw[l]),
                      row(g_cv_ln[l]), row(b_cv_ln[l]), w_pw2[l].astype(BF16), row(b_pw2[l]))
    o_mx = _memx(mem, row(g_mem[l]), w_mem_kv[l].astype(BF16), proj3)

    h, hn3, idx_pad, rank_pad, gate_pad, counts = _outproj(
        o_sb.reshape(T, SB_W), o_cv.reshape(T, CV_W), o_mx.reshape(T, MX_W), x2,
        row(g_sb_out[l]), row(g_cv_out[l]), row(g_mx_out[l]), w_out[l].astype(BF16),
        row(g_ffn_norm[l]), w_router[l], row(b_router[l]))

    bexp, n_used, dest, fill_start, fill_rows, n_rows = _routing_tables(
        counts[0].astype(jnp.int32), idx_pad[:, :TOP_K], rank_pad[:, :TOP_K])
    xs3 = _dispatch(fill_start, fill_rows, dest, hn3, n_rows)
    ys3 = _experts(bexp, n_used, xs3, w_gu[l], b_gu[l].reshape(N_EXPERTS, 1, 2 * D_FF),
                   w_down[l], b_down[l].reshape(N_EXPERTS, 1, D_MODEL))
    out = _combine(h, ys3, dest, gate_pad, row(g_final))
    return out.reshape(B, S, D)
```

```python
import functools

import jax
import jax.numpy as jnp
from jax import lax
from jax.experimental import pallas as pl
from jax.experimental.pallas import tpu as pltpu

F32 = jnp.float32
BF16 = jnp.bfloat16

D_MODEL = 1024
HEAD_DIM = 64
SB_W = 512
CV_W = 256
MX_W = 256
IN_W = 3 * SB_W + 2 * CV_W + MX_W
CONV_K = 31
N_MEM = 256
N_EXPERTS = 32
TOP_K = 4
D_FF = 1024
SWIGLU_ALPHA = 1.702
SWIGLU_LIMIT = 7.0
EPS = 1e-6

LANES = 128
SUBLANES = 8
CHUNKS = D_MODEL // LANES
HALF = D_MODEL // 2
PCHUNKS = HALF // LANES
U32 = jnp.uint32
ROW_TILE = 512
SB_TILE = 256
SB_PAIRS = 4
SB_EXP_ZERO_BELOW = -104.0
CONV_PAD = 32
CONV_ROWS = 128
MX_ROWS = 1024
MOE_BM = 512
FF_CHUNK = 512
COMBINE_TILE = 512
COMBINE_ROWS = 64
DMA_UNROLL = 16
DMA_PRIORITIES = 2
VMEM_LIMIT = 56 * 1024 * 1024


def _rms(x, g):
    return x * lax.rsqrt(jnp.mean(x * x, axis=-1, keepdims=True) + EPS) * g


def _pack_halves(v):
    bits = pltpu.bitcast(v.astype(BF16).astype(F32), U32)
    return lax.shift_right_logical(bits[:, :HALF], U32(16)) | bits[:, HALF:]


def _unpack_halves(w):
    low = pltpu.bitcast(lax.shift_left(w, U32(16)), F32)
    high = pltpu.bitcast(w & U32(0xFFFF0000), F32)
    return low, high


def _inproj_kernel(x_ref, g_ref, w_ref, o_ref):
    xn = _rms(x_ref[...], g_ref[...]).astype(BF16)
    o_ref[...] = jnp.dot(xn, w_ref[...], preferred_element_type=F32).astype(BF16)


def _inproj(x2, g, w_bf):
    T = x2.shape[0]
    return pl.pallas_call(
        _inproj_kernel,
        grid=(T // ROW_TILE,),
        in_specs=[
            pl.BlockSpec((ROW_TILE, D_MODEL), lambda i: (i, 0)),
            pl.BlockSpec((1, D_MODEL), lambda i: (0, 0)),
            pl.BlockSpec((D_MODEL, IN_W), lambda i: (0, 0)),
        ],
        out_specs=pl.BlockSpec((ROW_TILE, IN_W), lambda i: (i, 0)),
        out_shape=jax.ShapeDtypeStruct((T, IN_W), BF16),
        compiler_params=pltpu.CompilerParams(
            dimension_semantics=("arbitrary",), vmem_limit_bytes=VMEM_LIMIT),
        name="inproj",
    )(x2, g, w_bf)


def _sb_kernel(q_ref, k_ref, v_ref, o_ref, *, seq):
    lane = lax.broadcasted_iota(jnp.int32, (SB_TILE, LANES), 1)
    head0 = lane < HEAD_DIM
    row = lax.broadcasted_iota(jnp.int32, (SB_TILE, SB_TILE), 0)
    col = lax.broadcasted_iota(jnp.int32, (SB_TILE, SB_TILE), 1)
    tri = jnp.where(row > col, 1.0, 0.0).astype(BF16)
    dmask = col < row

    def tile(head, qh, s0, c, diag):
        lanes = slice((head // 2) * LANES, (head // 2 + 1) * LANES)
        kb = k_ref[pl.ds(s0, SB_TILE), lanes]
        z = lax.dot_general(qh, kb, (((1,), (1,)), ((), ())), preferred_element_type=F32)
        lb = jnp.minimum(z, 0.0) - jnp.log(1.0 + jnp.exp(-jnp.abs(z)))
        l1m = lb - z
        if diag:
            l1m = jnp.where(dmask, l1m, 0.0)
        after = jnp.dot(l1m.astype(BF16), tri, preferred_element_type=F32)
        rowsum = jnp.broadcast_to(jnp.sum(l1m, axis=1, keepdims=True), (SB_TILE, LANES))
        arg = lb + after
        if c is not None:
            arg = arg + jnp.concatenate([c] * (SB_TILE // LANES), axis=1)
        a = jnp.exp(arg)
        if diag:
            a = jnp.where(dmask, a, 0.0)
        pv = jnp.dot(a.astype(BF16), v_ref[pl.ds(s0, SB_TILE), lanes],
                     preferred_element_type=F32)
        return pv, rowsum

    n_heads = 2 * SB_PAIRS

    def alive(cs):
        top = cs[0]
        for c in cs[1:]:
            top = jnp.maximum(top, c)
        return jnp.max(top) >= SB_EXP_ZERO_BELOW

    def heads(t0):
        out = []
        for p in range(SB_PAIRS):
            q = q_ref[pl.ds(t0, SB_TILE), p * LANES:(p + 1) * LANES] * (HEAD_DIM ** -0.5)
            out += [jnp.where(head0, q, jnp.zeros_like(q)), jnp.where(head0, jnp.zeros_like(q), q)]
        return out

    def store(t0, accs):
        for p in range(SB_PAIRS):
            o_ref[pl.ds(t0, SB_TILE), p * LANES:(p + 1) * LANES] = jnp.where(
                head0, accs[2 * p], accs[2 * p + 1]).astype(BF16)

    qs = heads(0)
    store(0, [tile(h, qs[h], 0, None, True)[0] for h in range(n_heads)])

    def qtile(i, _):
        t0 = pl.multiple_of(i * SB_TILE, SB_TILE)
        s0 = pl.multiple_of((i - 1) * SB_TILE, SB_TILE)
        qs = heads(t0)
        diag = [tile(h, qs[h], t0, None, True) for h in range(n_heads)]
        prev = [tile(h, qs[h], s0, diag[h][1], False) for h in range(n_heads)]
        accs = tuple(d[0] + p[0] for d, p in zip(diag, prev))
        cs = tuple(d[1] + p[1] for d, p in zip(diag, prev))

        def cond(st):
            return (st[0] >= 0) & st[1]

        def body(st):
            j, _, accs, cs = st
            s0 = pl.multiple_of(j * SB_TILE, SB_TILE)
            new = [tile(h, qs[h], s0, cs[h], False) for h in range(n_heads)]
            accs = tuple(a + t[0] for a, t in zip(accs, new))
            cs = tuple(c + t[1] for c, t in zip(cs, new))
            return j - 1, alive(cs), accs, cs

        st = lax.while_loop(cond, body, (i - 2, alive(cs), accs, cs))
        store(t0, st[2])
        return 0

    lax.fori_loop(1, seq // SB_TILE, qtile, 0)


def _sb_attention(proj3):
    B, S, _ = proj3.shape
    width = SB_PAIRS * LANES
    groups = SB_W // width
    return pl.pallas_call(
        functools.partial(_sb_kernel, seq=S),
        grid=(B, groups),
        in_specs=[
            pl.BlockSpec((None, S, width), lambda b, p: (b, 0, p)),
            pl.BlockSpec((None, S, width), lambda b, p: (b, 0, groups + p)),
            pl.BlockSpec((None, S, width), lambda b, p: (b, 0, 2 * groups + p)),
        ],
        out_specs=pl.BlockSpec((None, S, width), lambda b, p: (b, 0, p)),
        out_shape=jax.ShapeDtypeStruct((B, S, SB_W), BF16),
        compiler_params=pltpu.CompilerParams(
            dimension_semantics=("arbitrary", "arbitrary"), vmem_limit_bytes=VMEM_LIMIT),
        name="sb_attention",
    )(proj3, proj3, proj3)


def _conv_kernel(glu_ref, bglu_ref, wdw_ref, bdw_ref, gln_ref, bln_ref, wpw_ref, bpw_ref,
                 o_ref, upad_ref, shift_ref, *, seq):
    upad_ref[0:CONV_PAD, :] = jnp.zeros((CONV_PAD, CV_W), F32)
    for c in range(seq // CONV_ROWS):
        r0 = c * CONV_ROWS
        g = glu_ref[r0:r0 + CONV_ROWS, :].astype(F32) + bglu_ref[...]
        upad_ref[CONV_PAD + r0:CONV_PAD + r0 + CONV_ROWS, :] = (
            g[:, :CV_W] * jax.nn.sigmoid(g[:, CV_W:]))
    for c in range(seq // CONV_ROWS):
        r0 = c * CONV_ROWS
        acc = jnp.zeros((CONV_ROWS, CV_W), F32) + bdw_ref[...]
        span = CONV_ROWS + CONV_PAD - SUBLANES
        for r in range(1, SUBLANES):
            shift_ref[r, 0:span, :] = upad_ref[r0 + r:r0 + r + span, :]
        for k in range(CONV_K):
            off = CONV_PAD - (CONV_K - 1) + k
            r = off % SUBLANES
            a = off - r
            if r == 0:
                tap = upad_ref[r0 + a:r0 + a + CONV_ROWS, :]
            else:
                tap = shift_ref[r, a:a + CONV_ROWS, :]
            acc = acc + tap * wdw_ref[k:k + 1, :]
        mu = jnp.mean(acc, axis=-1, keepdims=True)
        d = acc - mu
        var = jnp.mean(d * d, axis=-1, keepdims=True)
        y = d * lax.rsqrt(var + EPS) * gln_ref[...] + bln_ref[...]
        y = y * jax.nn.sigmoid(y)
        out = jnp.dot(y.astype(BF16), wpw_ref[...], preferred_element_type=F32) + bpw_ref[...]
        o_ref[r0:r0 + CONV_ROWS, :] = out.astype(BF16)


def _conformer(proj3, b_glu, w_dw, b_dw, g_ln, b_ln, w_pw_bf, b_pw):
    B, S, _ = proj3.shape
    glu_block = (3 * SB_W) // (2 * CV_W)
    vec = lambda n: pl.BlockSpec((1, n), lambda b: (0, 0))
    return pl.pallas_call(
        functools.partial(_conv_kernel, seq=S),
        grid=(B,),
        in_specs=[
            pl.BlockSpec((None, S, 2 * CV_W), lambda b: (b, 0, glu_block)),
            vec(2 * CV_W),
            pl.BlockSpec((CONV_K, CV_W), lambda b: (0, 0)),
            vec(CV_W), vec(CV_W), vec(CV_W),
            pl.BlockSpec((CV_W, CV_W), lambda b: (0, 0)),
            vec(CV_W),
        ],
        out_specs=pl.BlockSpec((None, S, CV_W), lambda b: (b, 0, 0)),
        out_shape=jax.ShapeDtypeStruct((B, S, CV_W), BF16),
        scratch_shapes=[pltpu.VMEM((CONV_PAD + S, CV_W), F32),
                        pltpu.VMEM((SUBLANES, CONV_ROWS + CONV_PAD, CV_W), F32)],
        compiler_params=pltpu.CompilerParams(
            dimension_semantics=("arbitrary",), vmem_limit_bytes=VMEM_LIMIT),
        name="conformer",
    )(proj3, b_glu, w_dw, b_dw, g_ln, b_ln, w_pw_bf, b_pw)


def _memx_kernel(mem_ref, gm_ref, wkv_ref, q_ref, o_ref, *, seq):
    scale = HEAD_DIM ** -0.5
    mn = _rms(mem_ref[...], gm_ref[...]).astype(BF16)
    kv = jnp.dot(mn, wkv_ref[...], preferred_element_type=F32)
    km = kv[:, :MX_W].astype(BF16)
    vm = kv[:, MX_W:].astype(BF16)
    lane = lax.broadcasted_iota(jnp.int32, (MX_ROWS, MX_W), 1)

    def chunk(c, _):
        r0 = pl.multiple_of(c * MX_ROWS, MX_ROWS)
        q = q_ref[pl.ds(r0, MX_ROWS), :]
        out = jnp.zeros((MX_ROWS, MX_W), F32)
        for h in range(MX_W // HEAD_DIM):
            head = (lane >= HEAD_DIM * h) & (lane < HEAD_DIM * (h + 1))
            qh = jnp.where(head, q, jnp.zeros_like(q))
            s = lax.dot_general(qh, km, (((1,), (1,)), ((), ())),
                                preferred_element_type=F32) * scale
            p = jnp.exp(s - jnp.max(s, axis=-1, keepdims=True))
            p = p / jnp.sum(p, axis=-1, keepdims=True)
            oh = jnp.dot(p.astype(BF16), vm, preferred_element_type=F32)
            out = jnp.where(head, oh, out)
        o_ref[pl.ds(r0, MX_ROWS), :] = out.astype(BF16)
        return 0

    lax.fori_loop(0, seq // MX_ROWS, chunk, 0)


def _memx(mem, g_mem, w_kv_bf, proj3):
    B, S, _ = proj3.shape
    q_block = (3 * SB_W + 2 * CV_W) // MX_W
    return pl.pallas_call(
        functools.partial(_memx_kernel, seq=S),
        grid=(B,),
        in_specs=[
            pl.BlockSpec((None, N_MEM, D_MODEL), lambda b: (b, 0, 0)),
            pl.BlockSpec((1, D_MODEL), lambda b: (0, 0)),
            pl.BlockSpec((D_MODEL, 2 * MX_W), lambda b: (0, 0)),
            pl.BlockSpec((None, S, MX_W), lambda b: (b, 0, q_block)),
        ],
        out_specs=pl.BlockSpec((None, S, MX_W), lambda b: (b, 0, 0)),
        out_shape=jax.ShapeDtypeStruct((B, S, MX_W), BF16),
        compiler_params=pltpu.CompilerParams(
            dimension_semantics=("arbitrary",), vmem_limit_bytes=VMEM_LIMIT),
        name="memx",
    )(mem, g_mem, w_kv_bf, proj3)


def _outproj_kernel(sb_ref, cv_ref, mx_ref, x_ref, gsb_ref, gcv_ref, gmx_ref, wo_ref,
                    gffn_ref, wr_ref, br_ref, h_ref, hn3_ref, idx_ref, rank_ref, gate_ref,
                    cnt_ref, run_ref):
    def normed(o_ref, g_ref):
        return _rms(o_ref[...].astype(F32), g_ref[...]).astype(BF16)

    mix = jnp.dot(normed(sb_ref, gsb_ref), wo_ref[0:SB_W, :], preferred_element_type=F32)
    mix += jnp.dot(normed(cv_ref, gcv_ref), wo_ref[SB_W:SB_W + CV_W, :],
                   preferred_element_type=F32)
    mix += jnp.dot(normed(mx_ref, gmx_ref), wo_ref[SB_W + CV_W:, :],
                   preferred_element_type=F32)
    h = x_ref[...] + mix
    h_ref[...] = h
    hn = _rms(h, gffn_ref[...])
    packed = _pack_halves(hn)
    for s in range(PCHUNKS):
        hn3_ref[pl.ds(s, ROW_TILE, stride=PCHUNKS), :] = packed[:, s * LANES:(s + 1) * LANES]
    wr = wr_ref[...]
    hn_hi = hn.astype(BF16)
    hn_lo = (hn - hn_hi.astype(F32)).astype(BF16)
    wr_hi = wr.astype(BF16)
    wr_lo = (wr - wr_hi.astype(F32)).astype(BF16)
    logits = (jnp.dot(hn_hi, wr_hi, preferred_element_type=F32)
              + jnp.dot(hn_lo, wr_hi, preferred_element_type=F32)
              + jnp.dot(hn_hi, wr_lo, preferred_element_type=F32)) + br_ref[...]
    eid = lax.broadcasted_iota(jnp.int32, logits.shape, 1)
    vals, idxs = [], []
    for _ in range(TOP_K):
        m = jnp.max(logits, axis=-1, keepdims=True)
        i = jnp.min(jnp.where(logits == m, eid, N_EXPERTS), axis=-1, keepdims=True)
        vals.append(m)
        idxs.append(i)
        logits = jnp.where(eid == i, -jnp.inf, logits)
    es = [jnp.exp(v - vals[0]) for v in vals]
    denom = es[0] + es[1] + es[2] + es[3]

    @pl.when(pl.program_id(0) == 0)
    def _():
        run_ref[...] = jnp.zeros_like(run_ref)
    tm = logits.shape[0]
    row = lax.broadcasted_iota(jnp.int32, (tm, tm), 0)
    col = lax.broadcasted_iota(jnp.int32, (tm, tm), 1)
    before = jnp.where(col < row, 1.0, 0.0).astype(BF16)
    base = run_ref[...]
    ranks = []
    for k in range(TOP_K):
        onehot = jnp.where(eid == idxs[k], 1.0, 0.0)
        prefix = jnp.dot(before, onehot.astype(BF16), preferred_element_type=F32)
        ranks.append(jnp.sum(onehot * (prefix + base), axis=-1, keepdims=True))
        base = base + jnp.sum(onehot, axis=0, keepdims=True)
    run_ref[...] = base
    cnt_ref[...] = jnp.broadcast_to(base, cnt_ref.shape)

    lane = lax.broadcasted_iota(jnp.int32, idx_ref.shape, 1)
    idx_out = jnp.zeros(idx_ref.shape, jnp.int32)
    rank_out = jnp.zeros(rank_ref.shape, jnp.int32)
    gate_out = jnp.zeros(gate_ref.shape, F32)
    for k in range(TOP_K):
        idx_out = jnp.where(lane == k, idxs[k], idx_out)
        rank_out = jnp.where(lane == k, ranks[k].astype(jnp.int32), rank_out)
        gate_out = jnp.where(lane == k, es[k] / denom, gate_out)
    idx_ref[...] = idx_out
    rank_ref[...] = rank_out
    gate_ref[...] = gate_out


def _outproj(o_sb, o_cv, o_mx, x2, g_sb, g_cv, g_mx, w_out_bf, g_ffn, w_router, b_router):
    T = x2.shape[0]
    rows = lambda n: pl.BlockSpec((ROW_TILE, n), lambda i: (i, 0))
    full = lambda a, b: pl.BlockSpec((a, b), lambda i: (0, 0))
    return pl.pallas_call(
        _outproj_kernel,
        grid=(T // ROW_TILE,),
        in_specs=[
            rows(SB_W), rows(CV_W), rows(MX_W), rows(D_MODEL),
            full(1, SB_W), full(1, CV_W), full(1, MX_W),
            full(D_MODEL, D_MODEL), full(1, D_MODEL),
            full(D_MODEL, N_EXPERTS), full(1, N_EXPERTS),
        ],
        out_specs=[rows(D_MODEL),
                   pl.BlockSpec((ROW_TILE * PCHUNKS, LANES), lambda i: (i, 0)),
                   rows(LANES), rows(LANES), rows(LANES),
                   full(SUBLANES, N_EXPERTS)],
        out_shape=[
            jax.ShapeDtypeStruct((T, D_MODEL), F32),
            jax.ShapeDtypeStruct((T * PCHUNKS, LANES), U32),
            jax.ShapeDtypeStruct((T, LANES), jnp.int32),
            jax.ShapeDtypeStruct((T, LANES), jnp.int32),
            jax.ShapeDtypeStruct((T, LANES), F32),
            jax.ShapeDtypeStruct((SUBLANES, N_EXPERTS), F32),
        ],
        scratch_shapes=[pltpu.VMEM((1, N_EXPERTS), F32)],
        compiler_params=pltpu.CompilerParams(
            dimension_semantics=("arbitrary",), vmem_limit_bytes=VMEM_LIMIT),
        name="outproj_router",
    )(o_sb, o_cv, o_mx, x2, g_sb, g_cv, g_mx, w_out_bf, g_ffn, w_router, b_router)


def _rows_copy(src_ref, src_row, dst_ref, dst_row, n, sem):
    src = src_ref.at[pl.ds(pl.multiple_of(src_row * PCHUNKS, PCHUNKS), n * PCHUNKS), :]
    dst = dst_ref.at[pl.ds(pl.multiple_of(dst_row * PCHUNKS, PCHUNKS), n * PCHUNKS), :]
    return pltpu.make_async_copy(src, dst, sem)


def _dispatch_kernel(fill_start_ref, fill_rows_ref, dest_ref, hn3_ref, xs3_ref,
                     tile_ref, zero_ref, sem_ref, load_sem_ref, fill_sem_ref, *, tm):
    i = pl.program_id(0)
    n = pl.num_programs(0)
    slot = lax.rem(i, 3)
    slot_next = lax.rem(i + 1, 3)
    n_copies = tm * TOP_K
    pieces = [1 << p for p in range(MOE_BM.bit_length())]

    def fill(wait):
        for e in range(2 * N_EXPERTS):
            rows = fill_rows_ref[e]
            start = fill_start_ref[e]
            for p in pieces:
                @pl.when((rows & p) != 0)
                def _(p=p, start=start, rows=rows):
                    cp = _rows_copy(zero_ref, 0, xs3_ref, start + (rows & (p - 1)), p,
                                    fill_sem_ref.at[0])
                    if wait:
                        cp.wait()
                    else:
                        cp.start()

    @pl.when(i == 0)
    def _():
        zero_ref[...] = jnp.zeros_like(zero_ref)
        fill(wait=False)

    def tile_load(j, s):
        return _rows_copy(hn3_ref, j * tm, tile_ref.at[s], 0, tm, load_sem_ref.at[s])

    def rows_wait(s):
        for _ in range(TOP_K):
            _rows_copy(tile_ref.at[s], 0, xs3_ref, 0, tm, sem_ref.at[s]).wait()

    @pl.when(i == 0)
    def _():
        tile_load(0, 0).start()

    @pl.when(i >= 2)
    def _():
        rows_wait(slot_next)

    @pl.when(i + 1 < n)
    def _():
        tile_load(i + 1, slot_next).start()

    tile_load(i, slot).wait()

    def group(g, _):
        m0 = pl.multiple_of(g * DMA_UNROLL, DMA_UNROLL)
        dests = [dest_ref[0, 0, m0 + u] for u in range(DMA_UNROLL)]
        for u in range(DMA_UNROLL):
            r = g * (DMA_UNROLL // TOP_K) + u // TOP_K
            _rows_copy(tile_ref.at[slot], r, xs3_ref, dests[u], 1,
                       sem_ref.at[slot]).start(priority=u % DMA_PRIORITIES)
        return 0
    lax.fori_loop(0, n_copies // DMA_UNROLL, group, 0)

    @pl.when(i == 0)
    def _():
        fill(wait=True)

    @pl.when(i == n - 1)
    def _():
        @pl.when(i >= 1)
        def _():
            rows_wait(lax.rem(i + 2, 3))
        rows_wait(slot)


def _dispatch(fill_start, fill_rows, dest, hn3, n_blocks):
    T = hn3.shape[0] // PCHUNKS
    tm = ROW_TILE
    grid_spec = pltpu.PrefetchScalarGridSpec(
        num_scalar_prefetch=2,
        grid=(T // tm,),
        in_specs=[
            pl.BlockSpec((1, 1, tm * TOP_K), lambda i, fs, fr: (i, 0, 0),
                         memory_space=pltpu.SMEM),
            pl.BlockSpec(memory_space=pl.ANY),
        ],
        out_specs=pl.BlockSpec(memory_space=pl.ANY),
        scratch_shapes=[
            pltpu.VMEM((3, tm * PCHUNKS, LANES), U32),
            pltpu.VMEM((MOE_BM * PCHUNKS, LANES), U32),
            pltpu.SemaphoreType.DMA((3,)),
            pltpu.SemaphoreType.DMA((3,)),
            pltpu.SemaphoreType.DMA((1,)),
        ],
    )
    return pl.pallas_call(
        functools.partial(_dispatch_kernel, tm=tm),
        grid_spec=grid_spec,
        out_shape=jax.ShapeDtypeStruct((n_blocks * MOE_BM * PCHUNKS, LANES), U32),
        compiler_params=pltpu.CompilerParams(
            dimension_semantics=("arbitrary",), vmem_limit_bytes=VMEM_LIMIT),
        name="dispatch",
    )(fill_start, fill_rows, dest.reshape(T // tm, 1, tm * TOP_K), hn3)


def _experts_kernel(bexp_ref, nused_ref, x_ref, wgu_ref, bgu_ref, wd_ref, bd_ref, y_ref,
                    hm_ref, wgu_bf_ref, wd_bf_ref):
    b = pl.program_id(0)

    @pl.when(b >= nused_ref[0])
    def _():
        y_ref[...] = jnp.zeros_like(y_ref)

    @pl.when(b < nused_ref[0])
    def _():
        prev = jnp.maximum(b - 1, 0)
        @pl.when((b == 0) | (bexp_ref[b] != bexp_ref[prev]))
        def _():
            wgu_bf_ref[...] = wgu_ref[...].astype(BF16)
            wd_bf_ref[...] = wd_ref[...].astype(BF16)

        halves = [_unpack_halves(x_ref[pl.ds(s, MOE_BM, stride=PCHUNKS), :])
                  for s in range(PCHUNKS)]
        x = jnp.concatenate([lo for lo, _ in halves] + [hi for _, hi in halves],
                            axis=1).astype(BF16)

        for c0 in range(0, D_FF, FF_CHUNK):
            g_cols = slice(c0, c0 + FF_CHUNK)
            u_cols = slice(D_FF + c0, D_FF + c0 + FF_CHUNK)
            gate = jnp.dot(x, wgu_bf_ref[:, g_cols], preferred_element_type=F32) + bgu_ref[:, g_cols]
            up = jnp.dot(x, wgu_bf_ref[:, u_cols], preferred_element_type=F32) + bgu_ref[:, u_cols]
            gate = jnp.minimum(gate, SWIGLU_LIMIT)
            up = jnp.clip(up, -SWIGLU_LIMIT, SWIGLU_LIMIT)
            hm_ref[:, g_cols] = ((up + 1.0) * (gate * jax.nn.sigmoid(SWIGLU_ALPHA * gate))
                                 ).astype(BF16)
        y = jnp.dot(hm_ref[...], wd_bf_ref[...], preferred_element_type=F32) + bd_ref[...]
        packed = _pack_halves(y)
        for s in range(PCHUNKS):
            y_ref[pl.ds(s, MOE_BM, stride=PCHUNKS), :] = packed[:, s * LANES:(s + 1) * LANES]


def _experts(block_exp, n_used, xs3, w_gu, b_gu, w_down, b_down):
    NB = block_exp.shape[0]
    used = lambda b, nu: jnp.minimum(b, nu[0] - 1)
    grid_spec = pltpu.PrefetchScalarGridSpec(
        num_scalar_prefetch=2,
        grid=(NB,),
        in_specs=[
            pl.BlockSpec((MOE_BM * PCHUNKS, LANES), lambda b, be, nu: (used(b, nu), 0)),
            pl.BlockSpec((None, D_MODEL, 2 * D_FF), lambda b, be, nu: (be[b], 0, 0)),
            pl.BlockSpec((None, 1, 2 * D_FF), lambda b, be, nu: (be[b], 0, 0)),
            pl.BlockSpec((None, D_FF, D_MODEL), lambda b, be, nu: (be[b], 0, 0)),
            pl.BlockSpec((None, 1, D_MODEL), lambda b, be, nu: (be[b], 0, 0)),
        ],
        out_specs=pl.BlockSpec((MOE_BM * PCHUNKS, LANES), lambda b, be, nu: (b, 0)),
        scratch_shapes=[
            pltpu.VMEM((MOE_BM, D_FF), BF16),
            pltpu.VMEM((D_MODEL, 2 * D_FF), BF16),
            pltpu.VMEM((D_FF, D_MODEL), BF16),
        ],
    )
    return pl.pallas_call(
        _experts_kernel,
        grid_spec=grid_spec,
        out_shape=jax.ShapeDtypeStruct((NB * MOE_BM * PCHUNKS, LANES), U32),
        compiler_params=pltpu.CompilerParams(
            dimension_semantics=("arbitrary",), vmem_limit_bytes=VMEM_LIMIT),
        name="experts",
    )(block_exp, n_used, xs3, w_gu, b_gu, w_down, b_down)


def _routing_tables(counts, top_idx, rank):
    M = top_idx.shape[0] * TOP_K
    NB = -(-M // MOE_BM) + N_EXPERTS
    nblk_e = (counts + MOE_BM - 1) // MOE_BM
    blk_end = jnp.cumsum(nblk_e)
    row_start = (blk_end - nblk_e) * MOE_BM
    n_used = blk_end[-1]
    blk = jnp.minimum(jnp.arange(NB, dtype=jnp.int32), n_used - 1)
    bexp = jnp.sum((blk[:, None] >= blk_end[None, :]).astype(jnp.int32), axis=1)
    experts = jnp.arange(N_EXPERTS, dtype=jnp.int32)
    dest = rank + jnp.sum(jnp.where(top_idx[:, :, None] == experts, row_start, 0), axis=-1)
    tail = NB - N_EXPERTS + experts
    fill_start = jnp.concatenate([row_start + counts, tail * MOE_BM])
    fill_rows = jnp.concatenate([nblk_e * MOE_BM - counts,
                                 jnp.where(tail >= n_used, MOE_BM, 0)])
    return (bexp.astype(jnp.int32), n_used.reshape(1).astype(jnp.int32),
            dest.astype(jnp.int32), fill_start.astype(jnp.int32),
            fill_rows.astype(jnp.int32), NB)


def _combine_kernel(dest_ref, dest_next_ref, dest_next2_ref, h_ref, gate_ref, g_ref, ys3_ref,
                    o_ref, buf_ref, sem_ref, *, tm):
    i = pl.program_id(0)
    n = pl.num_programs(0)
    slot = lax.rem(i, 3)
    slot_ahead = lax.rem(i + 2, 3)
    n_groups = tm * TOP_K // DMA_UNROLL
    groups_per_trip = n_groups * COMBINE_ROWS // tm

    def issue_group(idx_ref, s, g):
        m0 = pl.multiple_of(g * DMA_UNROLL, DMA_UNROLL)
        srcs = [idx_ref[0, 0, m0 + u] for u in range(DMA_UNROLL)]
        for u in range(DMA_UNROLL):
            r = g * (DMA_UNROLL // TOP_K) + u // TOP_K
            _rows_copy(ys3_ref, srcs[u], buf_ref.at[s], (u % TOP_K) * tm + r, 1,
                       sem_ref.at[s]).start(priority=u % DMA_PRIORITIES)

    def gather_wait(s):
        _rows_copy(ys3_ref, 0, buf_ref.at[s], 0, tm * TOP_K, sem_ref.at[s]).wait()

    @pl.when(i == 0)
    def _():
        def group(g, _):
            issue_group(dest_ref, 0, g)
            issue_group(dest_next_ref, 1, g)
            return 0
        lax.fori_loop(0, n_groups, group, 0)

    gather_wait(slot)

    def row_block(rb, _):
        r0 = pl.multiple_of(rb * COMBINE_ROWS, COMBINE_ROWS)
        rows = pl.ds(r0, COMBINE_ROWS)
        gates = gate_ref[rows, :]
        chunks = [None] * CHUNKS
        ssq = jnp.zeros((COMBINE_ROWS, 1), F32)
        for s in range(PCHUNKS):
            acc_lo = h_ref[rows, s * LANES:(s + 1) * LANES]
            acc_hi = h_ref[rows, HALF + s * LANES:HALF + (s + 1) * LANES]
            for k in range(TOP_K):
                lo, hi = _unpack_halves(buf_ref[
                    slot, pl.ds((k * tm + r0) * PCHUNKS + s, COMBINE_ROWS, stride=PCHUNKS), :])
                acc_lo = acc_lo + gates[:, k:k + 1] * lo
                acc_hi = acc_hi + gates[:, k:k + 1] * hi
            chunks[s], chunks[PCHUNKS + s] = acc_lo, acc_hi
            ssq = ssq + jnp.sum(acc_lo * acc_lo + acc_hi * acc_hi, axis=-1, keepdims=True)
        for gg in range(groups_per_trip):
            issue_group(dest_next2_ref, slot_ahead, rb * groups_per_trip + gg)
        inv = lax.rsqrt(ssq * (1.0 / D_MODEL) + EPS)
        for s in range(CHUNKS):
            cols = slice(s * LANES, (s + 1) * LANES)
            o_ref[rows, cols] = chunks[s] * inv * g_ref[:, cols]
        return 0
    lax.fori_loop(0, tm // COMBINE_ROWS, row_block, 0)

    @pl.when(i == n - 1)
    def _():
        gather_wait(lax.rem(i + 1, 3))
        gather_wait(slot_ahead)


def _combine(h, ys3, dest, gates, g_final):
    T = h.shape[0]
    tm = COMBINE_TILE
    n = T // tm
    table = dest.reshape(n, 1, tm * TOP_K)
    idx_spec = lambda f: pl.BlockSpec((1, 1, tm * TOP_K), lambda i: (f(i), 0, 0),
                                      memory_space=pltpu.SMEM)
    return pl.pallas_call(
        functools.partial(_combine_kernel, tm=tm),
        grid=(n,),
        in_specs=[
            idx_spec(lambda i: i), idx_spec(lambda i: jnp.minimum(i + 1, n - 1)),
            idx_spec(lambda i: jnp.minimum(i + 2, n - 1)),
            pl.BlockSpec((tm, D_MODEL), lambda i: (i, 0)),
            pl.BlockSpec((tm, LANES), lambda i: (i, 0)),
            pl.BlockSpec((1, D_MODEL), lambda i: (0, 0)),
            pl.BlockSpec(memory_space=pl.ANY),
        ],
        out_specs=pl.BlockSpec((tm, D_MODEL), lambda i: (i, 0)),
        out_shape=jax.ShapeDtypeStruct((T, D_MODEL), F32),
        scratch_shapes=[
            pltpu.VMEM((3, TOP_K * tm * PCHUNKS, LANES), U32),
            pltpu.SemaphoreType.DMA((3,)),
        ],
        compiler_params=pltpu.CompilerParams(
            dimension_semantics=("arbitrary",), vmem_limit_bytes=VMEM_LIMIT),
        name="combine",
    )(table, table, table, h, gates, g_final, ys3)


def kernel(x, mem, g_attn_norm, w_in, b_glu, w_dw, b_dw, g_cv_ln, b_cv_ln, w_pw2, b_pw2, g_mem, w_mem_kv, g_sb_out, g_cv_out, g_mx_out, w_out, g_ffn_norm, w_router, b_router, w_gu, b_gu, w_down, b_down, g_final):
    B, S, D = x.shape
    T = B * S
    assert D == D_MODEL and S % ROW_TILE == 0 and g_attn_norm.shape[0] == 1
    l = 0
    row = lambda v: v.reshape(1, -1)
    x2 = x.reshape(T, D)

    proj = _inproj(x2, row(g_attn_norm[l]), w_in[l].astype(BF16))
    proj3 = proj.reshape(B, S, IN_W)
    o_sb = _sb_attention(proj3)
    o_cv = _conformer(proj3, row(b_glu[l]), w_dw[l].reshape(CONV_K, CV_W), row(b_dw[l]),
                      row(g_cv_ln[l]), row(b_cv_ln[l]), w_pw2[l].astype(BF16), row(b_pw2[l]))
    o_mx = _memx(mem, row(g_mem[l]), w_mem_kv[l].astype(BF16), proj3)

    h, hn3, idx_pad, rank_pad, gate_pad, counts = _outproj(
        o_sb.reshape(T, SB_W), o_cv.reshape(T, CV_W), o_mx.reshape(T, MX_W), x2,
        row(g_sb_out[l]), row(g_cv_out[l]), row(g_mx_out[l]), w_out[l].astype(BF16),
        row(g_ffn_norm[l]), w_router[l], row(b_router[l]))

    bexp, n_used, dest, fill_start, fill_rows, n_rows = _routing_tables(
        counts[0].astype(jnp.int32), idx_pad[:, :TOP_K], rank_pad[:, :TOP_K])
    xs3 = _dispatch(fill_start, fill_rows, dest, hn3, n_rows)
    ys3 = _experts(bexp, n_used, xs3, w_gu[l], b_gu[l].reshape(N_EXPERTS, 1, 2 * D_FF),
                   w_down[l], b_down[l].reshape(N_EXPERTS, 1, D_MODEL))
    out = _combine(h, ys3, dest, gate_pad, row(g_final))
    return out.reshape(B, S, D)
```

```python
import functools

import jax
import jax.numpy as jnp
from jax import lax
from jax.experimental import pallas as pl
from jax.experimental.pallas import tpu as pltpu

F32 = jnp.float32
BF16 = jnp.bfloat16

D_MODEL = 1024
HEAD_DIM = 64
SB_W = 512
CV_W = 256
MX_W = 256
IN_W = 3 * SB_W + 2 * CV_W + MX_W
CONV_K = 31
N_MEM = 256
N_EXPERTS = 32
TOP_K = 4
D_FF = 1024
SWIGLU_ALPHA = 1.702
SWIGLU_LIMIT = 7.0
EPS = 1e-6

LANES = 128
SUBLANES = 8
CHUNKS = D_MODEL // LANES
HALF = D_MODEL // 2
PCHUNKS = HALF // LANES
U32 = jnp.uint32
ROW_TILE = 512
SB_TILE = 256
SB_PAIRS = 4
SB_EXP_ZERO_BELOW = -104.0
CONV_PAD = 32
CONV_ROWS = 128
MX_ROWS = 1024
MOE_BM = 512
FF_CHUNK = 512
COMBINE_TILE = 512
COMBINE_ROWS = 64
DMA_UNROLL = 16
DMA_PRIORITIES = 2
VMEM_LIMIT = 56 * 1024 * 1024


def _rms(x, g):
    return x * lax.rsqrt(jnp.mean(x * x, axis=-1, keepdims=True) + EPS) * g


def _pack_halves(v):
    bits = pltpu.bitcast(v.astype(BF16).astype(F32), U32)
    return lax.shift_right_logical(bits[:, :HALF], U32(16)) | bits[:, HALF:]


def _unpack_halves(w):
    low = pltpu.bitcast(lax.shift_left(w, U32(16)), F32)
    high = pltpu.bitcast(w & U32(0xFFFF0000), F32)
    return low, high


def _inproj_kernel(x_ref, g_ref, w_ref, o_ref):
    xn = _rms(x_ref[...], g_ref[...]).astype(BF16)
    o_ref[...] = jnp.dot(xn, w_ref[...], preferred_element_type=F32).astype(BF16)


def _inproj(x2, g, w_bf):
    T = x2.shape[0]
    return pl.pallas_call(
        _inproj_kernel,
        grid=(T // ROW_TILE,),
        in_specs=[
            pl.BlockSpec((ROW_TILE, D_MODEL), lambda i: (i, 0)),
            pl.BlockSpec((1, D_MODEL), lambda i: (0, 0)),
            pl.BlockSpec((D_MODEL, IN_W), lambda i: (0, 0)),
        ],
        out_specs=pl.BlockSpec((ROW_TILE, IN_W), lambda i: (i, 0)),
        out_shape=jax.ShapeDtypeStruct((T, IN_W), BF16),
        compiler_params=pltpu.CompilerParams(
            dimension_semantics=("arbitrary",), vmem_limit_bytes=VMEM_LIMIT),
        name="inproj",
    )(x2, g, w_bf)


def _sb_kernel(q_ref, k_ref, v_ref, o_ref, *, seq):
    lane = lax.broadcasted_iota(jnp.int32, (SB_TILE, LANES), 1)
    head0 = lane < HEAD_DIM
    row = lax.broadcasted_iota(jnp.int32, (SB_TILE, SB_TILE), 0)
    col = lax.broadcasted_iota(jnp.int32, (SB_TILE, SB_TILE), 1)
    tri = jnp.where(row > col, 1.0, 0.0).astype(BF16)
    dmask = col < row

    def tile(head, qh, s0, c, diag):
        lanes = slice((head // 2) * LANES, (head // 2 + 1) * LANES)
        kb = k_ref[pl.ds(s0, SB_TILE), lanes]
        z = lax.dot_general(qh, kb, (((1,), (1,)), ((), ())), preferred_element_type=F32)
        lb = jnp.minimum(z, 0.0) - jnp.log(1.0 + jnp.exp(-jnp.abs(z)))
        l1m = lb - z
        if diag:
            l1m = jnp.where(dmask, l1m, 0.0)
        after = jnp.dot(l1m.astype(BF16), tri, preferred_element_type=F32)
        rowsum = jnp.broadcast_to(jnp.sum(l1m, axis=1, keepdims=True), (SB_TILE, LANES))
        arg = lb + after
        if c is not None:
            arg = arg + jnp.concatenate([c] * (SB_TILE // LANES), axis=1)
        a = jnp.exp(arg)
        if diag:
            a = jnp.where(dmask, a, 0.0)
        pv = jnp.dot(a.astype(BF16), v_ref[pl.ds(s0, SB_TILE), lanes],
                     preferred_element_type=F32)
        return pv, rowsum

    n_heads = 2 * SB_PAIRS

    def alive(cs):
        top = cs[0]
        for c in cs[1:]:
            top = jnp.maximum(top, c)
        return jnp.max(top) >= SB_EXP_ZERO_BELOW

    def heads(t0):
        out = []
        for p in range(SB_PAIRS):
            q = q_ref[pl.ds(t0, SB_TILE), p * LANES:(p + 1) * LANES] * (HEAD_DIM ** -0.5)
            out += [jnp.where(head0, q, jnp.zeros_like(q)), jnp.where(head0, jnp.zeros_like(q), q)]
        return out

    def store(t0, accs):
        for p in range(SB_PAIRS):
            o_ref[pl.ds(t0, SB_TILE), p * LANES:(p + 1) * LANES] = jnp.where(
                head0, accs[2 * p], accs[2 * p + 1]).astype(BF16)

    qs = heads(0)
    store(0, [tile(h, qs[h], 0, None, True)[0] for h in range(n_heads)])

    def qtile(i, _):
        t0 = pl.multiple_of(i * SB_TILE, SB_TILE)
        s0 = pl.multiple_of((i - 1) * SB_TILE, SB_TILE)
        qs = heads(t0)
        diag = [tile(h, qs[h], t0, None, True) for h in range(n_heads)]
        prev = [tile(h, qs[h], s0, diag[h][1], False) for h in range(n_heads)]
        accs = tuple(d[0] + p[0] for d, p in zip(diag, prev))
        cs = tuple(d[1] + p[1] for d, p in zip(diag, prev))

        def cond(st):
            return (st[0] >= 0) & st[1]

        def body(st):
            j, _, accs, cs = st
            s0 = pl.multiple_of(j * SB_TILE, SB_TILE)
            new = [tile(h, qs[h], s0, cs[h], False) for h in range(n_heads)]
            accs = tuple(a + t[0] for a, t in zip(accs, new))
            cs = tuple(c + t[1] for c, t in zip(cs, new))
            return j - 1, alive(cs), accs, cs

        st = lax.while_loop(cond, body, (i - 2, alive(cs), accs, cs))
        store(t0, st[2])
        return 0

    lax.fori_loop(1, seq // SB_TILE, qtile, 0)


def _sb_attention(proj3):
    B, S, _ = proj3.shape
    width = SB_PAIRS * LANES
    groups = SB_W // width
    return pl.pallas_call(
        functools.partial(_sb_kernel, seq=S),
        grid=(B, groups),
        in_specs=[
            pl.BlockSpec((None, S, width), lambda b, p: (b, 0, p)),
            pl.BlockSpec((None, S, width), lambda b, p: (b, 0, groups + p)),
            pl.BlockSpec((None, S, width), lambda b, p: (b, 0, 2 * groups + p)),
        ],
        out_specs=pl.BlockSpec((None, S, width), lambda b, p: (b, 0, p)),
        out_shape=jax.ShapeDtypeStruct((B, S, SB_W), BF16),
        compiler_params=pltpu.CompilerParams(
            dimension_semantics=("arbitrary", "arbitrary"), vmem_limit_bytes=VMEM_LIMIT),
        name="sb_attention",
    )(proj3, proj3, proj3)


def _conv_kernel(glu_ref, bglu_ref, wdw_ref, bdw_ref, gln_ref, bln_ref, wpw_ref, bpw_ref,
                 o_ref, upad_ref, shift_ref, *, seq):
    upad_ref[0:CONV_PAD, :] = jnp.zeros((CONV_PAD, CV_W), F32)
    for c in range(seq // CONV_ROWS):
        r0 = c * CONV_ROWS
        g = glu_ref[r0:r0 + CONV_ROWS, :].astype(F32) + bglu_ref[...]
        upad_ref[CONV_PAD + r0:CONV_PAD + r0 + CONV_ROWS, :] = (
            g[:, :CV_W] * jax.nn.sigmoid(g[:, CV_W:]))
    for c in range(seq // CONV_ROWS):
        r0 = c * CONV_ROWS
        acc = jnp.zeros((CONV_ROWS, CV_W), F32) + bdw_ref[...]
        span = CONV_ROWS + CONV_PAD - SUBLANES
        for r in range(1, SUBLANES):
            shift_ref[r, 0:span, :] = upad_ref[r0 + r:r0 + r + span, :]
        for k in range(CONV_K):
            off = CONV_PAD - (CONV_K - 1) + k
            r = off % SUBLANES
            a = off - r
            if r == 0:
                tap = upad_ref[r0 + a:r0 + a + CONV_ROWS, :]
            else:
                tap = shift_ref[r, a:a + CONV_ROWS, :]
            acc = acc + tap * wdw_ref[k:k + 1, :]
        mu = jnp.mean(acc, axis=-1, keepdims=True)
        d = acc - mu
        var = jnp.mean(d * d, axis=-1, keepdims=True)
        y = d * lax.rsqrt(var + EPS) * gln_ref[...] + bln_ref[...]
        y = y * jax.nn.sigmoid(y)
        out = jnp.dot(y.astype(BF16), wpw_ref[...], preferred_element_type=F32) + bpw_ref[...]
        o_ref[r0:r0 + CONV_ROWS, :] = out.astype(BF16)


def _conformer(proj3, b_glu, w_dw, b_dw, g_ln, b_ln, w_pw_bf, b_pw):
    B, S, _ = proj3.shape
    glu_block = (3 * SB_W) // (2 * CV_W)
    vec = lambda n: pl.BlockSpec((1, n), lambda b: (0, 0))
    return pl.pallas_call(
        functools.partial(_conv_kernel, seq=S),
        grid=(B,),
        in_specs=[
            pl.BlockSpec((None, S, 2 * CV_W), lambda b: (b, 0, glu_block)),
            vec(2 * CV_W),
            pl.BlockSpec((CONV_K, CV_W), lambda b: (0, 0)),
            vec(CV_W), vec(CV_W), vec(CV_W),
            pl.BlockSpec((CV_W, CV_W), lambda b: (0, 0)),
            vec(CV_W),
        ],
        out_specs=pl.BlockSpec((None, S, CV_W), lambda b: (b, 0, 0)),
        out_shape=jax.ShapeDtypeStruct((B, S, CV_W), BF16),
        scratch_shapes=[pltpu.VMEM((CONV_PAD + S, CV_W), F32),
                        pltpu.VMEM((SUBLANES, CONV_ROWS + CONV_PAD, CV_W), F32)],
        compiler_params=pltpu.CompilerParams(
            dimension_semantics=("arbitrary",), vmem_limit_bytes=VMEM_LIMIT),
        name="conformer",
    )(proj3, b_glu, w_dw, b_dw, g_ln, b_ln, w_pw_bf, b_pw)


def _memx_kernel(mem_ref, gm_ref, wkv_ref, q_ref, o_ref, *, seq):
    scale = HEAD_DIM ** -0.5
    mn = _rms(mem_ref[...], gm_ref[...]).astype(BF16)
    kv = jnp.dot(mn, wkv_ref[...], preferred_element_type=F32)
    km = kv[:, :MX_W].astype(BF16)
    vm = kv[:, MX_W:].astype(BF16)
    lane = lax.broadcasted_iota(jnp.int32, (MX_ROWS, MX_W), 1)

    def chunk(c, _):
        r0 = pl.multiple_of(c * MX_ROWS, MX_ROWS)
        q = q_ref[pl.ds(r0, MX_ROWS), :]
        out = jnp.zeros((MX_ROWS, MX_W), F32)
        for h in range(MX_W // HEAD_DIM):
            head = (lane >= HEAD_DIM * h) & (lane < HEAD_DIM * (h + 1))
            qh = jnp.where(head, q, jnp.zeros_like(q))
            s = lax.dot_general(qh, km, (((1,), (1,)), ((), ())),
                                preferred_element_type=F32) * scale
            p = jnp.exp(s - jnp.max(s, axis=-1, keepdims=True))
            p = p / jnp.sum(p, axis=-1, keepdims=True)
            oh = jnp.dot(p.astype(BF16), vm, preferred_element_type=F32)
            out = jnp.where(head, oh, out)
        o_ref[pl.ds(r0, MX_ROWS), :] = out.astype(BF16)
        return 0

    lax.fori_loop(0, seq // MX_ROWS, chunk, 0)


def _memx(mem, g_mem, w_kv_bf, proj3):
    B, S, _ = proj3.shape
    q_block = (3 * SB_W + 2 * CV_W) // MX_W
    return pl.pallas_call(
        functools.partial(_memx_kernel, seq=S),
        grid=(B,),
        in_specs=[
            pl.BlockSpec((None, N_MEM, D_MODEL), lambda b: (b, 0, 0)),
            pl.BlockSpec((1, D_MODEL), lambda b: (0, 0)),
            pl.BlockSpec((D_MODEL, 2 * MX_W), lambda b: (0, 0)),
            pl.BlockSpec((None, S, MX_W), lambda b: (b, 0, q_block)),
        ],
        out_specs=pl.BlockSpec((None, S, MX_W), lambda b: (b, 0, 0)),
        out_shape=jax.ShapeDtypeStruct((B, S, MX_W), BF16),
        compiler_params=pltpu.CompilerParams(
            dimension_semantics=("arbitrary",), vmem_limit_bytes=VMEM_LIMIT),
        name="memx",
    )(mem, g_mem, w_kv_bf, proj3)


def _outproj_kernel(sb_ref, cv_ref, mx_ref, x_ref, gsb_ref, gcv_ref, gmx_ref, wo_ref,
                    gffn_ref, wr_ref, br_ref, h_ref, hn3_ref, idx_ref, rank_ref, gate_ref,
                    cnt_ref, run_ref):
    def normed(o_ref, g_ref):
        return _rms(o_ref[...].astype(F32), g_ref[...]).astype(BF16)

    mix = jnp.dot(normed(sb_ref, gsb_ref), wo_ref[0:SB_W, :], preferred_element_type=F32)
    mix += jnp.dot(normed(cv_ref, gcv_ref), wo_ref[SB_W:SB_W + CV_W, :],
                   preferred_element_type=F32)
    mix += jnp.dot(normed(mx_ref, gmx_ref), wo_ref[SB_W + CV_W:, :],
                   preferred_element_type=F32)
    h = x_ref[...] + mix
    h_ref[...] = h
    hn = _rms(h, gffn_ref[...])
    packed = _pack_halves(hn)
    for s in range(PCHUNKS):
        hn3_ref[pl.ds(s, ROW_TILE, stride=PCHUNKS), :] = packed[:, s * LANES:(s + 1) * LANES]
    wr = wr_ref[...]
    hn_hi = hn.astype(BF16)
    hn_lo = (hn - hn_hi.astype(F32)).astype(BF16)
    wr_hi = wr.astype(BF16)
    wr_lo = (wr - wr_hi.astype(F32)).astype(BF16)
    logits = (jnp.dot(hn_hi, wr_hi, preferred_element_type=F32)
              + jnp.dot(hn_lo, wr_hi, preferred_element_type=F32)
              + jnp.dot(hn_hi, wr_lo, preferred_element_type=F32)) + br_ref[...]
    eid = lax.broadcasted_iota(jnp.int32, logits.shape, 1)
    vals, idxs = [], []
    for _ in range(TOP_K):
        m = jnp.max(logits, axis=-1, keepdims=True)
        i = jnp.min(jnp.where(logits == m, eid, N_EXPERTS), axis=-1, keepdims=True)
        vals.append(m)
        idxs.append(i)
        logits = jnp.where(eid == i, -jnp.inf, logits)
    es = [jnp.exp(v - vals[0]) for v in vals]
    denom = es[0] + es[1] + es[2] + es[3]

    @pl.when(pl.program_id(0) == 0)
    def _():
        run_ref[...] = jnp.zeros_like(run_ref)
    tm = logits.shape[0]
    row = lax.broadcasted_iota(jnp.int32, (tm, tm), 0)
    col = lax.broadcasted_iota(jnp.int32, (tm, tm), 1)
    before = jnp.where(col < row, 1.0, 0.0).astype(BF16)
    base = run_ref[...]
    ranks = []
    for k in range(TOP_K):
        onehot = jnp.where(eid == idxs[k], 1.0, 0.0)
        prefix = jnp.dot(before, onehot.astype(BF16), preferred_element_type=F32)
        ranks.append(jnp.sum(onehot * (prefix + base), axis=-1, keepdims=True))
        base = base + jnp.sum(onehot, axis=0, keepdims=True)
    run_ref[...] = base
    cnt_ref[...] = jnp.broadcast_to(base, cnt_ref.shape)

    lane = lax.broadcasted_iota(jnp.int32, idx_ref.shape, 1)
    idx_out = jnp.zeros(idx_ref.shape, jnp.int32)
    rank_out = jnp.zeros(rank_ref.shape, jnp.int32)
    gate_out = jnp.zeros(gate_ref.shape, F32)
    for k in range(TOP_K):
        idx_out = jnp.where(lane == k, idxs[k], idx_out)
        rank_out = jnp.where(lane == k, ranks[k].astype(jnp.int32), rank_out)
        gate_out = jnp.where(lane == k, es[k] / denom, gate_out)
    idx_ref[...] = idx_out
    rank_ref[...] = rank_out
    gate_ref[...] = gate_out


def _outproj(o_sb, o_cv, o_mx, x2, g_sb, g_cv, g_mx, w_out_bf, g_ffn, w_router, b_router):
    T = x2.shape[0]
    rows = lambda n: pl.BlockSpec((ROW_TILE, n), lambda i: (i, 0))
    full = lambda a, b: pl.BlockSpec((a, b), lambda i: (0, 0))
    return pl.pallas_call(
        _outproj_kernel,
        grid=(T // ROW_TILE,),
        in_specs=[
            rows(SB_W), rows(CV_W), rows(MX_W), rows(D_MODEL),
            full(1, SB_W), full(1, CV_W), full(1, MX_W),
            full(D_MODEL, D_MODEL), full(1, D_MODEL),
            full(D_MODEL, N_EXPERTS), full(1, N_EXPERTS),
        ],
        out_specs=[rows(D_MODEL),
                   pl.BlockSpec((ROW_TILE * PCHUNKS, LANES), lambda i: (i, 0)),
                   rows(LANES), rows(LANES), rows(LANES),
                   full(SUBLANES, N_EXPERTS)],
        out_shape=[
            jax.ShapeDtypeStruct((T, D_MODEL), F32),
            jax.ShapeDtypeStruct((T * PCHUNKS, LANES), U32),
            jax.ShapeDtypeStruct((T, LANES), jnp.int32),
            jax.ShapeDtypeStruct((T, LANES), jnp.int32),
            jax.ShapeDtypeStruct((T, LANES), F32),
            jax.ShapeDtypeStruct((SUBLANES, N_EXPERTS), F32),
        ],
        scratch_shapes=[pltpu.VMEM((1, N_EXPERTS), F32)],
        compiler_params=pltpu.CompilerParams(
            dimension_semantics=("arbitrary",), vmem_limit_bytes=VMEM_LIMIT),
        name="outproj_router",
    )(o_sb, o_cv, o_mx, x2, g_sb, g_cv, g_mx, w_out_bf, g_ffn, w_router, b_router)


def _rows_copy(src_ref, src_row, dst_ref, dst_row, n, sem):
    src = src_ref.at[pl.ds(pl.multiple_of(src_row * PCHUNKS, PCHUNKS), n * PCHUNKS), :]
    dst = dst_ref.at[pl.ds(pl.multiple_of(dst_row * PCHUNKS, PCHUNKS), n * PCHUNKS), :]
    return pltpu.make_async_copy(src, dst, sem)


def _dispatch_kernel(fill_start_ref, fill_rows_ref, dest_ref, hn3_ref, xs3_ref,
                     tile_ref, zero_ref, sem_ref, load_sem_ref, fill_sem_ref, *, tm):
    i = pl.program_id(0)
    n = pl.num_programs(0)
    slot = lax.rem(i, 3)
    slot_next = lax.rem(i + 1, 3)
    n_copies = tm * TOP_K
    pieces = [1 << p for p in range(MOE_BM.bit_length())]

    def fill(wait):
        for e in range(2 * N_EXPERTS):
            rows = fill_rows_ref[e]
            start = fill_start_ref[e]
            for p in pieces:
                @pl.when((rows & p) != 0)
                def _(p=p, start=start, rows=rows):
                    cp = _rows_copy(zero_ref, 0, xs3_ref, start + (rows & (p - 1)), p,
                                    fill_sem_ref.at[0])
                    if wait:
                        cp.wait()
                    else:
                        cp.start()

    @pl.when(i == 0)
    def _():
        zero_ref[...] = jnp.zeros_like(zero_ref)
        fill(wait=False)

    def tile_load(j, s):
        return _rows_copy(hn3_ref, j * tm, tile_ref.at[s], 0, tm, load_sem_ref.at[s])

    def rows_wait(s):
        for _ in range(TOP_K):
            _rows_copy(tile_ref.at[s], 0, xs3_ref, 0, tm, sem_ref.at[s]).wait()

    @pl.when(i == 0)
    def _():
        tile_load(0, 0).start()

    @pl.when(i >= 2)
    def _():
        rows_wait(slot_next)

    @pl.when(i + 1 < n)
    def _():
        tile_load(i + 1, slot_next).start()

    tile_load(i, slot).wait()

    def group(g, _):
        m0 = pl.multiple_of(g * DMA_UNROLL, DMA_UNROLL)
        dests = [dest_ref[0, 0, m0 + u] for u in range(DMA_UNROLL)]
        for u in range(DMA_UNROLL):
            r = g * (DMA_UNROLL // TOP_K) + u // TOP_K
            _rows_copy(tile_ref.at[slot], r, xs3_ref, dests[u], 1,
                       sem_ref.at[slot]).start(priority=u % DMA_PRIORITIES)
        return 0
    lax.fori_loop(0, n_copies // DMA_UNROLL, group, 0)

    @pl.when(i == 0)
    def _():
        fill(wait=True)

    @pl.when(i == n - 1)
    def _():
        @pl.when(i >= 1)
        def _():
            rows_wait(lax.rem(i + 2, 3))
        rows_wait(slot)


def _dispatch(fill_start, fill_rows, dest, hn3, n_blocks):
    T = hn3.shape[0] // PCHUNKS
    tm = ROW_TILE
    grid_spec = pltpu.PrefetchScalarGridSpec(
        num_scalar_prefetch=2,
        grid=(T // tm,),
        in_specs=[
            pl.BlockSpec((1, 1, tm * TOP_K), lambda i, fs, fr: (i, 0, 0),
                         memory_space=pltpu.SMEM),
            pl.BlockSpec(memory_space=pl.ANY),
        ],
        out_specs=pl.BlockSpec(memory_space=pl.ANY),
        scratch_shapes=[
            pltpu.VMEM((3, tm * PCHUNKS, LANES), U32),
            pltpu.VMEM((MOE_BM * PCHUNKS, LANES), U32),
            pltpu.SemaphoreType.DMA((3,)),
            pltpu.SemaphoreType.DMA((3,)),
            pltpu.SemaphoreType.DMA((1,)),
        ],
    )
    return pl.pallas_call(
        functools.partial(_dispatch_kernel, tm=tm),
        grid_spec=grid_spec,
        out_shape=jax.ShapeDtypeStruct((n_blocks * MOE_BM * PCHUNKS, LANES), U32),
        compiler_params=pltpu.CompilerParams(
            dimension_semantics=("arbitrary",), vmem_limit_bytes=VMEM_LIMIT),
        name="dispatch",
    )(fill_start, fill_rows, dest.reshape(T // tm, 1, tm * TOP_K), hn3)


def _experts_kernel(bexp_ref, nused_ref, next_ref, x_ref, wgu_hbm, bgu_ref, wd_hbm, bd_ref,
                    y_ref, hm_ref, wgu_f32_ref, wd_f32_ref, wgu_bf_ref, wd_bf_ref, wsem_ref):
    b = pl.program_id(0)

    def weight_copies(e):
        return (pltpu.make_async_copy(wgu_hbm.at[e], wgu_f32_ref, wsem_ref.at[0]),
                pltpu.make_async_copy(wd_hbm.at[e], wd_f32_ref, wsem_ref.at[1]))

    @pl.when(b == 0)
    def _():
        for cp in weight_copies(bexp_ref[0]):
            cp.start()

    @pl.when(b >= nused_ref[0])
    def _():
        y_ref[...] = jnp.zeros_like(y_ref)

    @pl.when(b < nused_ref[0])
    def _():
        prev = jnp.maximum(b - 1, 0)
        expert = bexp_ref[b]
        @pl.when((b == 0) | (expert != bexp_ref[prev]))
        def _():
            for cp in weight_copies(expert):
                cp.wait()
            wgu_bf_ref[...] = wgu_f32_ref[...].astype(BF16)
            wd_bf_ref[...] = wd_f32_ref[...].astype(BF16)
            @pl.when(next_ref[b] != expert)
            def _():
                for cp in weight_copies(next_ref[b]):
                    cp.start()

        halves = [_unpack_halves(x_ref[pl.ds(s, MOE_BM, stride=PCHUNKS), :])
                  for s in range(PCHUNKS)]
        x = jnp.concatenate([lo for lo, _ in halves] + [hi for _, hi in halves],
                            axis=1).astype(BF16)

        for c0 in range(0, D_FF, FF_CHUNK):
            g_cols = slice(c0, c0 + FF_CHUNK)
            u_cols = slice(D_FF + c0, D_FF + c0 + FF_CHUNK)
            gate = jnp.dot(x, wgu_bf_ref[:, g_cols], preferred_element_type=F32) + bgu_ref[:, g_cols]
            up = jnp.dot(x, wgu_bf_ref[:, u_cols], preferred_element_type=F32) + bgu_ref[:, u_cols]
            gate = jnp.minimum(gate, SWIGLU_LIMIT)
            up = jnp.clip(up, -SWIGLU_LIMIT, SWIGLU_LIMIT)
            hm_ref[:, g_cols] = ((up + 1.0) * (gate * jax.nn.sigmoid(SWIGLU_ALPHA * gate))
                                 ).astype(BF16)
        y = jnp.dot(hm_ref[...], wd_bf_ref[...], preferred_element_type=F32) + bd_ref[...]
        packed = _pack_halves(y)
        for s in range(PCHUNKS):
            y_ref[pl.ds(s, MOE_BM, stride=PCHUNKS), :] = packed[:, s * LANES:(s + 1) * LANES]


def _experts(block_exp, n_used, xs3, w_gu, b_gu, w_down, b_down):
    NB = block_exp.shape[0]
    used = lambda b, nu: jnp.minimum(b, nu[0] - 1)
    blocks = jnp.arange(NB, dtype=jnp.int32)
    after = jnp.minimum(jnp.sum((block_exp[None, :] <= block_exp[:, None]).astype(jnp.int32),
                                axis=1), NB - 1)
    next_exp = jnp.sum(jnp.where(blocks[None, :] == after[:, None], block_exp[None, :], 0),
                       axis=1).astype(jnp.int32)
    grid_spec = pltpu.PrefetchScalarGridSpec(
        num_scalar_prefetch=3,
        grid=(NB,),
        in_specs=[
            pl.BlockSpec((MOE_BM * PCHUNKS, LANES), lambda b, be, nu, nx: (used(b, nu), 0)),
            pl.BlockSpec(memory_space=pl.ANY),
            pl.BlockSpec((None, 1, 2 * D_FF), lambda b, be, nu, nx: (be[b], 0, 0)),
            pl.BlockSpec(memory_space=pl.ANY),
            pl.BlockSpec((None, 1, D_MODEL), lambda b, be, nu, nx: (be[b], 0, 0)),
        ],
        out_specs=pl.BlockSpec((MOE_BM * PCHUNKS, LANES), lambda b, be, nu, nx: (b, 0)),
        scratch_shapes=[
            pltpu.VMEM((MOE_BM, D_FF), BF16),
            pltpu.VMEM((D_MODEL, 2 * D_FF), F32),
            pltpu.VMEM((D_FF, D_MODEL), F32),
            pltpu.VMEM((D_MODEL, 2 * D_FF), BF16),
            pltpu.VMEM((D_FF, D_MODEL), BF16),
            pltpu.SemaphoreType.DMA((2,)),
        ],
    )
    return pl.pallas_call(
        _experts_kernel,
        grid_spec=grid_spec,
        out_shape=jax.ShapeDtypeStruct((NB * MOE_BM * PCHUNKS, LANES), U32),
        compiler_params=pltpu.CompilerParams(
            dimension_semantics=("arbitrary",), vmem_limit_bytes=VMEM_LIMIT),
        name="experts",
    )(block_exp, n_used, next_exp, xs3, w_gu, b_gu, w_down, b_down)


def _routing_tables(counts, top_idx, rank):
    M = top_idx.shape[0] * TOP_K
    NB = -(-M // MOE_BM) + N_EXPERTS
    nblk_e = (counts + MOE_BM - 1) // MOE_BM
    blk_end = jnp.cumsum(nblk_e)
    row_start = (blk_end - nblk_e) * MOE_BM
    n_used = blk_end[-1]
    blk = jnp.minimum(jnp.arange(NB, dtype=jnp.int32), n_used - 1)
    bexp = jnp.sum((blk[:, None] >= blk_end[None, :]).astype(jnp.int32), axis=1)
    experts = jnp.arange(N_EXPERTS, dtype=jnp.int32)
    dest = rank + jnp.sum(jnp.where(top_idx[:, :, None] == experts, row_start, 0), axis=-1)
    tail = NB - N_EXPERTS + experts
    fill_start = jnp.concatenate([row_start + counts, tail * MOE_BM])
    fill_rows = jnp.concatenate([nblk_e * MOE_BM - counts,
                                 jnp.where(tail >= n_used, MOE_BM, 0)])
    return (bexp.astype(jnp.int32), n_used.reshape(1).astype(jnp.int32),
            dest.astype(jnp.int32), fill_start.astype(jnp.int32),
            fill_rows.astype(jnp.int32), NB)


def _combine_kernel(dest_ref, dest_next_ref, dest_next2_ref, h_ref, gate_ref, g_ref, ys3_ref,
                    o_ref, buf_ref, sem_ref, *, tm):
    i = pl.program_id(0)
    n = pl.num_programs(0)
    slot = lax.rem(i, 3)
    slot_ahead = lax.rem(i + 2, 3)
    n_groups = tm * TOP_K // DMA_UNROLL
    groups_per_trip = n_groups * COMBINE_ROWS // tm

    def issue_group(idx_ref, s, g):
        m0 = pl.multiple_of(g * DMA_UNROLL, DMA_UNROLL)
        srcs = [idx_ref[0, 0, m0 + u] for u in range(DMA_UNROLL)]
        for u in range(DMA_UNROLL):
            r = g * (DMA_UNROLL // TOP_K) + u // TOP_K
            _rows_copy(ys3_ref, srcs[u], buf_ref.at[s], (u % TOP_K) * tm + r, 1,
                       sem_ref.at[s]).start(priority=u % DMA_PRIORITIES)

    def gather_wait(s):
        _rows_copy(ys3_ref, 0, buf_ref.at[s], 0, tm * TOP_K, sem_ref.at[s]).wait()

    @pl.when(i == 0)
    def _():
        def group(g, _):
            issue_group(dest_ref, 0, g)
            issue_group(dest_next_ref, 1, g)
            return 0
        lax.fori_loop(0, n_groups, group, 0)

    gather_wait(slot)

    def row_block(rb, _):
        r0 = pl.multiple_of(rb * COMBINE_ROWS, COMBINE_ROWS)
        rows = pl.ds(r0, COMBINE_ROWS)
        gates = gate_ref[rows, :]
        chunks = [None] * CHUNKS
        ssq = jnp.zeros((COMBINE_ROWS, 1), F32)
        for s in range(PCHUNKS):
            acc_lo = h_ref[rows, s * LANES:(s + 1) * LANES]
            acc_hi = h_ref[rows, HALF + s * LANES:HALF + (s + 1) * LANES]
            for k in range(TOP_K):
                lo, hi = _unpack_halves(buf_ref[
                    slot, pl.ds((k * tm + r0) * PCHUNKS + s, COMBINE_ROWS, stride=PCHUNKS), :])
                acc_lo = acc_lo + gates[:, k:k + 1] * lo
                acc_hi = acc_hi + gates[:, k:k + 1] * hi
            chunks[s], chunks[PCHUNKS + s] = acc_lo, acc_hi
            ssq = ssq + jnp.sum(acc_lo * acc_lo + acc_hi * acc_hi, axis=-1, keepdims=True)
        for gg in range(groups_per_trip):
            issue_group(dest_next2_ref, slot_ahead, rb * groups_per_trip + gg)
        inv = lax.rsqrt(ssq * (1.0 / D_MODEL) + EPS)
        for s in range(CHUNKS):
            cols = slice(s * LANES, (s + 1) * LANES)
            o_ref[rows, cols] = chunks[s] * inv * g_ref[:, cols]
        return 0
    lax.fori_loop(0, tm // COMBINE_ROWS, row_block, 0)

    @pl.when(i == n - 1)
    def _():
        gather_wait(lax.rem(i + 1, 3))
        gather_wait(slot_ahead)


def _combine(h, ys3, dest, gates, g_final):
    T = h.shape[0]
    tm = COMBINE_TILE
    n = T // tm
    table = dest.reshape(n, 1, tm * TOP_K)
    idx_spec = lambda f: pl.BlockSpec((1, 1, tm * TOP_K), lambda i: (f(i), 0, 0),
                                      memory_space=pltpu.SMEM)
    return pl.pallas_call(
        functools.partial(_combine_kernel, tm=tm),
        grid=(n,),
        in_specs=[
            idx_spec(lambda i: i), idx_spec(lambda i: jnp.minimum(i + 1, n - 1)),
            idx_spec(lambda i: jnp.minimum(i + 2, n - 1)),
            pl.BlockSpec((tm, D_MODEL), lambda i: (i, 0)),
            pl.BlockSpec((tm, LANES), lambda i: (i, 0)),
            pl.BlockSpec((1, D_MODEL), lambda i: (0, 0)),
            pl.BlockSpec(memory_space=pl.ANY),
        ],
        out_specs=pl.BlockSpec((tm, D_MODEL), lambda i: (i, 0)),
        out_shape=jax.ShapeDtypeStruct((T, D_MODEL), F32),
        scratch_shapes=[
            pltpu.VMEM((3, TOP_K * tm * PCHUNKS, LANES), U32),
            pltpu.SemaphoreType.DMA((3,)),
        ],
        compiler_params=pltpu.CompilerParams(
            dimension_semantics=("arbitrary",), vmem_limit_bytes=VMEM_LIMIT),
        name="combine",
    )(table, table, table, h, gates, g_final, ys3)


def kernel(x, mem, g_attn_norm, w_in, b_glu, w_dw, b_dw, g_cv_ln, b_cv_ln, w_pw2, b_pw2, g_mem, w_mem_kv, g_sb_out, g_cv_out, g_mx_out, w_out, g_ffn_norm, w_router, b_router, w_gu, b_gu, w_down, b_down, g_final):
    B, S, D = x.shape
    T = B * S
    assert D == D_MODEL and S % ROW_TILE == 0 and g_attn_norm.shape[0] == 1
    l = 0
    row = lambda v: v.reshape(1, -1)
    x2 = x.reshape(T, D)

    proj = _inproj(x2, row(g_attn_norm[l]), w_in[l].astype(BF16))
    proj3 = proj.reshape(B, S, IN_W)
    o_sb = _sb_attention(proj3)
    o_cv = _conformer(proj3, row(b_glu[l]), w_dw[l].reshape(CONV_K, CV_W), row(b_dw[l]),
                      row(g_cv_ln[l]), row(b_cv_ln[l]), w_pw2[l].astype(BF16), row(b_pw2[l]))
    o_mx = _memx(mem, row(g_mem[l]), w_mem_kv[l].astype(BF16), proj3)

    h, hn3, idx_pad, rank_pad, gate_pad, counts = _outproj(
        o_sb.reshape(T, SB_W), o_cv.reshape(T, CV_W), o_mx.reshape(T, MX_W), x2,
        row(g_sb_out[l]), row(g_cv_out[l]), row(g_mx_out[l]), w_out[l].astype(BF16),
        row(g_ffn_norm[l]), w_router[l], row(b_router[l]))

    bexp, n_used, dest, fill_start, fill_rows, n_rows = _routing_tables(
        counts[0].astype(jnp.int32), idx_pad[:, :TOP_K], rank_pad[:, :TOP_K])
    xs3 = _dispatch(fill_start, fill_rows, dest, hn3, n_rows)
    ys3 = _experts(bexp, n_used, xs3, w_gu[l], b_gu[l].reshape(N_EXPERTS, 1, 2 * D_FF),
                   w_down[l], b_down[l].reshape(N_EXPERTS, 1, D_MODEL))
    out = _combine(h, ys3, dest, gate_pad, row(g_final))
    return out.reshape(B, S, D)
```
